```python
import math
import jax, jax.numpy as jnp
from jax import lax
import numpy as np


D_MODEL = 2048
BATCH = 2
SEQ = 16384
DEPTH = 2

GRID_W = 64
CTX_LEN = 256
EPS = 1e-6
NEG_INF = -1e30
HEAD_DIM = 64

NA_WIDTH = D_MODEL // 4
NA_HEADS = NA_WIDTH // HEAD_DIM
NA_WIN_R = 8
NA_WIN_C = 16

GLA_WIDTH = D_MODEL // 4
GLA_HEADS = 4
GLA_DV = GLA_WIDTH // GLA_HEADS
GLA_DK = GLA_DV // 2
GLA_GATE_RANK = 16
GLA_NORMALIZER = 16.0
GLA_CHUNK = 64

DIFF_WIDTH = D_MODEL // 2
DIFF_DV = 2 * HEAD_DIM
DIFF_DK = HEAD_DIM
DIFF_HEADS = DIFF_WIDTH // DIFF_DV
ATTN_BLOCK = 128
ROPE_BASE = 10000.0

MIX_WIDTH = NA_WIDTH + GLA_WIDTH + DIFF_WIDTH
NA_IN = 3 * NA_WIDTH
GLA_IN = 2 * GLA_HEADS * GLA_DK + 2 * GLA_WIDTH + 2 * GLA_GATE_RANK
DIFF_IN = 2 * DIFF_HEADS * 2 * DIFF_DK + DIFF_WIDTH
IN_WIDTH = NA_IN + GLA_IN + DIFF_IN
MLP_HIDDEN = 4 * D_MODEL

kernel_name = 'hybrid_na_gla_diffattn_dit_block'


def rms_norm(x, g):
    xf = x.astype(jnp.float32)
    y = xf * lax.rsqrt(jnp.mean(xf * xf, axis=-1, keepdims=True) + EPS)
    return (y * g.astype(jnp.float32)).astype(x.dtype)


def split_groups(p):
    return jnp.split(p, [NA_IN, NA_IN + GLA_IN], axis=-1)


def axial_rope_tables(n_tokens, dim):
    t = jnp.arange(n_tokens, dtype=jnp.int32)
    row = (t // GRID_W).astype(jnp.float32)
    col = (t % GRID_W).astype(jnp.float32)
    n_freq = dim // 4
    inv = ROPE_BASE ** (-jnp.arange(n_freq, dtype=jnp.float32) / n_freq)
    ang = jnp.concatenate([row[:, None] * inv, col[:, None] * inv], axis=-1)
    return jnp.cos(ang), jnp.sin(ang)


def apply_rope(x, cos, sin):
    half = x.shape[-1] // 2
    shp = (1, cos.shape[0]) + (1,) * (x.ndim - 3) + (half,)
    cs, sn = cos.reshape(shp), sin.reshape(shp)
    xf = x.astype(jnp.float32)
    x1, x2 = xf[..., :half], xf[..., half:]
    return jnp.concatenate([x1 * cs - x2 * sn, x1 * sn + x2 * cs], axis=-1).astype(x.dtype)


def dense_softmax_attention(q, k, v):
    s = jnp.einsum('bnhd,bmhd->bhnm', q, k).astype(jnp.float32) * (q.shape[-1] ** -0.5)
    p = jax.nn.softmax(s, axis=-1).astype(v.dtype)
    return jnp.einsum('bhnm,bmhd->bnhd', p, v)


def neighbourhood_attention(p_lat, p_ctx, rel_bias, need_ctx_out):
    B, T, _ = p_lat.shape
    N = p_ctx.shape[1]
    H, d = NA_HEADS, HEAD_DIM
    rows = T // GRID_W
    wr, wc = min(NA_WIN_R, rows), NA_WIN_C
    q, k, v = [z.reshape(B, T, H, d) for z in jnp.split(p_lat, 3, axis=-1)]
    qc, kc, vc = [z.reshape(B, N, H, d) for z in jnp.split(p_ctx, 3, axis=-1)]
    r = jnp.arange(rows, dtype=jnp.int32)
    col = jnp.arange(GRID_W, dtype=jnp.int32)
    key_rows = jnp.clip(r - wr // 2, 0, rows - wr)[:, None] + jnp.arange(wr, dtype=jnp.int32)[None, :]
    c0 = jnp.clip(col - wc // 2, 0, GRID_W - wc)
    col_ok = (col[None, :] >= c0[:, None]) & (col[None, :] < c0[:, None] + wc)
    dr = key_rows - r[:, None] + (NA_WIN_R - 1)
    dc = jnp.clip(col[None, :] - col[:, None] + (NA_WIN_C - 1), 0, 2 * NA_WIN_C - 2)
    bias = rel_bias[:, dr[:, None, :, None], dc[None, :, None, :]].astype(jnp.float32)
    qg = q.reshape(B, rows, GRID_W, H, d)
    kg = k.reshape(B, rows, GRID_W, H, d)[:, key_rows]
    vg = v.reshape(B, rows, GRID_W, H, d)[:, key_rows]
    scale = d ** -0.5
    s_loc = jnp.einsum('brqhd,brjkhd->bhrqjk', qg, kg).astype(jnp.float32) * scale + bias[None]
    s_loc = jnp.where(col_ok[:, None, :], s_loc, NEG_INF).reshape(B, H, rows, GRID_W, wr * GRID_W)
    s_ctx = jnp.einsum('brqhd,bnhd->bhrqn', qg, kc).astype(jnp.float32) * scale
    p = jax.nn.softmax(jnp.concatenate([s_loc, s_ctx], axis=-1), axis=-1).astype(v.dtype)
    p_loc = p[..., : wr * GRID_W].reshape(B, H, rows, GRID_W, wr, GRID_W)
    p_ctx = p[..., wr * GRID_W:]
    o = jnp.einsum('bhrqjk,brjkhd->brqhd', p_loc, vg) + jnp.einsum('bhrqn,bnhd->brqhd', p_ctx, vc)
    o = o.reshape(B, T, H * d)
    oc = dense_softmax_attention(qc, kc, vc).reshape(B, N, H * d) if need_ctx_out else None
    return o, oc


def gla_split(p, w_gate, b_gate):
    B, n, _ = p.shape
    hk = GLA_HEADS * GLA_DK
    q, k, v, g, a = jnp.split(p, [hk, 2 * hk, 2 * hk + GLA_WIDTH, 2 * hk + 2 * GLA_WIDTH], axis=-1)
    q = q.reshape(B, n, GLA_HEADS, GLA_DK) * (GLA_DK ** -0.5)
    k = k.reshape(B, n, GLA_HEADS, GLA_DK)
    v = v.reshape(B, n, GLA_HEADS, GLA_DV)
    a = a.reshape(B, n, 2, GLA_GATE_RANK)
    pre = jnp.einsum('bnzr,zrk->bnzk', a, w_gate).astype(jnp.float32) + b_gate.astype(jnp.float32)
    log_a = (jax.nn.log_sigmoid(pre) / GLA_NORMALIZER).reshape(B, n, 2, GLA_HEADS, GLA_DK)
    return q, k, v, g, log_a[:, :, 0], log_a[:, :, 1]


def gla_chunked(q, k, v, log_a, s0, strict):
    B, T, H, dk = q.shape
    dv = v.shape[-1]
    n = T // GLA_CHUNK

    def blocks(z):
        return z.astype(jnp.float32).reshape(B, n, GLA_CHUNK, H, z.shape[-1]).transpose(1, 0, 3, 2, 4)

    mask = jnp.tril(jnp.ones((GLA_CHUNK, GLA_CHUNK), dtype=bool), k=-1 if strict else 0)

    def step(S, inp):
        qc, kc, vc, gc = inp
        b = jnp.cumsum(gc, axis=-2)
        b_last = b[..., -1:, :]
        q_in = qc * jnp.exp(b)
        att = jnp.where(mask, jnp.einsum('bhtd,bhsd->bhts', q_in, kc * jnp.exp(-b)), 0.0)
        o = jnp.einsum('bhts,bhsv->bhtv', att, vc) + jnp.einsum('bhtd,bhdv->bhtv', q_in, S)
        S_new = jnp.exp(b_last[..., 0, :])[..., None] * S + jnp.einsum('bhsd,bhsv->bhdv', kc * jnp.exp(b_last - b), vc)
        return S_new, o

    S, o = lax.scan(step, s0, (blocks(q), blocks(k), blocks(v), blocks(log_a)))
    return o.transpose(1, 0, 3, 2, 4).reshape(B, T, H, dv), S


def gla_final_state(k, v, log_a):
    b = jnp.cumsum(log_a.astype(jnp.float32), axis=1)
    w = jnp.exp(b[:, -1:] - b)
    return jnp.einsum('bthd,bthv->bhdv', k.astype(jnp.float32) * w, v.astype(jnp.float32))


def gla_output(o, g, g_norm):
    B, n = g.shape[:2]
    o = rms_norm(o, g_norm) * jax.nn.silu(g.reshape(B, n, GLA_HEADS, GLA_DV).astype(jnp.float32))
    return o.astype(g.dtype).reshape(B, n, GLA_WIDTH)


def gla_mixer(p_lat, p_ctx, w_gate, b_gate, g_norm, need_ctx_out):
    q, k, v, g, la_f, la_b = gla_split(p_lat, w_gate, b_gate)
    qc, kc, vc, gc, lac_f, lac_b = gla_split(p_ctx, w_gate, b_gate)
    flip = lambda z: jnp.flip(z, axis=1)
    B = q.shape[0]
    if need_ctx_out:
        zeros = jnp.zeros((B, GLA_HEADS, GLA_DK, GLA_DV), jnp.float32)
        oc_f, S_f = gla_chunked(qc, kc, vc, lac_f, zeros, False)
        oc_b, S_b = gla_chunked(flip(qc), flip(kc), flip(vc), flip(lac_b), zeros, True)
        oc = gla_output(oc_f + flip(oc_b), gc, g_norm)
    else:
        S_f = gla_final_state(kc, vc, lac_f)
        S_b = gla_final_state(flip(kc), flip(vc), flip(lac_b))
        oc = None
    o_f, _ = gla_chunked(q, k, v, la_f, S_f, False)
    o_b, _ = gla_chunked(flip(q), flip(k), flip(v), flip(la_b), S_b, True)
    return gla_output(o_f + flip(o_b), g, g_norm), oc


def diff_attention(p_lat, p_ctx, lam_params, g_norm, layer_idx, need_ctx_out):
    B, T, _ = p_lat.shape
    N = p_ctx.shape[1]
    H, dk, dv = DIFF_HEADS, DIFF_DK, DIFF_DV
    qk_w = H * 2 * dk
    lam_init = 0.8 - 0.6 * math.exp(-0.3 * layer_idx)
    lp = lam_params.astype(jnp.float32)
    lam = jnp.exp(jnp.sum(lp[0] * lp[1])) - jnp.exp(jnp.sum(lp[2] * lp[3])) + lam_init

    def split(p, n):
        q, k, v = jnp.split(p, [qk_w, 2 * qk_w], axis=-1)
        return q.reshape(B, n, H, 2, dk), k.reshape(B, n, H, 2, dk), v.reshape(B, n, H, dv)

    q, k, v = split(p_lat, T)
    qc, kc, vc = split(p_ctx, N)
    cos, sin = axial_rope_tables(T, dk)
    q, k = apply_rope(q, cos, sin), apply_rope(k, cos, sin)
    scale = dk ** -0.5

    def diff_weights(s):
        p = jax.nn.softmax(s * scale, axis=-1)
        return (p[:, :, 0] - lam * p[:, :, 1]).astype(v.dtype)

    def attend(q_blk):
        s = jnp.concatenate([jnp.einsum('bqhpd,bkhpd->bhpqk', q_blk, k).astype(jnp.float32),
                             jnp.einsum('bqhpd,bnhpd->bhpqn', q_blk, kc).astype(jnp.float32)], axis=-1)
        a = diff_weights(s)
        return jnp.einsum('bhqk,bkhv->bqhv', a[..., :T], v) + jnp.einsum('bhqn,bnhv->bqhv', a[..., T:], vc)

    n_blk = T // ATTN_BLOCK
    q_blocks = q.reshape(B, n_blk, ATTN_BLOCK, H, 2, dk).transpose(1, 0, 2, 3, 4, 5)
    o = lax.map(attend, q_blocks).transpose(1, 0, 2, 3, 4).reshape(B, T, H, dv)
    o = (rms_norm(o, g_norm) * (1.0 - lam_init)).reshape(B, T, H * dv)
    if need_ctx_out:
        a = diff_weights(jnp.einsum('bnhpd,bmhpd->bhpnm', qc, kc).astype(jnp.float32))
        oc = jnp.einsum('bhnm,bmhv->bnhv', a, vc)
        oc = (rms_norm(oc, g_norm) * (1.0 - lam_init)).reshape(B, N, H * dv)
    else:
        oc = None
    return o, oc


def sq_relu_mlp(h, w1, w2):
    return jnp.square(jax.nn.relu(h @ w1)) @ w2


def setup_inputs(seed: int = 0) -> dict:
    key = jax.random.key(seed)
    ks = jax.random.split(key, 20)
    D = D_MODEL

    def nrm(k, shape, scale):
        return jax.random.normal(k, shape, jnp.float32) * scale

    return {
        'x': nrm(ks[0], (BATCH, SEQ, D), 1.0),
        'c': nrm(ks[1], (BATCH, D), 1.0),
        'ctx': nrm(ks[2], (BATCH, CTX_LEN, D), 1.0),
        'c_ctx': nrm(ks[3], (D,), 1.0),
        'w_ada': nrm(ks[4], (DEPTH, D, 6 * D), 0.5 * D ** -0.5),
        'b_ada': nrm(ks[5], (DEPTH, 6 * D), 0.02),
        'g_pre_mix': 1.0 + nrm(ks[6], (DEPTH, D), 0.05),
        'g_post_mix': 1.0 + nrm(ks[7], (DEPTH, D), 0.05),
        'g_pre_mlp': 1.0 + nrm(ks[8], (DEPTH, D), 0.05),
        'g_post_mlp': 1.0 + nrm(ks[9], (DEPTH, D), 0.05),
        'w_in': nrm(ks[10], (DEPTH, D, IN_WIDTH), D ** -0.5),
        'w_out': nrm(ks[11], (DEPTH, MIX_WIDTH, D), MIX_WIDTH ** -0.5),
        'na_rel_bias': nrm(ks[12], (DEPTH, NA_HEADS, 2 * NA_WIN_R - 1, 2 * NA_WIN_C - 1), 0.5),
        'gla_w_gate': nrm(ks[13], (DEPTH, 2, GLA_GATE_RANK, GLA_HEADS * GLA_DK), GLA_GATE_RANK ** -0.5),
        'gla_b_gate': nrm(ks[14], (DEPTH, 2, GLA_HEADS * GLA_DK), 0.1),
        'gla_g_norm': 1.0 + nrm(ks[15], (DEPTH, GLA_DV), 0.05),
        'diff_lambda': nrm(ks[16], (DEPTH, 4, DIFF_DK), 0.1),
        'diff_g_norm': 1.0 + nrm(ks[17], (DEPTH, DIFF_DV), 0.05),
        'w_mlp_in': nrm(ks[18], (DEPTH, D, MLP_HIDDEN), D ** -0.5),
        'w_mlp_out': nrm(ks[19], (DEPTH, MLP_HIDDEN, D), MLP_HIDDEN ** -0.5),
    }


def reference(x, c, ctx, c_ctx, w_ada, b_ada, g_pre_mix, g_post_mix, g_pre_mlp, g_post_mlp,
              w_in, w_out, na_rel_bias, gla_w_gate, gla_b_gate, gla_g_norm, diff_lambda,
              diff_g_norm, w_mlp_in, w_mlp_out):
    B, T, D = x.shape
    h_ctx = ctx
    for l in range(DEPTH):
        need_ctx = l < DEPTH - 1
        mx = (jax.nn.silu(c) @ w_ada[l] + b_ada[l]).reshape(B, 6, 1, D)
        mc = (jax.nn.silu(c_ctx) @ w_ada[l] + b_ada[l]).reshape(6, D)
        hx = rms_norm(x, g_pre_mix[l]) * (1 + mx[:, 1]) + mx[:, 0]
        hc = rms_norm(h_ctx, g_pre_mix[l]) * (1 + mc[1]) + mc[0]
        na_x, gla_x, diff_x = split_groups(hx @ w_in[l])
        na_c, gla_c, diff_c = split_groups(hc @ w_in[l])
        ya, ya_c = neighbourhood_attention(na_x, na_c, na_rel_bias[l], need_ctx)
        yb, yb_c = gla_mixer(gla_x, gla_c, gla_w_gate[l], gla_b_gate[l], gla_g_norm[l], need_ctx)
        yd, yd_c = diff_attention(diff_x, diff_c, diff_lambda[l], diff_g_norm[l], l, need_ctx)
        y = jnp.concatenate([ya, yb, yd], axis=-1) @ w_out[l]
        x = x + mx[:, 2] * rms_norm(y, g_post_mix[l])
        hx = rms_norm(x, g_pre_mlp[l]) * (1 + mx[:, 4]) + mx[:, 3]
        x = x + mx[:, 5] * rms_norm(sq_relu_mlp(hx, w_mlp_in[l], w_mlp_out[l]), g_post_mlp[l])
        if need_ctx:
            yc = jnp.concatenate([ya_c, yb_c, yd_c], axis=-1) @ w_out[l]
            h_ctx = h_ctx + mc[2] * rms_norm(yc, g_post_mix[l])
            hc = rms_norm(h_ctx, g_pre_mlp[l]) * (1 + mc[4]) + mc[3]
            h_ctx = h_ctx + mc[5] * rms_norm(sq_relu_mlp(hc, w_mlp_in[l], w_mlp_out[l]), g_post_mlp[l])
    return x
```

```python
import functools
import math

import jax
import jax.numpy as jnp
from jax import lax
from jax.experimental import pallas as pl
from jax.experimental.pallas import tpu as pltpu

F32 = jnp.float32
BF16 = jnp.bfloat16

GRID_W = 64
EPS = 1e-6
NEG_INF = -1e30
HEAD_DIM = 64
NA_WIN_R = 8
NA_WIN_C = 16
GLA_HEADS = 4
GLA_DK = 64
GLA_DV = 128
GLA_GATE_RANK = 16
GLA_NORMALIZER = 16.0
GLA_CHUNK = 64
DIFF_DK = 64
DIFF_DV = 128
ROPE_BASE = 10000.0

LANES = 128
VMEM_LIMIT = 56 * 1024 * 1024

NA_ROWS_PER_STEP = 8
NA_KEY_ROWS = NA_ROWS_PER_STEP + NA_WIN_R


def _params(*sem):
    return pltpu.CompilerParams(dimension_semantics=sem, vmem_limit_bytes=VMEM_LIMIT)


def _nt_dot(a, b):
    return lax.dot_general(a, b, (((1,), (1,)), ((), ())), preferred_element_type=F32)


def _rms(y):
    return y * lax.rsqrt(jnp.mean(y * y, axis=-1, keepdims=True) + EPS)


def _mods_kernel(c_ref, w_ref, b_ref, o_ref):
    c = c_ref[...]
    s = (c * jax.nn.sigmoid(c)).astype(BF16)
    o_ref[0] = jnp.dot(s, w_ref[0].astype(BF16), preferred_element_type=F32) + b_ref[0]


def _mods(cc, w_ada, b_ada):
    depth, d, n = w_ada.shape
    tn = 1024
    return pl.pallas_call(
        _mods_kernel,
        out_shape=jax.ShapeDtypeStruct((depth, 8, n), F32),
        grid=(depth, n // tn),
        in_specs=[
            pl.BlockSpec((8, d), lambda l, j: (0, 0)),
            pl.BlockSpec((1, d, tn), lambda l, j: (l, 0, j)),
            pl.BlockSpec((1, 1, tn), lambda l, j: (l, 0, j)),
        ],
        out_specs=pl.BlockSpec((1, 8, tn), lambda l, j: (l, 0, j)),
        compiler_params=_params("arbitrary", "arbitrary"),
        name="ada_mods",
    )(cc, w_ada, b_ada.reshape(depth, 1, n))


def _inproj_kernel(x_ref, g_ref, shift_ref, scale_ref, w_ref, *rest, rope):
    if rope:
        cos_ref, sin_ref, o_ref, h_scr = rest
    else:
        o_ref, h_scr = rest

    @pl.when(pl.program_id(2) == 0)
    def _():
        y = _rms(x_ref[0]) * g_ref[...]
        h_scr[...] = (y * (1.0 + scale_ref[0]) + shift_ref[0]).astype(BF16)

    acc = jnp.dot(h_scr[...], w_ref[...], preferred_element_type=F32)
    if not rope:
        o_ref[0] = acc.astype(BF16)
        return
    cos = cos_ref[...]
    sin = sin_ref[...]
    lower = (lax.broadcasted_iota(jnp.int32, cos.shape, 1) % DIFF_DK) < DIFF_DK // 2
    for c in range(acc.shape[1] // LANES):
        a = acc[:, c * LANES:(c + 1) * LANES]
        partner = jnp.where(lower, pltpu.roll(a, LANES - DIFF_DK // 2, 1), pltpu.roll(a, DIFF_DK // 2, 1))
        o_ref[0, :, c * LANES:(c + 1) * LANES] = (a * cos + partner * sin).astype(BF16)


def _inproj(x, g, mods, mod_row0, w, tm, tn, rope_tabs=None):
    b, t, d = x.shape
    n = w.shape[1]
    rope = rope_tabs is not None
    in_specs = [
        pl.BlockSpec((1, tm, d), lambda bi, i, j: (bi, i, 0)),
        pl.BlockSpec((1, d), lambda bi, i, j: (0, 0)),
        pl.BlockSpec((1, 1, d), lambda bi, i, j: ((bi + mod_row0) * 6 + 0, 0, 0)),
        pl.BlockSpec((1, 1, d), lambda bi, i, j: ((bi + mod_row0) * 6 + 1, 0, 0)),
        pl.BlockSpec((d, tn), lambda bi, i, j: (0, j)),
    ]
    args = [x, g, mods, mods, w]
    if rope:
        in_specs += [pl.BlockSpec((tm, LANES), lambda bi, i, j: (i, 0))] * 2
        args += list(rope_tabs)
    return pl.pallas_call(
        functools.partial(_inproj_kernel, rope=rope),
        out_shape=jax.ShapeDtypeStruct((b, t, n), BF16),
        grid=(b, t // tm, n // tn),
        in_specs=in_specs,
        out_specs=pl.BlockSpec((1, tm, tn), lambda bi, i, j: (bi, i, j)),
        scratch_shapes=[pltpu.VMEM((tm, d), BF16)],
        compiler_params=_params("parallel", "parallel", "arbitrary"),
        name="inproj_rope" if rope else "inproj",
    )(*args)


def _na_bias_table(rel_bias, rows):
    rq, kwin = NA_ROWS_PER_STEP, NA_KEY_ROWS
    nb = rows // rq
    h = rel_bias.shape[0]
    blocks = jnp.array([0, min(1, nb - 1), nb - 1], jnp.int32)
    r = blocks[:, None] * rq + jnp.arange(rq, dtype=jnp.int32)[None, :]
    kst = jnp.clip(blocks * rq - NA_WIN_R // 2, 0, rows - kwin)
    kr = kst[:, None] + jnp.arange(kwin, dtype=jnp.int32)[None, :]
    s_r = jnp.clip(r - NA_WIN_R // 2, 0, rows - NA_WIN_R)
    row_ok = (kr[:, None, :] >= s_r[:, :, None]) & (kr[:, None, :] < s_r[:, :, None] + NA_WIN_R)
    dr = jnp.clip(kr[:, None, :] - r[:, :, None] + (NA_WIN_R - 1), 0, 2 * NA_WIN_R - 2)
    col = jnp.arange(GRID_W, dtype=jnp.int32)
    c0 = jnp.clip(col - NA_WIN_C // 2, 0, GRID_W - NA_WIN_C)
    col_ok = (col[None, :] >= c0[:, None]) & (col[None, :] < c0[:, None] + NA_WIN_C)
    dc = jnp.clip(col[None, :] - col[:, None] + (NA_WIN_C - 1), 0, 2 * NA_WIN_C - 2)
    bias = rel_bias.astype(F32)[:, dr[:, :, None, :, None], dc[None, None, :, None, :]]
    ok = row_ok[:, :, None, :, None] & col_ok[None, None, :, None, :]
    bias = jnp.where(ok[None], bias, NEG_INF)
    return bias.transpose(1, 0, 2, 3, 4, 5).reshape(3, h, rq * GRID_W, kwin * GRID_W)


def _na_kernel(q_ref, k_ref, v_ref, kc_ref, vc_ref, bias_ref, o_ref, *, rows):
    i = pl.program_id(2)
    kstart = jnp.clip(i * NA_ROWS_PER_STEP - NA_WIN_R // 2, 0, rows - NA_KEY_ROWS)
    off = pl.multiple_of(kstart * GRID_W, GRID_W)
    kw = k_ref[0, pl.ds(off, NA_KEY_ROWS * GRID_W), :]
    vw = v_ref[0, pl.ds(off, NA_KEY_ROWS * GRID_W), :]
    q = q_ref[0]
    kc = kc_ref[0]
    vc = vc_ref[0]
    scale = HEAD_DIM ** -0.5
    lane = lax.broadcasted_iota(jnp.int32, q.shape, 1)
    outs = []
    for hh in range(LANES // HEAD_DIM):
        own = (lane >= hh * HEAD_DIM) & (lane < (hh + 1) * HEAD_DIM)
        qh = jnp.where(own, q, 0.0).astype(BF16)
        s_loc = _nt_dot(qh, kw) * scale + bias_ref[0, hh]
        s_ctx = _nt_dot(qh, kc) * scale
        m = jnp.maximum(jnp.max(s_loc, axis=-1, keepdims=True), jnp.max(s_ctx, axis=-1, keepdims=True))
        p_loc = jnp.exp(s_loc - m)
        p_ctx = jnp.exp(s_ctx - m)
        l = jnp.sum(p_loc, axis=-1, keepdims=True) + jnp.sum(p_ctx, axis=-1, keepdims=True)
        o = jnp.dot(p_loc.astype(BF16), vw, preferred_element_type=F32)
        o = o + jnp.dot(p_ctx.astype(BF16), vc, preferred_element_type=F32)
        outs.append(o / l)
    o_ref[0] = jnp.where(lane < HEAD_DIM, outs[0], outs[1]).astype(BF16)


def _na_attention(p_lat, p_ctx, bias_tbl):
    b, t, _ = p_lat.shape
    n = p_ctx.shape[1]
    rows = t // GRID_W
    nb = rows // NA_ROWS_PER_STEP
    hp = bias_tbl.shape[1] // 2
    tq = NA_ROWS_PER_STEP * GRID_W

    def variant(i):
        return jnp.where(i == 0, 0, jnp.where(i == nb - 1, 2, 1))

    return pl.pallas_call(
        functools.partial(_na_kernel, rows=rows),
        out_shape=jax.ShapeDtypeStruct((b, t, hp * LANES), BF16),
        grid=(b, hp, nb),
        in_specs=[
            pl.BlockSpec((1, tq, LANES), lambda bi, h, i: (bi, i, h)),
            pl.BlockSpec((1, t, LANES), lambda bi, h, i: (bi, 0, hp + h)),
            pl.BlockSpec((1, t, LANES), lambda bi, h, i: (bi, 0, 2 * hp + h)),
            pl.BlockSpec((1, n, LANES), lambda bi, h, i: (bi, 0, hp + h)),
            pl.BlockSpec((1, n, LANES), lambda bi, h, i: (bi, 0, 2 * hp + h)),
            pl.BlockSpec((1, 2, tq, NA_KEY_ROWS * GRID_W), lambda bi, h, i: (variant(i), h, 0, 0)),
        ],
        out_specs=pl.BlockSpec((1, tq, LANES), lambda bi, h, i: (bi, i, h)),
        compiler_params=_params("parallel", "parallel", "arbitrary"),
        name="na_attention",
    )(p_lat, p_lat, p_lat, p_ctx, p_ctx, bias_tbl)


def _ctx_attn_kernel(q_ref, k_ref, v_ref, o_ref):
    q, k, v = q_ref[0], k_ref[0], v_ref[0]
    scale = HEAD_DIM ** -0.5
    lane = lax.broadcasted_iota(jnp.int32, q.shape, 1)
    outs = []
    for hh in range(LANES // HEAD_DIM):
        own = (lane >= hh * HEAD_DIM) & (lane < (hh + 1) * HEAD_DIM)
        s = _nt_dot(jnp.where(own, q, 0.0).astype(BF16), k) * scale
        p = jnp.exp(s - jnp.max(s, axis=-1, keepdims=True))
        l = jnp.sum(p, axis=-1, keepdims=True)
        outs.append(jnp.dot(p.astype(BF16), v, preferred_element_type=F32) / l)
    o_ref[0] = jnp.where(lane < HEAD_DIM, outs[0], outs[1]).astype(BF16)


def _ctx_attention(p_ctx, hp):
    b, n, _ = p_ctx.shape
    return pl.pallas_call(
        _ctx_attn_kernel,
        out_shape=jax.ShapeDtypeStruct((b, n, hp * LANES), BF16),
        grid=(b, hp),
        in_specs=[
            pl.BlockSpec((1, n, LANES), lambda bi, h: (bi, 0, h)),
            pl.BlockSpec((1, n, LANES), lambda bi, h: (bi, 0, hp + h)),
            pl.BlockSpec((1, n, LANES), lambda bi, h: (bi, 0, 2 * hp + h)),
        ],
        out_specs=pl.BlockSpec((1, n, LANES), lambda bi, h: (bi, 0, h)),
        compiler_params=_params("parallel", "parallel"),
        name="ctx_attention",
    )(p_ctx, p_ctx, p_ctx)


def _gla_kernel(*refs, backward, nc):
    if backward:
        (q_ref, k_ref, v_ref, a_ref, wg_ref, bg_ref, s0_ref, of_ref, g_ref, gn_ref,
         o_ref, sout_ref, s_scr) = refs
    else:
        q_ref, k_ref, v_ref, a_ref, wg_ref, bg_ref, s0_ref, o_ref, sout_ref, s_scr = refs
    j = pl.program_id(1)
    tb = nc * GLA_CHUNK
    kw = GLA_HEADS * GLA_DK

    @pl.when(j == 0)
    def _():
        s_scr[...] = s0_ref[0]

    pre = jnp.dot(a_ref[0], wg_ref[...], preferred_element_type=F32) + bg_ref[...]
    log_a = (jnp.minimum(pre, 0.0) - jnp.log1p(jnp.exp(-jnp.abs(pre)))) * (1.0 / GLA_NORMALIZER)

    pos = lax.broadcasted_iota(jnp.int32, (tb, kw), 0) % GLA_CHUNK
    bsum = log_a
    step = 1
    while step < GLA_CHUNK:
        if backward:
            bsum = bsum + jnp.where(pos < GLA_CHUNK - step, pltpu.roll(bsum, tb - step, 0), 0.0)
        else:
            bsum = bsum + jnp.where(pos >= step, pltpu.roll(bsum, step, 0), 0.0)
        step *= 2

    b3 = bsum.reshape(nc, GLA_CHUNK, kw)
    b_end = b3[:, 0:1, :] if backward else b3[:, GLA_CHUNK - 1:GLA_CHUNK, :]
    q3 = q_ref[0].astype(F32).reshape(nc, GLA_CHUNK, kw)
    k3 = k_ref[0].astype(F32).reshape(nc, GLA_CHUNK, kw)
    q_in = q3 * (GLA_DK ** -0.5) * jnp.exp(b3)
    k_out = (k3 * jnp.exp(-b3)).astype(BF16)
    k_dec = (k3 * jnp.exp(b_end - b3)).astype(BF16)
    decay = jnp.exp(b_end)
    v3 = v_ref[0].reshape(nc, GLA_CHUNK, GLA_HEADS * GLA_DV)

    t_idx = lax.broadcasted_iota(jnp.int32, (nc, GLA_CHUNK, GLA_CHUNK), 1)
    s_idx = lax.broadcasted_iota(jnp.int32, (nc, GLA_CHUNK, GLA_CHUNK), 2)
    tri = (s_idx > t_idx) if backward else (s_idx <= t_idx)
    eye = (lax.broadcasted_iota(jnp.int32, (nc, GLA_DV, GLA_DV), 1)
           == lax.broadcasted_iota(jnp.int32, (nc, GLA_DV, GLA_DV), 2)).astype(BF16)
    lane = lax.broadcasted_iota(jnp.int32, (nc, GLA_CHUNK, LANES), 2)
    order = range(nc - 1, -1, -1) if backward else range(nc)

    outs = []
    for h in range(GLA_HEADS):
        pair = slice((h // 2) * LANES, (h // 2 + 1) * LANES)
        own = (lane >= GLA_DK) if h % 2 else (lane < GLA_DK)
        qm = jnp.where(own, q_in[:, :, pair], 0.0).astype(BF16)
        vh = v3[:, :, h * GLA_DV:(h + 1) * GLA_DV]
        att = jnp.einsum('ctd,csd->cts', qm, k_out[:, :, pair], preferred_element_type=F32)
        att = jnp.where(tri, att, 0.0).astype(BF16)
        o_h = jnp.einsum('cts,csv->ctv', att, vh, preferred_element_type=F32)
        vt = jnp.einsum('cvw,csw->cvs', eye, vh, preferred_element_type=F32).astype(BF16)
        ut = jnp.einsum('cvs,csd->cvd', vt, k_dec[:, :, pair], preferred_element_type=F32)
        s_h = s_scr[h]
        before = [None] * nc
        for c in order:
            before[c] = s_h
            s_h = s_h * decay[c, :, pair] + ut[c]
        s_scr[h] = s_h
        s_all = jnp.stack(before, axis=0).astype(BF16)
        o_h = o_h + jnp.einsum('ctd,cvd->ctv', qm, s_all, preferred_element_type=F32)
        outs.append(o_h.reshape(tb, GLA_DV))

    @pl.when(j == pl.num_programs(1) - 1)
    def _():
        sout_ref[0] = s_scr[...]

    if not backward:
        o_ref[0] = jnp.concatenate(outs, axis=-1)
        return
    of = of_ref[0]
    g = g_ref[0].astype(F32)
    fin = []
    for h in range(GLA_HEADS):
        sl = slice(h * GLA_DV, (h + 1) * GLA_DV)
        y = _rms(of[:, sl] + outs[h]) * gn_ref[...]
        gh = g[:, sl]
        fin.append(y * (gh * jax.nn.sigmoid(gh)))
    o_ref[0] = jnp.concatenate(fin, axis=-1).astype(BF16)


def _gla_sweep(p, wg, bg, s0, nc, col0, a_col, backward, o_fwd=None, g_norm=None):
    b, n, _ = p.shape
    tb = nc * GLA_CHUNK
    nblk = n // tb
    kw = GLA_HEADS * GLA_DK
    vw = GLA_HEADS * GLA_DV
    cq, ck, cv, cg, ca = col0 // kw, col0 // kw + 1, (col0 + 2 * kw) // vw, (col0 + 2 * kw) // vw + 1, a_col // LANES

    def blk(j):
        return nblk - 1 - j if backward else j

    in_specs = [
        pl.BlockSpec((1, tb, kw), lambda bi, j: (bi, blk(j), cq)),
        pl.BlockSpec((1, tb, kw), lambda bi, j: (bi, blk(j), ck)),
        pl.BlockSpec((1, tb, vw), lambda bi, j: (bi, blk(j), cv)),
        pl.BlockSpec((1, tb, LANES), lambda bi, j: (bi, blk(j), ca)),
        pl.BlockSpec((LANES, kw), lambda bi, j: (0, 0)),
        pl.BlockSpec((1, kw), lambda bi, j: (0, 0)),
        pl.BlockSpec((1, GLA_HEADS, GLA_DV, LANES), lambda bi, j: (bi, 0, 0, 0)),
    ]
    args = [p, p, p, p, wg, bg, s0]
    if backward:
        in_specs += [
            pl.BlockSpec((1, tb, vw), lambda bi, j: (bi, blk(j), 0)),
            pl.BlockSpec((1, tb, vw), lambda bi, j: (bi, blk(j), cg)),
            pl.BlockSpec((1, GLA_DV), lambda bi, j: (0, 0)),
        ]
        args += [o_fwd, p, g_norm]
    return pl.pallas_call(
        functools.partial(_gla_kernel, backward=backward, nc=nc),
        out_shape=(jax.ShapeDtypeStruct((b, n, vw), BF16 if backward else F32),
                   jax.ShapeDtypeStruct((b, GLA_HEADS, GLA_DV, LANES), F32)),
        grid=(b, nblk),
        in_specs=in_specs,
        out_specs=(pl.BlockSpec((1, tb, vw), lambda bi, j: (bi, blk(j), 0)),
                   pl.BlockSpec((1, GLA_HEADS, GLA_DV, LANES), lambda bi, j: (bi, 0, 0, 0))),
        scratch_shapes=[pltpu.VMEM((GLA_HEADS, GLA_DV, LANES), F32)],
        compiler_params=_params("parallel", "arbitrary"),
        name="gla_bwd" if backward else "gla_fwd",
    )(*args)


def _diff_kernel(*refs, extra, lam_init):
    if extra:
        q_ref, k_ref, v_ref, kc_ref, vc_ref, lam_ref, gn_ref, o_ref, qs, m_scr, l_scr, acc = refs
    else:
        q_ref, k_ref, v_ref, lam_ref, gn_ref, o_ref, qs, m_scr, l_scr, acc = refs
    j = pl.program_id(3)
    tq = q_ref.shape[1]

    def update(k, v):
        s = _nt_dot(qs[...], k)
        m_old = m_scr[...]
        m_new = jnp.maximum(m_old, jnp.max(s, axis=-1, keepdims=True))
        alpha = jnp.exp(m_old - m_new)
        p = jnp.exp(s - m_new)
        l_scr[...] = alpha * l_scr[...] + jnp.sum(p, axis=-1, keepdims=True)
        acc[...] = alpha * acc[...] + jnp.dot(p.astype(BF16), v, preferred_element_type=F32)
        m_scr[...] = m_new

    @pl.when(j == 0)
    def _():
        q = q_ref[0] * (DIFF_DK ** -0.5)
        lane = lax.broadcasted_iota(jnp.int32, q.shape, 1)
        qs[0:tq, :] = jnp.where(lane < DIFF_DK, q, 0.0).astype(BF16)
        qs[tq:2 * tq, :] = jnp.where(lane >= DIFF_DK, q, 0.0).astype(BF16)
        m_scr[...] = jnp.full(m_scr.shape, -jnp.inf, F32)
        l_scr[...] = jnp.zeros(l_scr.shape, F32)
        acc[...] = jnp.zeros(acc.shape, F32)
        if extra:
            update(kc_ref[0], vc_ref[0])

    update(k_ref[0], v_ref[0])

    @pl.when(j == pl.num_programs(3) - 1)
    def _():
        lp = lam_ref[...]
        lam = (jnp.exp(jnp.sum(lp[0:1] * lp[1:2], axis=-1, keepdims=True))
               - jnp.exp(jnp.sum(lp[2:3] * lp[3:4], axis=-1, keepdims=True)) + lam_init)
        o = acc[0:tq, :] / l_scr[0:tq, :] - lam * (acc[tq:2 * tq, :] / l_scr[tq:2 * tq, :])
        o_ref[0] = (_rms(o) * gn_ref[...] * (1.0 - lam_init)).astype(BF16)


def _diff_attention(qk_q, qk_k, p_k, lam_params, g_norm, lam_init, tq, tk, qk_c=None, p_c=None, v_col0=0):
    b, nq, w = qk_q.shape
    heads = w // (2 * LANES)
    nk = qk_k.shape[1]
    extra = qk_c is not None
    vb = v_col0 // LANES
    in_specs = [
        pl.BlockSpec((1, tq, LANES), lambda bi, h, i, j: (bi, i, h)),
        pl.BlockSpec((1, tk, LANES), lambda bi, h, i, j: (bi, j, heads + h)),
        pl.BlockSpec((1, tk, LANES), lambda bi, h, i, j: (bi, j, vb + h)),
    ]
    args = [qk_q, qk_k, p_k]
    if extra:
        nc = qk_c.shape[1]
        in_specs += [
            pl.BlockSpec((1, nc, LANES), lambda bi, h, i, j: (bi, 0, heads + h)),
            pl.BlockSpec((1, nc, LANES), lambda bi, h, i, j: (bi, 0, vb + h)),
        ]
        args += [qk_c, p_c]
    in_specs += [
        pl.BlockSpec((4, DIFF_DK), lambda bi, h, i, j: (0, 0)),
        pl.BlockSpec((1, DIFF_DV), lambda bi, h, i, j: (0, 0)),
    ]
    args += [lam_params, g_norm]
    return pl.pallas_call(
        functools.partial(_diff_kernel, extra=extra, lam_init=lam_init),
        out_shape=jax.ShapeDtypeStruct((b, nq, heads * DIFF_DV), BF16),
        grid=(b, heads, nq // tq, nk // tk),
        in_specs=in_specs,
        out_specs=pl.BlockSpec((1, tq, LANES), lambda bi, h, i, j: (bi, i, h)),
        scratch_shapes=[
            pltpu.VMEM((2 * tq, LANES), BF16),
            pltpu.VMEM((2 * tq, 1), F32),
            pltpu.VMEM((2 * tq, 1), F32),
            pltpu.VMEM((2 * tq, DIFF_DV), F32),
        ],
        compiler_params=_params("parallel", "parallel", "parallel", "arbitrary"),
        name="diff_attention",
    )(*args)


def _outproj_kernel(ya_ref, yb_ref, yd_ref, w_ref, x_ref, g_ref, gate_ref, o_ref):
    wa, wb = ya_ref.shape[2], yb_ref.shape[2]
    y = jnp.dot(ya_ref[0], w_ref[0:wa, :], preferred_element_type=F32)
    y = y + jnp.dot(yb_ref[0], w_ref[wa:wa + wb, :], preferred_element_type=F32)
    y = y + jnp.dot(yd_ref[0], w_ref[wa + wb:, :], preferred_element_type=F32)
    o_ref[0] = x_ref[0] + gate_ref[0] * (_rms(y) * g_ref[...])


def _outproj(ya, yb, yd, w, x, g, mods, mod_row0, tm):
    b, t, d = x.shape
    return pl.pallas_call(
        _outproj_kernel,
        out_shape=jax.ShapeDtypeStruct((b, t, d), F32),
        grid=(b, t // tm),
        in_specs=[
            pl.BlockSpec((1, tm, ya.shape[2]), lambda bi, i: (bi, i, 0)),
            pl.BlockSpec((1, tm, yb.shape[2]), lambda bi, i: (bi, i, 0)),
            pl.BlockSpec((1, tm, yd.shape[2]), lambda bi, i: (bi, i, 0)),
            pl.BlockSpec(w.shape, lambda bi, i: (0, 0)),
            pl.BlockSpec((1, tm, d), lambda bi, i: (bi, i, 0)),
            pl.BlockSpec((1, d), lambda bi, i: (0, 0)),
            pl.BlockSpec((1, 1, d), lambda bi, i: ((bi + mod_row0) * 6 + 2, 0, 0)),
        ],
        out_specs=pl.BlockSpec((1, tm, d), lambda bi, i: (bi, i, 0)),
        compiler_params=_params("parallel", "parallel"),
        name="outproj",
    )(ya, yb, yd, w, x, g, mods)


def _mlp_kernel(x_ref, gpre_ref, shift_ref, scale_ref, w1_ref, w2_ref, gpost_ref, gate_ref, o_ref, h_scr, acc):
    k = pl.program_id(2)

    @pl.when(k == 0)
    def _():
        y = _rms(x_ref[0]) * gpre_ref[...]
        h_scr[...] = (y * (1.0 + scale_ref[0]) + shift_ref[0]).astype(BF16)
        acc[...] = jnp.zeros(acc.shape, F32)

    u = jnp.maximum(jnp.dot(h_scr[...], w1_ref[...], preferred_element_type=F32), 0.0)
    acc[...] += jnp.dot((u * u).astype(BF16), w2_ref[...], preferred_element_type=F32)

    @pl.when(k == pl.num_programs(2) - 1)
    def _():
        o_ref[0] = x_ref[0] + gate_ref[0] * (_rms(acc[...]) * gpost_ref[...])


def _mlp(x, gpre, gpost, mods, mod_row0, w1, w2, tm, th):
    b, t, d = x.shape
    hid = w1.shape[1]

    def mod(comp):
        return pl.BlockSpec((1, 1, d), lambda bi, i, k: ((bi + mod_row0) * 6 + comp, 0, 0))

    return pl.pallas_call(
        _mlp_kernel,
        out_shape=jax.ShapeDtypeStruct((b, t, d), F32),
        grid=(b, t // tm, hid // th),
        in_specs=[
            pl.BlockSpec((1, tm, d), lambda bi, i, k: (bi, i, 0)),
            pl.BlockSpec((1, d), lambda bi, i, k: (0, 0)),
            mod(3),
            mod(4),
            pl.BlockSpec((d, th), lambda bi, i, k: (0, k)),
            pl.BlockSpec((th, d), lambda bi, i, k: (k, 0)),
            pl.BlockSpec((1, d), lambda bi, i, k: (0, 0)),
            mod(5),
        ],
        out_specs=pl.BlockSpec((1, tm, d), lambda bi, i, k: (bi, i, 0)),
        scratch_shapes=[pltpu.VMEM((tm, d), BF16), pltpu.VMEM((tm, d), F32)],
        compiler_params=_params("parallel", "parallel", "arbitrary"),
        name="mlp",
    )(x, gpre, mods, mods, w1, w2, gpost, mods)


def _rope_tables(t):
    tok = jnp.arange(t, dtype=jnp.int32)
    row = (tok // GRID_W).astype(F32)
    col = (tok % GRID_W).astype(F32)
    n_freq = DIFF_DK // 4
    inv = ROPE_BASE ** (-jnp.arange(n_freq, dtype=F32) / n_freq)
    ang = jnp.concatenate([row[:, None] * inv, col[:, None] * inv], axis=-1)
    cos, sin = jnp.cos(ang), jnp.sin(ang)
    reps = LANES // DIFF_DK
    return (jnp.tile(jnp.concatenate([cos, cos], axis=-1), (1, reps)),
            jnp.tile(jnp.concatenate([-sin, sin], axis=-1), (1, reps)))


def _pick(n, pref):
    return pref if n % pref == 0 else n


def kernel(x, c, ctx, c_ctx, w_ada, b_ada, g_pre_mix, g_post_mix, g_pre_mlp, g_post_mlp, w_in, w_out,
           na_rel_bias, gla_w_gate, gla_b_gate, gla_g_norm, diff_lambda, diff_g_norm, w_mlp_in, w_mlp_out):
    bsz, t, d = x.shape
    n_ctx = ctx.shape[1]
    depth = w_ada.shape[0]
    na_heads = na_rel_bias.shape[1]
    na_w = na_heads * HEAD_DIM
    gla_kw = GLA_HEADS * GLA_DK
    gla_vw = GLA_HEADS * GLA_DV
    diff_w = w_out.shape[1] - na_w - gla_vw
    diff_heads = diff_w // DIFF_DV
    diff_qk = diff_heads * 2 * DIFF_DK
    rows = t // GRID_W
    assert t % (NA_ROWS_PER_STEP * GRID_W) == 0 and rows >= NA_KEY_ROWS
    assert bsz + 1 <= 8

    o_gla = 3 * na_w
    o_gate = o_gla + 2 * gla_kw + 2 * gla_vw
    o_diff = o_gate + 2 * GLA_GATE_RANK
    gla_col0 = 3 * na_w
    diffv_col0 = gla_col0 + 2 * gla_kw + 2 * gla_vw
    gate_col0 = diffv_col0 + diff_w

    cc = jnp.zeros((8, d), F32).at[:bsz].set(c).at[bsz].set(c_ctx)
    mods_all = _mods(cc, w_ada, b_ada)
    rope_tabs = _rope_tables(t)
    xc = ctx.reshape(1, bsz * n_ctx, d)
    zeros_state = jnp.zeros((bsz, GLA_HEADS, GLA_DV, LANES), F32)

    tm_in = _pick(t, 1024)
    tm_out = _pick(t, 512)
    tm_ctx = _pick(bsz * n_ctx, 512)
    tq = _pick(t, 512)
    tk = _pick(t, 2048)
    nc_lat = _pick(t // GLA_CHUNK, 8)
    nc_ctx = _pick(n_ctx // GLA_CHUNK, 4)

    for l in range(depth):
        need_ctx = l < depth - 1
        mods = mods_all[l].reshape(8 * 6, 1, d)
        wl = w_in[l]
        w_qk = wl[:, o_diff:o_diff + 2 * diff_qk].astype(BF16)
        w_rest = jnp.concatenate(
            [wl[:, :o_gate], wl[:, o_diff + 2 * diff_qk:], wl[:, o_gate:o_diff],
             jnp.zeros((d, LANES - 2 * GLA_GATE_RANK), F32)], axis=1).astype(BF16)
        w_o = w_out[l].astype(BF16)
        w1 = w_mlp_in[l].astype(BF16)
        w2 = w_mlp_out[l].astype(BF16)
        g1 = g_pre_mix[l].reshape(1, d)
        g2 = g_post_mix[l].reshape(1, d)
        g3 = g_pre_mlp[l].reshape(1, d)
        g4 = g_post_mlp[l].reshape(1, d)
        wg = jnp.zeros((2, LANES, gla_kw), F32)
        wg = wg.at[0, :GLA_GATE_RANK].set(gla_w_gate[l, 0]).at[1, GLA_GATE_RANK:2 * GLA_GATE_RANK].set(gla_w_gate[l, 1])
        wg = wg.astype(BF16)
        bg = gla_b_gate[l].reshape(2, 1, gla_kw)
        gla_gn = gla_g_norm[l].reshape(1, GLA_DV)
        diff_gn = diff_g_norm[l].reshape(1, DIFF_DV)
        lam_init = 0.8 - 0.6 * math.exp(-0.3 * l)

        qk_x = _inproj(x, g1, mods, 0, w_qk, tm_in, 1024, rope_tabs)
        p_x = _inproj(x, g1, mods, 0, w_rest, tm_in, w_rest.shape[1] // 3)
        qk_c = _inproj(xc, g1, mods, bsz, w_qk, tm_ctx, 1024).reshape(bsz, n_ctx, -1)
        p_c = _inproj(xc, g1, mods, bsz, w_rest, tm_ctx, w_rest.shape[1] // 3).reshape(bsz, n_ctx, -1)

        ya = _na_attention(p_x, p_c, _na_bias_table(na_rel_bias[l], rows))

        of_c, s_f = _gla_sweep(p_c, wg[0], bg[0], zeros_state, nc_ctx, gla_col0, gate_col0, False)
        yb_c, s_b = _gla_sweep(p_c, wg[1], bg[1], zeros_state, nc_ctx, gla_col0, gate_col0, True, of_c, gla_gn)
        of_x, _ = _gla_sweep(p_x, wg[0], bg[0], s_f, nc_lat, gla_col0, gate_col0, False)
        yb, _ = _gla_sweep(p_x, wg[1], bg[1], s_b, nc_lat, gla_col0, gate_col0, True, of_x, gla_gn)

        yd = _diff_attention(qk_x, qk_x, p_x, diff_lambda[l], diff_gn, lam_init, tq, tk, qk_c, p_c, diffv_col0)

        x = _outproj(ya, yb, yd, w_o, x, g2, mods, 0, tm_out)
        x = _mlp(x, g3, g4, mods, 0, w1, w2, tm_out, 512)

        if need_ctx:
            ya_c = _ctx_attention(p_c, na_heads // 2)
            yd_c = _diff_attention(qk_c, qk_c, p_c, diff_lambda[l], diff_gn, lam_init, n_ctx, n_ctx, v_col0=diffv_col0)
            flat = lambda z: z.reshape(1, bsz * n_ctx, -1)
            xc = _outproj(flat(ya_c), flat(yb_c), flat(yd_c), w_o, xc, g2, mods, bsz, tm_ctx)
            xc = _mlp(xc, g3, g4, mods, bsz, w1, w2, tm_ctx, 512)
    return x
```

```python
import functools
import math

import jax
import jax.numpy as jnp
from jax import lax
from jax.experimental import pallas as pl
from jax.experimental.pallas import tpu as pltpu

F32 = jnp.float32
BF16 = jnp.bfloat16

GRID_W = 64
EPS = 1e-6
NEG_INF = -1e30
HEAD_DIM = 64
NA_WIN_R = 8
NA_WIN_C = 16
GLA_HEADS = 4
GLA_DK = 64
GLA_DV = 128
GLA_GATE_RANK = 16
GLA_NORMALIZER = 16.0
GLA_CHUNK = 64
DIFF_DK = 64
DIFF_DV = 128
ROPE_BASE = 10000.0
DIFF_Q_SCALE = DIFF_DK ** -0.5 * math.log2(math.e)

LANES = 128
VMEM_LIMIT = 56 * 1024 * 1024

NA_ROWS_PER_STEP = 8
NA_KEY_ROWS = NA_ROWS_PER_STEP + NA_WIN_R


def _params(*sem):
    return pltpu.CompilerParams(dimension_semantics=sem, vmem_limit_bytes=VMEM_LIMIT)


def _nt_dot(a, b):
    return lax.dot_general(a, b, (((1,), (1,)), ((), ())), preferred_element_type=F32)


def _rms(y):
    return y * lax.rsqrt(jnp.mean(y * y, axis=-1, keepdims=True) + EPS)


def _mods_kernel(c_ref, w_ref, b_ref, o_ref):
    c = c_ref[...]
    s = (c * jax.nn.sigmoid(c)).astype(BF16)
    o_ref[0] = jnp.dot(s, w_ref[0].astype(BF16), preferred_element_type=F32) + b_ref[0]


def _mods(cc, w_ada, b_ada):
    depth, d, n = w_ada.shape
    tn = 1024
    return pl.pallas_call(
        _mods_kernel,
        out_shape=jax.ShapeDtypeStruct((depth, 8, n), F32),
        grid=(depth, n // tn),
        in_specs=[
            pl.BlockSpec((8, d), lambda l, j: (0, 0)),
            pl.BlockSpec((1, d, tn), lambda l, j: (l, 0, j)),
            pl.BlockSpec((1, 1, tn), lambda l, j: (l, 0, j)),
        ],
        out_specs=pl.BlockSpec((1, 8, tn), lambda l, j: (l, 0, j)),
        compiler_params=_params("arbitrary", "arbitrary"),
        name="ada_mods",
    )(cc, w_ada, b_ada.reshape(depth, 1, n))


def _inproj_kernel(x_ref, g_ref, shift_ref, scale_ref, w_ref, *rest, rope, q_tiles, q_scale):
    if rope:
        cos_ref, sin_ref, o_ref, h_scr = rest
    else:
        o_ref, h_scr = rest

    @pl.when(pl.program_id(2) == 0)
    def _():
        y = _rms(x_ref[0]) * g_ref[...]
        h_scr[...] = (y * (1.0 + scale_ref[0]) + shift_ref[0]).astype(BF16)

    acc = jnp.dot(h_scr[...], w_ref[...], preferred_element_type=F32)
    if q_tiles:
        acc = acc * jnp.where(pl.program_id(2) < q_tiles, q_scale, 1.0)
    if not rope:
        o_ref[0] = acc.astype(BF16)
        return
    cos = cos_ref[...]
    sin = sin_ref[...]
    lower = (lax.broadcasted_iota(jnp.int32, cos.shape, 1) % DIFF_DK) < DIFF_DK // 2
    for c in range(acc.shape[1] // LANES):
        a = acc[:, c * LANES:(c + 1) * LANES]
        partner = jnp.where(lower, pltpu.roll(a, LANES - DIFF_DK // 2, 1), pltpu.roll(a, DIFF_DK // 2, 1))
        o_ref[0, :, c * LANES:(c + 1) * LANES] = (a * cos + partner * sin).astype(BF16)


def _inproj(x, g, mods, mod_row0, w, tm, tn, rope_tabs=None, q_tiles=0, q_scale=1.0):
    b, t, d = x.shape
    n = w.shape[1]
    rope = rope_tabs is not None
    in_specs = [
        pl.BlockSpec((1, tm, d), lambda bi, i, j: (bi, i, 0)),
        pl.BlockSpec((1, d), lambda bi, i, j: (0, 0)),
        pl.BlockSpec((1, 1, d), lambda bi, i, j: ((bi + mod_row0) * 6 + 0, 0, 0)),
        pl.BlockSpec((1, 1, d), lambda bi, i, j: ((bi + mod_row0) * 6 + 1, 0, 0)),
        pl.BlockSpec((d, tn), lambda bi, i, j: (0, j)),
    ]
    args = [x, g, mods, mods, w]
    if rope:
        in_specs += [pl.BlockSpec((tm, LANES), lambda bi, i, j: (i, 0))] * 2
        args += list(rope_tabs)
    return pl.pallas_call(
        functools.partial(_inproj_kernel, rope=rope, q_tiles=q_tiles, q_scale=q_scale),
        out_shape=jax.ShapeDtypeStruct((b, t, n), BF16),
        grid=(b, t // tm, n // tn),
        in_specs=in_specs,
        out_specs=pl.BlockSpec((1, tm, tn), lambda bi, i, j: (bi, i, j)),
        scratch_shapes=[pltpu.VMEM((tm, d), BF16)],
        compiler_params=_params("parallel", "parallel", "arbitrary"),
        name="inproj_rope" if rope else "inproj",
    )(*args)


def _na_bias_table(rel_bias, rows):
    rq, kwin = NA_ROWS_PER_STEP, NA_KEY_ROWS
    nb = rows // rq
    h = rel_bias.shape[0]
    blocks = jnp.array([0, min(1, nb - 1), nb - 1], jnp.int32)
    r = blocks[:, None] * rq + jnp.arange(rq, dtype=jnp.int32)[None, :]
    kst = jnp.clip(blocks * rq - NA_WIN_R // 2, 0, rows - kwin)
    kr = kst[:, None] + jnp.arange(kwin, dtype=jnp.int32)[None, :]
    s_r = jnp.clip(r - NA_WIN_R // 2, 0, rows - NA_WIN_R)
    row_ok = (kr[:, None, :] >= s_r[:, :, None]) & (kr[:, None, :] < s_r[:, :, None] + NA_WIN_R)
    dr = jnp.clip(kr[:, None, :] - r[:, :, None] + (NA_WIN_R - 1), 0, 2 * NA_WIN_R - 2)
    col = jnp.arange(GRID_W, dtype=jnp.int32)
    c0 = jnp.clip(col - NA_WIN_C // 2, 0, GRID_W - NA_WIN_C)
    col_ok = (col[None, :] >= c0[:, None]) & (col[None, :] < c0[:, None] + NA_WIN_C)
    dc = jnp.clip(col[None, :] - col[:, None] + (NA_WIN_C - 1), 0, 2 * NA_WIN_C - 2)
    pick_dc = (dc[:, :, None] == jnp.arange(2 * NA_WIN_C - 1)[None, None, :]).astype(F32)
    pick_dr = (dr[..., None] == jnp.arange(2 * NA_WIN_R - 1)[None, None, None, :]).astype(F32)
    by_col = jnp.einsum('hrd,qkd->hrqk', rel_bias.astype(F32), pick_dc, precision=lax.Precision.HIGHEST)
    bias = jnp.einsum('vqlr,hrab->vhqalb', pick_dr, by_col, precision=lax.Precision.HIGHEST)
    ok = row_ok[:, None, :, None, :, None] & col_ok[None, None, None, :, None, :]
    bias = jnp.where(ok, bias, NEG_INF)
    return bias.reshape(3, h, rq * GRID_W, kwin * GRID_W)


def _na_kernel(q_ref, k_ref, v_ref, kc_ref, vc_ref, bias_ref, o_ref, *, rows):
    i = pl.program_id(2)
    kstart = jnp.clip(i * NA_ROWS_PER_STEP - NA_WIN_R // 2, 0, rows - NA_KEY_ROWS)
    off = pl.multiple_of(kstart * GRID_W, GRID_W)
    kw = k_ref[0, pl.ds(off, NA_KEY_ROWS * GRID_W), :]
    vw = v_ref[0, pl.ds(off, NA_KEY_ROWS * GRID_W), :]
    q = q_ref[0]
    kc = kc_ref[0]
    vc = vc_ref[0]
    scale = HEAD_DIM ** -0.5
    lane = lax.broadcasted_iota(jnp.int32, q.shape, 1)
    outs = []
    for hh in range(LANES // HEAD_DIM):
        own = (lane >= hh * HEAD_DIM) & (lane < (hh + 1) * HEAD_DIM)
        qh = jnp.where(own, q, 0.0).astype(BF16)
        s_loc = _nt_dot(qh, kw) * scale + bias_ref[0, hh]
        s_ctx = _nt_dot(qh, kc) * scale
        m = jnp.maximum(jnp.max(s_loc, axis=-1, keepdims=True), jnp.max(s_ctx, axis=-1, keepdims=True))
        p_loc = jnp.exp(s_loc - m)
        p_ctx = jnp.exp(s_ctx - m)
        l = jnp.sum(p_loc, axis=-1, keepdims=True) + jnp.sum(p_ctx, axis=-1, keepdims=True)
        o = jnp.dot(p_loc.astype(BF16), vw, preferred_element_type=F32)
        o = o + jnp.dot(p_ctx.astype(BF16), vc, preferred_element_type=F32)
        outs.append(o / l)
    o_ref[0] = jnp.where(lane < HEAD_DIM, outs[0], outs[1]).astype(BF16)


def _na_attention(p_lat, p_ctx, bias_tbl):
    b, t, _ = p_lat.shape
    n = p_ctx.shape[1]
    rows = t // GRID_W
    nb = rows // NA_ROWS_PER_STEP
    hp = bias_tbl.shape[1] // 2
    tq = NA_ROWS_PER_STEP * GRID_W

    def variant(i):
        return jnp.where(i == 0, 0, jnp.where(i == nb - 1, 2, 1))

    return pl.pallas_call(
        functools.partial(_na_kernel, rows=rows),
        out_shape=jax.ShapeDtypeStruct((b, t, hp * LANES), BF16),
        grid=(b, hp, nb),
        in_specs=[
            pl.BlockSpec((1, tq, LANES), lambda bi, h, i: (bi, i, h)),
            pl.BlockSpec((1, t, LANES), lambda bi, h, i: (bi, 0, hp + h)),
            pl.BlockSpec((1, t, LANES), lambda bi, h, i: (bi, 0, 2 * hp + h)),
            pl.BlockSpec((1, n, LANES), lambda bi, h, i: (bi, 0, hp + h)),
            pl.BlockSpec((1, n, LANES), lambda bi, h, i: (bi, 0, 2 * hp + h)),
            pl.BlockSpec((1, 2, tq, NA_KEY_ROWS * GRID_W), lambda bi, h, i: (variant(i), h, 0, 0)),
        ],
        out_specs=pl.BlockSpec((1, tq, LANES), lambda bi, h, i: (bi, i, h)),
        compiler_params=_params("parallel", "parallel", "arbitrary"),
        name="na_attention",
    )(p_lat, p_lat, p_lat, p_ctx, p_ctx, bias_tbl)


def _ctx_attn_kernel(q_ref, k_ref, v_ref, o_ref):
    q, k, v = q_ref[0], k_ref[0], v_ref[0]
    scale = HEAD_DIM ** -0.5
    lane = lax.broadcasted_iota(jnp.int32, q.shape, 1)
    outs = []
    for hh in range(LANES // HEAD_DIM):
        own = (lane >= hh * HEAD_DIM) & (lane < (hh + 1) * HEAD_DIM)
        s = _nt_dot(jnp.where(own, q, 0.0).astype(BF16), k) * scale
        p = jnp.exp(s - jnp.max(s, axis=-1, keepdims=True))
        l = jnp.sum(p, axis=-1, keepdims=True)
        outs.append(jnp.dot(p.astype(BF16), v, preferred_element_type=F32) / l)
    o_ref[0] = jnp.where(lane < HEAD_DIM, outs[0], outs[1]).astype(BF16)


def _ctx_attention(p_ctx, hp):
    b, n, _ = p_ctx.shape
    return pl.pallas_call(
        _ctx_attn_kernel,
        out_shape=jax.ShapeDtypeStruct((b, n, hp * LANES), BF16),
        grid=(b, hp),
        in_specs=[
            pl.BlockSpec((1, n, LANES), lambda bi, h: (bi, 0, h)),
            pl.BlockSpec((1, n, LANES), lambda bi, h: (bi, 0, hp + h)),
            pl.BlockSpec((1, n, LANES), lambda bi, h: (bi, 0, 2 * hp + h)),
        ],
        out_specs=pl.BlockSpec((1, n, LANES), lambda bi, h: (bi, 0, h)),
        compiler_params=_params("parallel", "parallel"),
        name="ctx_attention",
    )(p_ctx, p_ctx, p_ctx)


def _gla_kernel(*refs, backward, nc):
    if backward:
        (q_ref, k_ref, v_ref, a_ref, wg_ref, bg_ref, s0_ref, of_ref, g_ref, gn_ref,
         o_ref, sout_ref, s_scr) = refs
    else:
        q_ref, k_ref, v_ref, a_ref, wg_ref, bg_ref, s0_ref, o_ref, sout_ref, s_scr = refs
    j = pl.program_id(1)
    tb = nc * GLA_CHUNK
    kw = GLA_HEADS * GLA_DK

    @pl.when(j == 0)
    def _():
        s_scr[...] = s0_ref[0]

    pre = jnp.dot(a_ref[0], wg_ref[...], preferred_element_type=F32) + bg_ref[...]
    log_a = (jnp.minimum(pre, 0.0) - jnp.log1p(jnp.exp(-jnp.abs(pre)))) * (1.0 / GLA_NORMALIZER)

    pos = lax.broadcasted_iota(jnp.int32, (tb, kw), 0) % GLA_CHUNK
    bsum = log_a
    step = 1
    while step < GLA_CHUNK:
        if backward:
            bsum = bsum + jnp.where(pos < GLA_CHUNK - step, pltpu.roll(bsum, tb - step, 0), 0.0)
        else:
            bsum = bsum + jnp.where(pos >= step, pltpu.roll(bsum, step, 0), 0.0)
        step *= 2

    b3 = bsum.reshape(nc, GLA_CHUNK, kw)
    b_end = b3[:, 0:1, :] if backward else b3[:, GLA_CHUNK - 1:GLA_CHUNK, :]
    q3 = q_ref[0].astype(F32).reshape(nc, GLA_CHUNK, kw)
    k3 = k_ref[0].astype(F32).reshape(nc, GLA_CHUNK, kw)
    q_in = q3 * (GLA_DK ** -0.5) * jnp.exp(b3)
    k_out = (k3 * jnp.exp(-b3)).astype(BF16)
    k_dec = (k3 * jnp.exp(b_end - b3)).astype(BF16)
    decay = jnp.exp(b_end)
    v3 = v_ref[0].reshape(nc, GLA_CHUNK, GLA_HEADS * GLA_DV)

    t_idx = lax.broadcasted_iota(jnp.int32, (nc, GLA_CHUNK, GLA_CHUNK), 1)
    s_idx = lax.broadcasted_iota(jnp.int32, (nc, GLA_CHUNK, GLA_CHUNK), 2)
    tri = (s_idx > t_idx) if backward else (s_idx <= t_idx)
    eye = (lax.broadcasted_iota(jnp.int32, (nc, GLA_DV, GLA_DV), 1)
           == lax.broadcasted_iota(jnp.int32, (nc, GLA_DV, GLA_DV), 2)).astype(BF16)
    lane = lax.broadcasted_iota(jnp.int32, (nc, GLA_CHUNK, LANES), 2)
    order = range(nc - 1, -1, -1) if backward else range(nc)

    outs = []
    for h in range(GLA_HEADS):
        pair = slice((h // 2) * LANES, (h // 2 + 1) * LANES)
        own = (lane >= GLA_DK) if h % 2 else (lane < GLA_DK)
        qm = jnp.where(own, q_in[:, :, pair], 0.0).astype(BF16)
        vh = v3[:, :, h * GLA_DV:(h + 1) * GLA_DV]
        att = jnp.einsum('ctd,csd->cts', qm, k_out[:, :, pair], preferred_element_type=F32)
        att = jnp.where(tri, att, 0.0).astype(BF16)
        o_h = jnp.einsum('cts,csv->ctv', att, vh, preferred_element_type=F32)
        vt = jnp.einsum('cvw,csw->cvs', eye, vh, preferred_element_type=F32).astype(BF16)
        ut = jnp.einsum('cvs,csd->cvd', vt, k_dec[:, :, pair], preferred_element_type=F32)
        s_h = s_scr[h]
        before = [None] * nc
        for c in order:
            before[c] = s_h
            s_h = s_h * decay[c, :, pair] + ut[c]
        s_scr[h] = s_h
        s_all = jnp.stack(before, axis=0).astype(BF16)
        o_h = o_h + jnp.einsum('ctd,cvd->ctv', qm, s_all, preferred_element_type=F32)
        outs.append(o_h.reshape(tb, GLA_DV))

    @pl.when(j == pl.num_programs(1) - 1)
    def _():
        sout_ref[0] = s_scr[...]

    if not backward:
        o_ref[0] = jnp.concatenate(outs, axis=-1)
        return
    of = of_ref[0]
    g = g_ref[0].astype(F32)
    fin = []
    for h in range(GLA_HEADS):
        sl = slice(h * GLA_DV, (h + 1) * GLA_DV)
        y = _rms(of[:, sl] + outs[h]) * gn_ref[...]
        gh = g[:, sl]
        fin.append(y * (gh * jax.nn.sigmoid(gh)))
    o_ref[0] = jnp.concatenate(fin, axis=-1).astype(BF16)


def _gla_sweep(p, wg, bg, s0, nc, col0, a_col, backward, o_fwd=None, g_norm=None):
    b, n, _ = p.shape
    tb = nc * GLA_CHUNK
    nblk = n // tb
    kw = GLA_HEADS * GLA_DK
    vw = GLA_HEADS * GLA_DV
    cq, ck, cv, cg, ca = col0 // kw, col0 // kw + 1, (col0 + 2 * kw) // vw, (col0 + 2 * kw) // vw + 1, a_col // LANES

    def blk(j):
        return nblk - 1 - j if backward else j

    in_specs = [
        pl.BlockSpec((1, tb, kw), lambda bi, j: (bi, blk(j), cq)),
        pl.BlockSpec((1, tb, kw), lambda bi, j: (bi, blk(j), ck)),
        pl.BlockSpec((1, tb, vw), lambda bi, j: (bi, blk(j), cv)),
        pl.BlockSpec((1, tb, LANES), lambda bi, j: (bi, blk(j), ca)),
        pl.BlockSpec((LANES, kw), lambda bi, j: (0, 0)),
        pl.BlockSpec((1, kw), lambda bi, j: (0, 0)),
        pl.BlockSpec((1, GLA_HEADS, GLA_DV, LANES), lambda bi, j: (bi, 0, 0, 0)),
    ]
    args = [p, p, p, p, wg, bg, s0]
    if backward:
        in_specs += [
            pl.BlockSpec((1, tb, vw), lambda bi, j: (bi, blk(j), 0)),
            pl.BlockSpec((1, tb, vw), lambda bi, j: (bi, blk(j), cg)),
            pl.BlockSpec((1, GLA_DV), lambda bi, j: (0, 0)),
        ]
        args += [o_fwd, p, g_norm]
    return pl.pallas_call(
        functools.partial(_gla_kernel, backward=backward, nc=nc),
        out_shape=(jax.ShapeDtypeStruct((b, n, vw), BF16 if backward else F32),
                   jax.ShapeDtypeStruct((b, GLA_HEADS, GLA_DV, LANES), F32)),
        grid=(b, nblk),
        in_specs=in_specs,
        out_specs=(pl.BlockSpec((1, tb, vw), lambda bi, j: (bi, blk(j), 0)),
                   pl.BlockSpec((1, GLA_HEADS, GLA_DV, LANES), lambda bi, j: (bi, 0, 0, 0))),
        scratch_shapes=[pltpu.VMEM((GLA_HEADS, GLA_DV, LANES), F32)],
        compiler_params=_params("parallel", "arbitrary"),
        name="gla_bwd" if backward else "gla_fwd",
    )(*args)


def _diff_lambda(lam_ref, lam_init):
    lp = lam_ref[...]
    return (jnp.exp(jnp.sum(lp[0:1] * lp[1:2], axis=-1, keepdims=True))
            - jnp.exp(jnp.sum(lp[2:3] * lp[3:4], axis=-1, keepdims=True)) + lam_init)


def _branch_queries(q):
    lane = lax.broadcasted_iota(jnp.int32, q.shape, 1)
    return jnp.concatenate([jnp.where(lane < DIFF_DK, q, 0.0), jnp.where(lane >= DIFF_DK, q, 0.0)], axis=0)


def _diff_ctx_kernel(q_ref, k_ref, v_ref, lam_ref, gn_ref, o_ref, *, lam_init):
    tq = q_ref.shape[1]
    s = _nt_dot(_branch_queries(q_ref[0]), k_ref[0])
    p = jnp.exp2(s - jnp.max(s, axis=-1, keepdims=True))
    a = jnp.dot(p.astype(BF16), v_ref[0], preferred_element_type=F32) / jnp.sum(p, axis=-1, keepdims=True)
    o = a[0:tq] - _diff_lambda(lam_ref, lam_init) * a[tq:2 * tq]
    o_ref[0] = (_rms(o) * gn_ref[...] * (1.0 - lam_init)).astype(BF16)


def _diff_attention_ctx(qk_c, p_c, lam_params, g_norm, lam_init, v_col0):
    b, n, w = qk_c.shape
    heads = w // (2 * LANES)
    vb = v_col0 // LANES
    return pl.pallas_call(
        functools.partial(_diff_ctx_kernel, lam_init=lam_init),
        out_shape=jax.ShapeDtypeStruct((b, n, heads * DIFF_DV), BF16),
        grid=(b, heads),
        in_specs=[
            pl.BlockSpec((1, n, LANES), lambda bi, h: (bi, 0, h)),
            pl.BlockSpec((1, n, LANES), lambda bi, h: (bi, 0, heads + h)),
            pl.BlockSpec((1, n, LANES), lambda bi, h: (bi, 0, vb + h)),
            pl.BlockSpec((4, DIFF_DK), lambda bi, h: (0, 0)),
            pl.BlockSpec((1, DIFF_DV), lambda bi, h: (0, 0)),
        ],
        out_specs=pl.BlockSpec((1, n, LANES), lambda bi, h: (bi, 0, h)),
        compiler_params=_params("parallel", "parallel"),
        name="diff_attention_ctx",
    )(qk_c, qk_c, p_c, lam_params, g_norm)


def _diff_kernel(q_ref, k_ref, vt_ref, kc_ref, vct_ref, lam_ref, gn_ref, o_ref,
                 qs, s0, s1, c0, c1, p0, p1, a0, a1, m_scr, l_scr, acc, *, nch, lam_init):
    tq = q_ref.shape[1]
    tkc = k_ref.shape[1] // nch

    def scores(c, s_out, c_out):
        k = k_ref[0, pl.ds(pl.multiple_of(c * tkc, tkc), tkc), :]
        s = _nt_dot(k, qs[...])
        s_out[...] = s
        c_out[...] = jnp.max(s, axis=0, keepdims=True)

    def values(c, p_in, a_in):
        acc[...] = acc[...] * a_in[...] + jnp.dot(vt_ref[0, 0, c], p_in[...], preferred_element_type=F32)

    def softmax(s_in, c_in, p_out, a_out):
        m_old = m_scr[...]
        m_new = jnp.maximum(m_old, c_in[...])
        alpha = jnp.exp2(m_old - m_new)
        p = jnp.exp2(s_in[...] - m_new)
        l_scr[...] = alpha * l_scr[...] + jnp.sum(p, axis=0, keepdims=True)
        p_out[...] = p.astype(BF16)
        a_out[...] = alpha
        m_scr[...] = m_new

    qs[...] = _branch_queries(q_ref[0])
    sc = _nt_dot(kc_ref[0], qs[...])
    mc = jnp.max(sc, axis=0, keepdims=True)
    pc = jnp.exp2(sc - mc)
    m_scr[...] = mc
    l_scr[...] = jnp.sum(pc, axis=0, keepdims=True)
    acc[...] = jnp.dot(vct_ref[0, 0], pc.astype(BF16), preferred_element_type=F32)
    scores(0, s0, c0)
    p1[...] = jnp.zeros(p1.shape, BF16)
    a1[...] = jnp.ones(a1.shape, F32)

    def pair(i, carry):
        scores(2 * i + 1, s1, c1)
        values(jnp.maximum(2 * i - 1, 0), p1, a1)
        softmax(s0, c0, p0, a0)
        scores(2 * i + 2, s0, c0)
        values(2 * i, p0, a0)
        softmax(s1, c1, p1, a1)
        return carry

    lax.fori_loop(0, nch // 2 - 1, pair, 0)
    scores(nch - 1, s1, c1)
    values(max(nch - 3, 0), p1, a1)
    softmax(s0, c0, p0, a0)
    values(nch - 2, p0, a0)
    softmax(s1, c1, p1, a1)
    values(nch - 1, p1, a1)

    a = acc[...] / l_scr[...]
    o = a[:, 0:tq] - _diff_lambda(lam_ref, lam_init) * a[:, tq:2 * tq]
    y = o * lax.rsqrt(jnp.mean(o * o, axis=0, keepdims=True) + EPS) * gn_ref[...] * (1.0 - lam_init)
    o_ref[0] = y.T.astype(BF16)


def _diff_attention(qk_x, vt_x, qk_c, vt_c, lam_params, g_norm, lam_init, tq):
    b, t, w = qk_x.shape
    heads = w // (2 * LANES)
    n_ctx = qk_c.shape[1]
    nch, tkc = vt_x.shape[2], vt_x.shape[4]
    assert nch % 2 == 0 and nch * tkc == t
    return pl.pallas_call(
        functools.partial(_diff_kernel, nch=nch, lam_init=lam_init),
        out_shape=jax.ShapeDtypeStruct((b, t, heads * DIFF_DV), BF16),
        grid=(b, heads, t // tq),
        in_specs=[
            pl.BlockSpec((1, tq, LANES), lambda bi, h, i: (bi, i, h)),
            pl.BlockSpec((1, t, LANES), lambda bi, h, i: (bi, 0, heads + h)),
            pl.BlockSpec((1, 1, nch, DIFF_DV, tkc), lambda bi, h, i: (bi, h, 0, 0, 0)),
            pl.BlockSpec((1, n_ctx, LANES), lambda bi, h, i: (bi, 0, heads + h)),
            pl.BlockSpec((1, 1, DIFF_DV, n_ctx), lambda bi, h, i: (bi, h, 0, 0)),
            pl.BlockSpec((4, DIFF_DK), lambda bi, h, i: (0, 0)),
            pl.BlockSpec((DIFF_DV, 1), lambda bi, h, i: (0, 0)),
        ],
        out_specs=pl.BlockSpec((1, tq, LANES), lambda bi, h, i: (bi, i, h)),
        scratch_shapes=[
            pltpu.VMEM((2 * tq, LANES), BF16),
            pltpu.VMEM((tkc, 2 * tq), F32),
            pltpu.VMEM((tkc, 2 * tq), F32),
            pltpu.VMEM((1, 2 * tq), F32),
            pltpu.VMEM((1, 2 * tq), F32),
            pltpu.VMEM((tkc, 2 * tq), BF16),
            pltpu.VMEM((tkc, 2 * tq), BF16),
            pltpu.VMEM((1, 2 * tq), F32),
            pltpu.VMEM((1, 2 * tq), F32),
            pltpu.VMEM((1, 2 * tq), F32),
            pltpu.VMEM((1, 2 * tq), F32),
            pltpu.VMEM((DIFF_DV, 2 * tq), F32),
        ],
        compiler_params=_params("parallel", "parallel", "arbitrary"),
        name="diff_attention",
    )(qk_x, qk_x, vt_x, qk_c, vt_c, lam_params, g_norm.reshape(DIFF_DV, 1))


def _outproj_kernel(ya_ref, yb_ref, yd_ref, w_ref, x_ref, g_ref, gate_ref, o_ref):
    wa, wb = ya_ref.shape[2], yb_ref.shape[2]
    y = jnp.dot(ya_ref[0], w_ref[0:wa, :], preferred_element_type=F32)
    y = y + jnp.dot(yb_ref[0], w_ref[wa:wa + wb, :], preferred_element_type=F32)
    y = y + jnp.dot(yd_ref[0], w_ref[wa + wb:, :], preferred_element_type=F32)
    o_ref[0] = x_ref[0] + gate_ref[0] * (_rms(y) * g_ref[...])


def _outproj(ya, yb, yd, w, x, g, mods, mod_row0, tm):
    b, t, d = x.shape
    return pl.pallas_call(
        _outproj_kernel,
        out_shape=jax.ShapeDtypeStruct((b, t, d), F32),
        grid=(b, t // tm),
        in_specs=[
            pl.BlockSpec((1, tm, ya.shape[2]), lambda bi, i: (bi, i, 0)),
            pl.BlockSpec((1, tm, yb.shape[2]), lambda bi, i: (bi, i, 0)),
            pl.BlockSpec((1, tm, yd.shape[2]), lambda bi, i: (bi, i, 0)),
            pl.BlockSpec(w.shape, lambda bi, i: (0, 0)),
            pl.BlockSpec((1, tm, d), lambda bi, i: (bi, i, 0)),
            pl.BlockSpec((1, d), lambda bi, i: (0, 0)),
            pl.BlockSpec((1, 1, d), lambda bi, i: ((bi + mod_row0) * 6 + 2, 0, 0)),
        ],
        out_specs=pl.BlockSpec((1, tm, d), lambda bi, i: (bi, i, 0)),
        compiler_params=_params("parallel", "parallel"),
        name="outproj",
    )(ya, yb, yd, w, x, g, mods)


def _mlp_kernel(x_ref, gpre_ref, shift_ref, scale_ref, w1_ref, w2_ref, gpost_ref, gate_ref, o_ref, h_scr, acc):
    k = pl.program_id(2)

    @pl.when(k == 0)
    def _():
        y = _rms(x_ref[0]) * gpre_ref[...]
        h_scr[...] = (y * (1.0 + scale_ref[0]) + shift_ref[0]).astype(BF16)
        acc[...] = jnp.zeros(acc.shape, F32)

    u = jnp.maximum(jnp.dot(h_scr[...], w1_ref[...], preferred_element_type=F32), 0.0)
    acc[...] += jnp.dot((u * u).astype(BF16), w2_ref[...], preferred_element_type=F32)

    @pl.when(k == pl.num_programs(2) - 1)
    def _():
        o_ref[0] = x_ref[0] + gate_ref[0] * (_rms(acc[...]) * gpost_ref[...])


def _mlp(x, gpre, gpost, mods, mod_row0, w1, w2, tm, th):
    b, t, d = x.shape
    hid = w1.shape[1]

    def mod(comp):
        return pl.BlockSpec((1, 1, d), lambda bi, i, k: ((bi + mod_row0) * 6 + comp, 0, 0))

    return pl.pallas_call(
        _mlp_kernel,
        out_shape=jax.ShapeDtypeStruct((b, t, d), F32),
        grid=(b, t // tm, hid // th),
        in_specs=[
            pl.BlockSpec((1, tm, d), lambda bi, i, k: (bi, i, 0)),
            pl.BlockSpec((1, d), lambda bi, i, k: (0, 0)),
            mod(3),
            mod(4),
            pl.BlockSpec((d, th), lambda bi, i, k: (0, k)),
            pl.BlockSpec((th, d), lambda bi, i, k: (k, 0)),
            pl.BlockSpec((1, d), lambda bi, i, k: (0, 0)),
            mod(5),
        ],
        out_specs=pl.BlockSpec((1, tm, d), lambda bi, i, k: (bi, i, 0)),
        scratch_shapes=[pltpu.VMEM((tm, d), BF16), pltpu.VMEM((tm, d), F32)],
        compiler_params=_params("parallel", "parallel", "arbitrary"),
        name="mlp",
    )(x, gpre, mods, mods, w1, w2, gpost, mods)


def _rope_tables(t):
    tok = jnp.arange(t, dtype=jnp.int32)
    row = (tok // GRID_W).astype(F32)
    col = (tok % GRID_W).astype(F32)
    n_freq = DIFF_DK // 4
    inv = ROPE_BASE ** (-jnp.arange(n_freq, dtype=F32) / n_freq)
    ang = jnp.concatenate([row[:, None] * inv, col[:, None] * inv], axis=-1)
    cos, sin = jnp.cos(ang), jnp.sin(ang)
    reps = LANES // DIFF_DK
    return (jnp.tile(jnp.concatenate([cos, cos], axis=-1), (1, reps)),
            jnp.tile(jnp.concatenate([-sin, sin], axis=-1), (1, reps)))


def _pick(n, pref):
    return pref if n % pref == 0 else n


def kernel(x, c, ctx, c_ctx, w_ada, b_ada, g_pre_mix, g_post_mix, g_pre_mlp, g_post_mlp, w_in, w_out,
           na_rel_bias, gla_w_gate, gla_b_gate, gla_g_norm, diff_lambda, diff_g_norm, w_mlp_in, w_mlp_out):
    bsz, t, d = x.shape
    n_ctx = ctx.shape[1]
    depth = w_ada.shape[0]
    na_heads = na_rel_bias.shape[1]
    na_w = na_heads * HEAD_DIM
    gla_kw = GLA_HEADS * GLA_DK
    gla_vw = GLA_HEADS * GLA_DV
    diff_w = w_out.shape[1] - na_w - gla_vw
    diff_heads = diff_w // DIFF_DV
    diff_qk = diff_heads * 2 * DIFF_DK
    rows = t // GRID_W
    assert t % (NA_ROWS_PER_STEP * GRID_W) == 0 and rows >= NA_KEY_ROWS
    assert bsz + 1 <= 8

    o_gla = 3 * na_w
    o_gate = o_gla + 2 * gla_kw + 2 * gla_vw
    o_diff = o_gate + 2 * GLA_GATE_RANK
    gla_col0 = 3 * na_w
    diffv_col0 = gla_col0 + 2 * gla_kw + 2 * gla_vw
    gate_col0 = diffv_col0 + diff_w

    cc = jnp.zeros((8, d), F32).at[:bsz].set(c).at[bsz].set(c_ctx)
    mods_all = _mods(cc, w_ada, b_ada)
    rope_tabs = _rope_tables(t)
    xc = ctx.reshape(1, bsz * n_ctx, d)
    zeros_state = jnp.zeros((bsz, GLA_HEADS, GLA_DV, LANES), F32)

    tm_in = _pick(t, 1024)
    tm_out = _pick(t, 512)
    tm_ctx = _pick(bsz * n_ctx, 512)
    tq = _pick(t, 512)
    tkc = min(1024, t // 2)
    nc_lat = _pick(t // GLA_CHUNK, 8)
    nc_ctx = _pick(n_ctx // GLA_CHUNK, 4)

    for l in range(depth):
        need_ctx = l < depth - 1
        mods = mods_all[l].reshape(8 * 6, 1, d)
        wl = w_in[l]
        w_qk = wl[:, o_diff:o_diff + 2 * diff_qk].astype(BF16)
        w_rest = jnp.concatenate(
            [wl[:, :o_gate], wl[:, o_diff + 2 * diff_qk:], wl[:, o_gate:o_diff],
             jnp.zeros((d, LANES - 2 * GLA_GATE_RANK), F32)], axis=1).astype(BF16)
        w_o = w_out[l].astype(BF16)
        w1 = w_mlp_in[l].astype(BF16)
        w2 = w_mlp_out[l].astype(BF16)
        g1 = g_pre_mix[l].reshape(1, d)
        g2 = g_post_mix[l].reshape(1, d)
        g3 = g_pre_mlp[l].reshape(1, d)
        g4 = g_post_mlp[l].reshape(1, d)
        wg = jnp.zeros((2, LANES, gla_kw), F32)
        wg = wg.at[0, :GLA_GATE_RANK].set(gla_w_gate[l, 0]).at[1, GLA_GATE_RANK:2 * GLA_GATE_RANK].set(gla_w_gate[l, 1])
        wg = wg.astype(BF16)
        bg = gla_b_gate[l].reshape(2, 1, gla_kw)
        gla_gn = gla_g_norm[l].reshape(1, GLA_DV)
        diff_gn = diff_g_norm[l].reshape(1, DIFF_DV)
        lam_init = 0.8 - 0.6 * math.exp(-0.3 * l)

        qk_x = _inproj(x, g1, mods, 0, w_qk, tm_in, diff_qk, rope_tabs, 1, DIFF_Q_SCALE)
        p_x = _inproj(x, g1, mods, 0, w_rest, tm_in, w_rest.shape[1] // 3)
        qk_c = _inproj(xc, g1, mods, bsz, w_qk, tm_ctx, diff_qk, None, 1, DIFF_Q_SCALE).reshape(bsz, n_ctx, -1)
        p_c = _inproj(xc, g1, mods, bsz, w_rest, tm_ctx, w_rest.shape[1] // 3).reshape(bsz, n_ctx, -1)

        ya = _na_attention(p_x, p_c, _na_bias_table(na_rel_bias[l], rows))

        of_c, s_f = _gla_sweep(p_c, wg[0], bg[0], zeros_state, nc_ctx, gla_col0, gate_col0, False)
        yb_c, s_b = _gla_sweep(p_c, wg[1], bg[1], zeros_state, nc_ctx, gla_col0, gate_col0, True, of_c, gla_gn)
        of_x, _ = _gla_sweep(p_x, wg[0], bg[0], s_f, nc_lat, gla_col0, gate_col0, False)
        yb, _ = _gla_sweep(p_x, wg[1], bg[1], s_b, nc_lat, gla_col0, gate_col0, True, of_x, gla_gn)

        vt_x = p_x[:, :, diffv_col0:diffv_col0 + diff_w].reshape(bsz, t // tkc, tkc, diff_heads, DIFF_DV)
        vt_x = vt_x.transpose(0, 3, 1, 4, 2)
        vt_c = p_c[:, :, diffv_col0:diffv_col0 + diff_w].reshape(bsz, n_ctx, diff_heads, DIFF_DV).transpose(0, 2, 3, 1)
        yd = _diff_attention(qk_x, vt_x, qk_c, vt_c, diff_lambda[l], diff_gn, lam_init, tq)

        x = _outproj(ya, yb, yd, w_o, x, g2, mods, 0, tm_out)
        x = _mlp(x, g3, g4, mods, 0, w1, w2, tm_out, 512)

        if need_ctx:
            ya_c = _ctx_attention(p_c, na_heads // 2)
            yd_c = _diff_attention_ctx(qk_c, p_c, diff_lambda[l], diff_gn, lam_init, diffv_col0)
            flat = lambda z: z.reshape(1, bsz * n_ctx, -1)
            xc = _outproj(flat(ya_c), flat(yb_c), flat(yd_c), w_o, xc, g2, mods, bsz, tm_ctx)
            xc = _mlp(xc, g3, g4, mods, bsz, w1, w2, tm_ctx, 512)
    return x
```

```python
import functools
import math

import jax
import jax.numpy as jnp
from jax import lax
from jax.experimental import pallas as pl
from jax.experimental.pallas import tpu as pltpu

F32 = jnp.float32
BF16 = jnp.bfloat16

GRID_W = 64
EPS = 1e-6
NEG_INF = -1e30
HEAD_DIM = 64
NA_WIN_R = 8
NA_WIN_C = 16
GLA_HEADS = 4
GLA_DK = 64
GLA_DV = 128
GLA_GATE_RANK = 16
GLA_NORMALIZER = 16.0
GLA_CHUNK = 64
DIFF_DK = 64
DIFF_DV = 128
ROPE_BASE = 10000.0
DIFF_Q_SCALE = DIFF_DK ** -0.5 * math.log2(math.e)

LANES = 128
SUBLANES = 8
MXU_N = 256
DIFF_VT_ROWS = 128 + 16
SOFTMAX_ROWS = 64
VMEM_LIMIT = 56 * 1024 * 1024

NA_ROWS_PER_STEP = 8
NA_KEY_ROWS = NA_ROWS_PER_STEP + NA_WIN_R


def _params(*sem):
    return pltpu.CompilerParams(dimension_semantics=sem, vmem_limit_bytes=VMEM_LIMIT)


def _nt_dot(a, b):
    return lax.dot_general(a, b, (((1,), (1,)), ((), ())), preferred_element_type=F32)


def _rms(y):
    return y * lax.rsqrt(jnp.mean(y * y, axis=-1, keepdims=True) + EPS)


def _mods_kernel(c_ref, w_ref, b_ref, o_ref):
    c = c_ref[...]
    s = (c * jax.nn.sigmoid(c)).astype(BF16)
    o_ref[0] = jnp.dot(s, w_ref[0].astype(BF16), preferred_element_type=F32) + b_ref[0]


def _mods(cc, w_ada, b_ada):
    depth, d, n = w_ada.shape
    tn = 1024
    return pl.pallas_call(
        _mods_kernel,
        out_shape=jax.ShapeDtypeStruct((depth, 8, n), F32),
        grid=(depth, n // tn),
        in_specs=[
            pl.BlockSpec((8, d), lambda l, j: (0, 0)),
            pl.BlockSpec((1, d, tn), lambda l, j: (l, 0, j)),
            pl.BlockSpec((1, 1, tn), lambda l, j: (l, 0, j)),
        ],
        out_specs=pl.BlockSpec((1, 8, tn), lambda l, j: (l, 0, j)),
        compiler_params=_params("arbitrary", "arbitrary"),
        name="ada_mods",
    )(cc, w_ada, b_ada.reshape(depth, 1, n))


def _inproj_kernel(x_ref, g_ref, shift_ref, scale_ref, w_ref, *rest, rope, q_tiles, q_scale):
    if rope:
        cos_ref, sin_ref, o_ref, h_scr = rest
    else:
        o_ref, h_scr = rest

    @pl.when(pl.program_id(2) == 0)
    def _():
        y = _rms(x_ref[0]) * g_ref[...]
        h_scr[...] = (y * (1.0 + scale_ref[0]) + shift_ref[0]).astype(BF16)

    acc = jnp.dot(h_scr[...], w_ref[...], preferred_element_type=F32)
    if q_tiles:
        acc = acc * jnp.where(pl.program_id(2) < q_tiles, q_scale, 1.0)
    if not rope:
        o_ref[0] = acc.astype(BF16)
        return
    cos = cos_ref[...]
    sin = sin_ref[...]
    lower = (lax.broadcasted_iota(jnp.int32, cos.shape, 1) % DIFF_DK) < DIFF_DK // 2
    for c in range(acc.shape[1] // LANES):
        a = acc[:, c * LANES:(c + 1) * LANES]
        partner = jnp.where(lower, pltpu.roll(a, LANES - DIFF_DK // 2, 1), pltpu.roll(a, DIFF_DK // 2, 1))
        o_ref[0, :, c * LANES:(c + 1) * LANES] = (a * cos + partner * sin).astype(BF16)


def _inproj(x, g, mods, mod_row0, w, tm, tn, rope_tabs=None, q_tiles=0, q_scale=1.0):
    b, t, d = x.shape
    n = w.shape[1]
    rope = rope_tabs is not None
    in_specs = [
        pl.BlockSpec((1, tm, d), lambda bi, i, j: (bi, i, 0)),
        pl.BlockSpec((1, d), lambda bi, i, j: (0, 0)),
        pl.BlockSpec((1, 1, d), lambda bi, i, j: ((bi + mod_row0) * 6 + 0, 0, 0)),
        pl.BlockSpec((1, 1, d), lambda bi, i, j: ((bi + mod_row0) * 6 + 1, 0, 0)),
        pl.BlockSpec((d, tn), lambda bi, i, j: (0, j)),
    ]
    args = [x, g, mods, mods, w]
    if rope:
        in_specs += [pl.BlockSpec((tm, LANES), lambda bi, i, j: (i, 0))] * 2
        args += list(rope_tabs)
    return pl.pallas_call(
        functools.partial(_inproj_kernel, rope=rope, q_tiles=q_tiles, q_scale=q_scale),
        out_shape=jax.ShapeDtypeStruct((b, t, n), BF16),
        grid=(b, t // tm, n // tn),
        in_specs=in_specs,
        out_specs=pl.BlockSpec((1, tm, tn), lambda bi, i, j: (bi, i, j)),
        scratch_shapes=[pltpu.VMEM((tm, d), BF16)],
        compiler_params=_params("parallel", "parallel", "arbitrary"),
        name="inproj_rope" if rope else "inproj",
    )(*args)


def _na_bias_table(rel_bias, rows):
    rq, kwin = NA_ROWS_PER_STEP, NA_KEY_ROWS
    nb = rows // rq
    h = rel_bias.shape[0]
    blocks = jnp.array([0, min(1, nb - 1), nb - 1], jnp.int32)
    r = blocks[:, None] * rq + jnp.arange(rq, dtype=jnp.int32)[None, :]
    kst = jnp.clip(blocks * rq - NA_WIN_R // 2, 0, rows - kwin)
    kr = kst[:, None] + jnp.arange(kwin, dtype=jnp.int32)[None, :]
    s_r = jnp.clip(r - NA_WIN_R // 2, 0, rows - NA_WIN_R)
    row_ok = (kr[:, None, :] >= s_r[:, :, None]) & (kr[:, None, :] < s_r[:, :, None] + NA_WIN_R)
    dr = jnp.clip(kr[:, None, :] - r[:, :, None] + (NA_WIN_R - 1), 0, 2 * NA_WIN_R - 2)
    col = jnp.arange(GRID_W, dtype=jnp.int32)
    c0 = jnp.clip(col - NA_WIN_C // 2, 0, GRID_W - NA_WIN_C)
    col_ok = (col[None, :] >= c0[:, None]) & (col[None, :] < c0[:, None] + NA_WIN_C)
    dc = jnp.clip(col[None, :] - col[:, None] + (NA_WIN_C - 1), 0, 2 * NA_WIN_C - 2)
    pick_dc = (dc[:, :, None] == jnp.arange(2 * NA_WIN_C - 1)[None, None, :]).astype(F32)
    pick_dr = (dr[..., None] == jnp.arange(2 * NA_WIN_R - 1)[None, None, None, :]).astype(F32)
    by_col = jnp.einsum('hrd,qkd->hrqk', rel_bias.astype(F32), pick_dc, precision=lax.Precision.HIGHEST)
    bias = jnp.einsum('vqlr,hrab->vhqalb', pick_dr, by_col, precision=lax.Precision.HIGHEST)
    ok = row_ok[:, None, :, None, :, None] & col_ok[None, None, None, :, None, :]
    bias = jnp.where(ok, bias, NEG_INF)
    return bias.reshape(3, h, rq * GRID_W, kwin * GRID_W)


def _na_kernel(q_ref, k_ref, v_ref, kc_ref, vc_ref, bias_ref, o_ref, *, rows):
    i = pl.program_id(2)
    kstart = jnp.clip(i * NA_ROWS_PER_STEP - NA_WIN_R // 2, 0, rows - NA_KEY_ROWS)
    off = pl.multiple_of(kstart * GRID_W, GRID_W)
    kw = k_ref[0, pl.ds(off, NA_KEY_ROWS * GRID_W), :]
    vw = v_ref[0, pl.ds(off, NA_KEY_ROWS * GRID_W), :]
    q = q_ref[0]
    kc = kc_ref[0]
    vc = vc_ref[0]
    scale = HEAD_DIM ** -0.5
    lane = lax.broadcasted_iota(jnp.int32, q.shape, 1)
    outs = []
    for hh in range(LANES // HEAD_DIM):
        own = (lane >= hh * HEAD_DIM) & (lane < (hh + 1) * HEAD_DIM)
        qh = jnp.where(own, q, 0.0).astype(BF16)
        s_loc = _nt_dot(qh, kw) * scale + bias_ref[0, hh]
        s_ctx = _nt_dot(qh, kc) * scale
        m = jnp.maximum(jnp.max(s_loc, axis=-1, keepdims=True), jnp.max(s_ctx, axis=-1, keepdims=True))
        p_loc = jnp.exp(s_loc - m)
        p_ctx = jnp.exp(s_ctx - m)
        l = jnp.sum(p_loc, axis=-1, keepdims=True) + jnp.sum(p_ctx, axis=-1, keepdims=True)
        o = jnp.dot(p_loc.astype(BF16), vw, preferred_element_type=F32)
        o = o + jnp.dot(p_ctx.astype(BF16), vc, preferred_element_type=F32)
        outs.append(o / l)
    o_ref[0] = jnp.where(lane < HEAD_DIM, outs[0], outs[1]).astype(BF16)


def _na_attention(p_lat, p_ctx, bias_tbl):
    b, t, _ = p_lat.shape
    n = p_ctx.shape[1]
    rows = t // GRID_W
    nb = rows // NA_ROWS_PER_STEP
    hp = bias_tbl.shape[1] // 2
    tq = NA_ROWS_PER_STEP * GRID_W

    def variant(i):
        return jnp.where(i == 0, 0, jnp.where(i == nb - 1, 2, 1))

    return pl.pallas_call(
        functools.partial(_na_kernel, rows=rows),
        out_shape=jax.ShapeDtypeStruct((b, t, hp * LANES), BF16),
        grid=(b, hp, nb),
        in_specs=[
            pl.BlockSpec((1, tq, LANES), lambda bi, h, i: (bi, i, h)),
            pl.BlockSpec((1, t, LANES), lambda bi, h, i: (bi, 0, hp + h)),
            pl.BlockSpec((1, t, LANES), lambda bi, h, i: (bi, 0, 2 * hp + h)),
            pl.BlockSpec((1, n, LANES), lambda bi, h, i: (bi, 0, hp + h)),
            pl.BlockSpec((1, n, LANES), lambda bi, h, i: (bi, 0, 2 * hp + h)),
            pl.BlockSpec((1, 2, tq, NA_KEY_ROWS * GRID_W), lambda bi, h, i: (variant(i), h, 0, 0)),
        ],
        out_specs=pl.BlockSpec((1, tq, LANES), lambda bi, h, i: (bi, i, h)),
        compiler_params=_params("parallel", "parallel", "arbitrary"),
        name="na_attention",
    )(p_lat, p_lat, p_lat, p_ctx, p_ctx, bias_tbl)


def _ctx_attn_kernel(q_ref, k_ref, v_ref, o_ref):
    q, k, v = q_ref[0], k_ref[0], v_ref[0]
    scale = HEAD_DIM ** -0.5
    lane = lax.broadcasted_iota(jnp.int32, q.shape, 1)
    outs = []
    for hh in range(LANES // HEAD_DIM):
        own = (lane >= hh * HEAD_DIM) & (lane < (hh + 1) * HEAD_DIM)
        s = _nt_dot(jnp.where(own, q, 0.0).astype(BF16), k) * scale
        p = jnp.exp(s - jnp.max(s, axis=-1, keepdims=True))
        l = jnp.sum(p, axis=-1, keepdims=True)
        outs.append(jnp.dot(p.astype(BF16), v, preferred_element_type=F32) / l)
    o_ref[0] = jnp.where(lane < HEAD_DIM, outs[0], outs[1]).astype(BF16)


def _ctx_attention(p_ctx, hp):
    b, n, _ = p_ctx.shape
    return pl.pallas_call(
        _ctx_attn_kernel,
        out_shape=jax.ShapeDtypeStruct((b, n, hp * LANES), BF16),
        grid=(b, hp),
        in_specs=[
            pl.BlockSpec((1, n, LANES), lambda bi, h: (bi, 0, h)),
            pl.BlockSpec((1, n, LANES), lambda bi, h: (bi, 0, hp + h)),
            pl.BlockSpec((1, n, LANES), lambda bi, h: (bi, 0, 2 * hp + h)),
        ],
        out_specs=pl.BlockSpec((1, n, LANES), lambda bi, h: (bi, 0, h)),
        compiler_params=_params("parallel", "parallel"),
        name="ctx_attention",
    )(p_ctx, p_ctx, p_ctx)


def _gla_kernel(*refs, backward, nc):
    if backward:
        (q_ref, k_ref, v_ref, a_ref, wg_ref, bg_ref, s0_ref, of_ref, g_ref, gn_ref,
         o_ref, sout_ref, s_scr) = refs
    else:
        q_ref, k_ref, v_ref, a_ref, wg_ref, bg_ref, s0_ref, o_ref, sout_ref, s_scr = refs
    j = pl.program_id(1)
    tb = nc * GLA_CHUNK
    kw = GLA_HEADS * GLA_DK

    @pl.when(j == 0)
    def _():
        s_scr[...] = s0_ref[0]

    pre = jnp.dot(a_ref[0], wg_ref[...], preferred_element_type=F32) + bg_ref[...]
    log_a = (jnp.minimum(pre, 0.0) - jnp.log1p(jnp.exp(-jnp.abs(pre)))) * (1.0 / GLA_NORMALIZER)

    pos = lax.broadcasted_iota(jnp.int32, (tb, kw), 0) % GLA_CHUNK
    bsum = log_a
    step = 1
    while step < GLA_CHUNK:
        if backward:
            bsum = bsum + jnp.where(pos < GLA_CHUNK - step, pltpu.roll(bsum, tb - step, 0), 0.0)
        else:
            bsum = bsum + jnp.where(pos >= step, pltpu.roll(bsum, step, 0), 0.0)
        step *= 2

    b3 = bsum.reshape(nc, GLA_CHUNK, kw)
    b_end = b3[:, 0:1, :] if backward else b3[:, GLA_CHUNK - 1:GLA_CHUNK, :]
    q3 = q_ref[0].astype(F32).reshape(nc, GLA_CHUNK, kw)
    k3 = k_ref[0].astype(F32).reshape(nc, GLA_CHUNK, kw)
    q_in = q3 * (GLA_DK ** -0.5) * jnp.exp(b3)
    k_out = (k3 * jnp.exp(-b3)).astype(BF16)
    k_dec = (k3 * jnp.exp(b_end - b3)).astype(BF16)
    decay = jnp.exp(b_end)
    v3 = v_ref[0].reshape(nc, GLA_CHUNK, GLA_HEADS * GLA_DV)

    t_idx = lax.broadcasted_iota(jnp.int32, (nc, GLA_CHUNK, GLA_CHUNK), 1)
    s_idx = lax.broadcasted_iota(jnp.int32, (nc, GLA_CHUNK, GLA_CHUNK), 2)
    tri = (s_idx > t_idx) if backward else (s_idx <= t_idx)
    eye = (lax.broadcasted_iota(jnp.int32, (nc, GLA_DV, GLA_DV), 1)
           == lax.broadcasted_iota(jnp.int32, (nc, GLA_DV, GLA_DV), 2)).astype(BF16)
    lane = lax.broadcasted_iota(jnp.int32, (nc, GLA_CHUNK, LANES), 2)
    order = range(nc - 1, -1, -1) if backward else range(nc)

    outs = []
    for h in range(GLA_HEADS):
        pair = slice((h // 2) * LANES, (h // 2 + 1) * LANES)
        own = (lane >= GLA_DK) if h % 2 else (lane < GLA_DK)
        qm = jnp.where(own, q_in[:, :, pair], 0.0).astype(BF16)
        vh = v3[:, :, h * GLA_DV:(h + 1) * GLA_DV]
        att = jnp.einsum('ctd,csd->cts', qm, k_out[:, :, pair], preferred_element_type=F32)
        att = jnp.where(tri, att, 0.0).astype(BF16)
        o_h = jnp.einsum('cts,csv->ctv', att, vh, preferred_element_type=F32)
        vt = jnp.einsum('cvw,csw->cvs', eye, vh, preferred_element_type=F32).astype(BF16)
        ut = jnp.einsum('cvs,csd->cvd', vt, k_dec[:, :, pair], preferred_element_type=F32)
        s_h = s_scr[h]
        before = [None] * nc
        for c in order:
            before[c] = s_h
            s_h = s_h * decay[c, :, pair] + ut[c]
        s_scr[h] = s_h
        s_all = jnp.stack(before, axis=0).astype(BF16)
        o_h = o_h + jnp.einsum('ctd,cvd->ctv', qm, s_all, preferred_element_type=F32)
        outs.append(o_h.reshape(tb, GLA_DV))

    @pl.when(j == pl.num_programs(1) - 1)
    def _():
        sout_ref[0] = s_scr[...]

    if not backward:
        o_ref[0] = jnp.concatenate(outs, axis=-1)
        return
    of = of_ref[0]
    g = g_ref[0].astype(F32)
    fin = []
    for h in range(GLA_HEADS):
        sl = slice(h * GLA_DV, (h + 1) * GLA_DV)
        y = _rms(of[:, sl] + outs[h]) * gn_ref[...]
        gh = g[:, sl]
        fin.append(y * (gh * jax.nn.sigmoid(gh)))
    o_ref[0] = jnp.concatenate(fin, axis=-1).astype(BF16)


def _gla_sweep(p, wg, bg, s0, nc, col0, a_col, backward, o_fwd=None, g_norm=None):
    b, n, _ = p.shape
    tb = nc * GLA_CHUNK
    nblk = n // tb
    kw = GLA_HEADS * GLA_DK
    vw = GLA_HEADS * GLA_DV
    cq, ck, cv, cg, ca = col0 // kw, col0 // kw + 1, (col0 + 2 * kw) // vw, (col0 + 2 * kw) // vw + 1, a_col // LANES

    def blk(j):
        return nblk - 1 - j if backward else j

    in_specs = [
        pl.BlockSpec((1, tb, kw), lambda bi, j: (bi, blk(j), cq)),
        pl.BlockSpec((1, tb, kw), lambda bi, j: (bi, blk(j), ck)),
        pl.BlockSpec((1, tb, vw), lambda bi, j: (bi, blk(j), cv)),
        pl.BlockSpec((1, tb, LANES), lambda bi, j: (bi, blk(j), ca)),
        pl.BlockSpec((LANES, kw), lambda bi, j: (0, 0)),
        pl.BlockSpec((1, kw), lambda bi, j: (0, 0)),
        pl.BlockSpec((1, GLA_HEADS, GLA_DV, LANES), lambda bi, j: (bi, 0, 0, 0)),
    ]
    args = [p, p, p, p, wg, bg, s0]
    if backward:
        in_specs += [
            pl.BlockSpec((1, tb, vw), lambda bi, j: (bi, blk(j), 0)),
            pl.BlockSpec((1, tb, vw), lambda bi, j: (bi, blk(j), cg)),
            pl.BlockSpec((1, GLA_DV), lambda bi, j: (0, 0)),
        ]
        args += [o_fwd, p, g_norm]
    return pl.pallas_call(
        functools.partial(_gla_kernel, backward=backward, nc=nc),
        out_shape=(jax.ShapeDtypeStruct((b, n, vw), BF16 if backward else F32),
                   jax.ShapeDtypeStruct((b, GLA_HEADS, GLA_DV, LANES), F32)),
        grid=(b, nblk),
        in_specs=in_specs,
        out_specs=(pl.BlockSpec((1, tb, vw), lambda bi, j: (bi, blk(j), 0)),
                   pl.BlockSpec((1, GLA_HEADS, GLA_DV, LANES), lambda bi, j: (bi, 0, 0, 0))),
        scratch_shapes=[pltpu.VMEM((GLA_HEADS, GLA_DV, LANES), F32)],
        compiler_params=_params("parallel", "arbitrary"),
        name="gla_bwd" if backward else "gla_fwd",
    )(*args)


def _diff_lambda(lam_ref, lam_init):
    lp = lam_ref[...]
    return (jnp.exp(jnp.sum(lp[0:1] * lp[1:2], axis=-1, keepdims=True))
            - jnp.exp(jnp.sum(lp[2:3] * lp[3:4], axis=-1, keepdims=True)) + lam_init)


def _branch_queries(q):
    lane = lax.broadcasted_iota(jnp.int32, q.shape, 1)
    return jnp.concatenate([jnp.where(lane < DIFF_DK, q, 0.0), jnp.where(lane >= DIFF_DK, q, 0.0)], axis=0)


def _diff_ctx_kernel(q_ref, k_ref, v_ref, lam_ref, gn_ref, o_ref, *, lam_init):
    tq = q_ref.shape[1]
    s = _nt_dot(_branch_queries(q_ref[0]), k_ref[0])
    p = jnp.exp2(s - jnp.max(s, axis=-1, keepdims=True))
    a = jnp.dot(p.astype(BF16), v_ref[0], preferred_element_type=F32) / jnp.sum(p, axis=-1, keepdims=True)
    o = a[0:tq] - _diff_lambda(lam_ref, lam_init) * a[tq:2 * tq]
    o_ref[0] = (_rms(o) * gn_ref[...] * (1.0 - lam_init)).astype(BF16)


def _diff_attention_ctx(qk_c, p_c, lam_params, g_norm, lam_init, v_col0):
    b, n, w = qk_c.shape
    heads = w // (2 * LANES)
    vb = v_col0 // LANES
    return pl.pallas_call(
        functools.partial(_diff_ctx_kernel, lam_init=lam_init),
        out_shape=jax.ShapeDtypeStruct((b, n, heads * DIFF_DV), BF16),
        grid=(b, heads),
        in_specs=[
            pl.BlockSpec((1, n, LANES), lambda bi, h: (bi, 0, h)),
            pl.BlockSpec((1, n, LANES), lambda bi, h: (bi, 0, heads + h)),
            pl.BlockSpec((1, n, LANES), lambda bi, h: (bi, 0, vb + h)),
            pl.BlockSpec((4, DIFF_DK), lambda bi, h: (0, 0)),
            pl.BlockSpec((1, DIFF_DV), lambda bi, h: (0, 0)),
        ],
        out_specs=pl.BlockSpec((1, n, LANES), lambda bi, h: (bi, 0, h)),
        compiler_params=_params("parallel", "parallel"),
        name="diff_attention_ctx",
    )(qk_c, qk_c, p_c, lam_params, g_norm)


def _diff_kernel(q_ref, k_ref, vt_ref, kc_ref, vct_ref, lam_ref, gn_ref, o_ref,
                 qs, s_buf, c_buf, p_buf, a_buf, m_scr, acc, *, nch, lam_init):
    tq = q_ref.shape[1]
    tkc = k_ref.shape[1] // nch

    def stage(value_chunk, softmax_on, score_chunk):
        for n in range(2 * tq // MXU_N):
            sl = slice(n * MXU_N, (n + 1) * MXU_N)
            if value_chunk is not None:
                pv = jnp.dot(vt_ref[0, 0, value_chunk], p_buf[:, sl], preferred_element_type=F32)
                acc[:, sl] = acc[:, sl] * a_buf[:, sl] + pv
            if softmax_on:
                m_old = m_scr[:, sl]
                m_new = jnp.maximum(m_old, c_buf[:, sl])
                a_buf[:, sl] = jnp.exp2(m_old - m_new)
                m_scr[:, sl] = m_new
                for r in range(tkc // SOFTMAX_ROWS):
                    rows = slice(r * SOFTMAX_ROWS, (r + 1) * SOFTMAX_ROWS)
                    p_buf[rows, sl] = jnp.exp2(s_buf[rows, sl] - m_new).astype(BF16)
            if score_chunk is not None:
                k = k_ref[0, pl.ds(pl.multiple_of(score_chunk * tkc, tkc), tkc), :]
                s = _nt_dot(k, qs[sl, :])
                s_buf[:, sl] = s
                c_buf[:, sl] = jnp.max(s, axis=0, keepdims=True)

    qs[...] = _branch_queries(q_ref[0])
    sc = _nt_dot(kc_ref[0], qs[...])
    mc = jnp.max(sc, axis=0, keepdims=True)
    pc = jnp.exp2(sc - mc)
    m_scr[...] = mc
    acc[...] = jnp.dot(vct_ref[0, 0], pc.astype(BF16), preferred_element_type=F32)
    stage(None, False, 0)
    stage(None, True, 1)

    def steady(c, carry):
        stage(c - 1, True, c + 1)
        return carry

    lax.fori_loop(1, nch - 1, steady, 0)
    stage(nch - 2, True, None)
    stage(nch - 1, False, None)

    a = acc[0:DIFF_DV, :] / acc[DIFF_DV:DIFF_DV + 1, :]
    o = a[:, 0:tq] - _diff_lambda(lam_ref, lam_init) * a[:, tq:2 * tq]
    y = o * lax.rsqrt(jnp.mean(o * o, axis=0, keepdims=True) + EPS) * gn_ref[...] * (1.0 - lam_init)
    o_ref[0] = y.T.astype(BF16)


def _diff_attention(qk_x, vt_x, qk_c, vt_c, lam_params, g_norm, lam_init, tq):
    b, t, w = qk_x.shape
    heads = w // (2 * LANES)
    n_ctx = qk_c.shape[1]
    nch, tkc = vt_x.shape[2], vt_x.shape[4]
    assert nch >= 2 and nch * tkc == t
    return pl.pallas_call(
        functools.partial(_diff_kernel, nch=nch, lam_init=lam_init),
        out_shape=jax.ShapeDtypeStruct((b, t, heads * DIFF_DV), BF16),
        grid=(b, heads, t // tq),
        in_specs=[
            pl.BlockSpec((1, tq, LANES), lambda bi, h, i: (bi, i, h)),
            pl.BlockSpec((1, t, LANES), lambda bi, h, i: (bi, 0, heads + h)),
            pl.BlockSpec((1, 1, nch, DIFF_VT_ROWS, tkc), lambda bi, h, i: (bi, h, 0, 0, 0)),
            pl.BlockSpec((1, n_ctx, LANES), lambda bi, h, i: (bi, 0, heads + h)),
            pl.BlockSpec((1, 1, DIFF_VT_ROWS, n_ctx), lambda bi, h, i: (bi, h, 0, 0)),
            pl.BlockSpec((4, DIFF_DK), lambda bi, h, i: (0, 0)),
            pl.BlockSpec((DIFF_DV, 1), lambda bi, h, i: (0, 0)),
        ],
        out_specs=pl.BlockSpec((1, tq, LANES), lambda bi, h, i: (bi, i, h)),
        scratch_shapes=[
            pltpu.VMEM((2 * tq, LANES), BF16),
            pltpu.VMEM((tkc, 2 * tq), F32),
            pltpu.VMEM((1, 2 * tq), F32),
            pltpu.VMEM((tkc, 2 * tq), BF16),
            pltpu.VMEM((1, 2 * tq), F32),
            pltpu.VMEM((1, 2 * tq), F32),
            pltpu.VMEM((DIFF_VT_ROWS, 2 * tq), F32),
        ],
        compiler_params=_params("parallel", "parallel", "arbitrary"),
        name="diff_attention",
    )(qk_x, qk_x, vt_x, qk_c, vt_c, lam_params, g_norm.reshape(DIFF_DV, 1))


def _outproj_kernel(ya_ref, yb_ref, yd_ref, w_ref, x_ref, g_ref, gate_ref, o_ref):
    wa, wb = ya_ref.shape[2], yb_ref.shape[2]
    y = jnp.dot(ya_ref[0], w_ref[0:wa, :], preferred_element_type=F32)
    y = y + jnp.dot(yb_ref[0], w_ref[wa:wa + wb, :], preferred_element_type=F32)
    y = y + jnp.dot(yd_ref[0], w_ref[wa + wb:, :], preferred_element_type=F32)
    o_ref[0] = x_ref[0] + gate_ref[0] * (_rms(y) * g_ref[...])


def _outproj(ya, yb, yd, w, x, g, mods, mod_row0, tm):
    b, t, d = x.shape
    return pl.pallas_call(
        _outproj_kernel,
        out_shape=jax.ShapeDtypeStruct((b, t, d), F32),
        grid=(b, t // tm),
        in_specs=[
            pl.BlockSpec((1, tm, ya.shape[2]), lambda bi, i: (bi, i, 0)),
            pl.BlockSpec((1, tm, yb.shape[2]), lambda bi, i: (bi, i, 0)),
            pl.BlockSpec((1, tm, yd.shape[2]), lambda bi, i: (bi, i, 0)),
            pl.BlockSpec(w.shape, lambda bi, i: (0, 0)),
            pl.BlockSpec((1, tm, d), lambda bi, i: (bi, i, 0)),
            pl.BlockSpec((1, d), lambda bi, i: (0, 0)),
            pl.BlockSpec((1, 1, d), lambda bi, i: ((bi + mod_row0) * 6 + 2, 0, 0)),
        ],
        out_specs=pl.BlockSpec((1, tm, d), lambda bi, i: (bi, i, 0)),
        compiler_params=_params("parallel", "parallel"),
        name="outproj",
    )(ya, yb, yd, w, x, g, mods)


def _mlp_kernel(x_ref, gpre_ref, shift_ref, scale_ref, w1_ref, w2_ref, gpost_ref, gate_ref, o_ref, h_scr, acc):
    k = pl.program_id(2)

    @pl.when(k == 0)
    def _():
        y = _rms(x_ref[0]) * gpre_ref[...]
        h_scr[...] = (y * (1.0 + scale_ref[0]) + shift_ref[0]).astype(BF16)
        acc[...] = jnp.zeros(acc.shape, F32)

    u = jnp.maximum(jnp.dot(h_scr[...], w1_ref[...], preferred_element_type=F32), 0.0)
    acc[...] += jnp.dot((u * u).astype(BF16), w2_ref[...], preferred_element_type=F32)

    @pl.when(k == pl.num_programs(2) - 1)
    def _():
        o_ref[0] = x_ref[0] + gate_ref[0] * (_rms(acc[...]) * gpost_ref[...])


def _mlp(x, gpre, gpost, mods, mod_row0, w1, w2, tm, th):
    b, t, d = x.shape
    hid = w1.shape[1]

    def mod(comp):
        return pl.BlockSpec((1, 1, d), lambda bi, i, k: ((bi + mod_row0) * 6 + comp, 0, 0))

    return pl.pallas_call(
        _mlp_kernel,
        out_shape=jax.ShapeDtypeStruct((b, t, d), F32),
        grid=(b, t // tm, hid // th),
        in_specs=[
            pl.BlockSpec((1, tm, d), lambda bi, i, k: (bi, i, 0)),
            pl.BlockSpec((1, d), lambda bi, i, k: (0, 0)),
            mod(3),
            mod(4),
            pl.BlockSpec((d, th), lambda bi, i, k: (0, k)),
            pl.BlockSpec((th, d), lambda bi, i, k: (k, 0)),
            pl.BlockSpec((1, d), lambda bi, i, k: (0, 0)),
            mod(5),
        ],
        out_specs=pl.BlockSpec((1, tm, d), lambda bi, i, k: (bi, i, 0)),
        scratch_shapes=[pltpu.VMEM((tm, d), BF16), pltpu.VMEM((tm, d), F32)],
        compiler_params=_params("parallel", "parallel", "arbitrary"),
        name="mlp",
    )(x, gpre, mods, mods, w1, w2, gpost, mods)


def _rope_tables(t):
    tok = jnp.arange(t, dtype=jnp.int32)
    row = (tok // GRID_W).astype(F32)
    col = (tok % GRID_W).astype(F32)
    n_freq = DIFF_DK // 4
    inv = ROPE_BASE ** (-jnp.arange(n_freq, dtype=F32) / n_freq)
    ang = jnp.concatenate([row[:, None] * inv, col[:, None] * inv], axis=-1)
    cos, sin = jnp.cos(ang), jnp.sin(ang)
    reps = LANES // DIFF_DK
    return (jnp.tile(jnp.concatenate([cos, cos], axis=-1), (1, reps)),
            jnp.tile(jnp.concatenate([-sin, sin], axis=-1), (1, reps)))


def _with_ones_row(vt):
    pad = jnp.zeros(vt.shape[:-2] + (DIFF_VT_ROWS - DIFF_DV, vt.shape[-1]), vt.dtype).at[..., 0, :].set(1.0)
    return jnp.concatenate([vt, pad], axis=-2)


def _pick(n, pref):
    return pref if n % pref == 0 else n


def kernel(x, c, ctx, c_ctx, w_ada, b_ada, g_pre_mix, g_post_mix, g_pre_mlp, g_post_mlp, w_in, w_out,
           na_rel_bias, gla_w_gate, gla_b_gate, gla_g_norm, diff_lambda, diff_g_norm, w_mlp_in, w_mlp_out):
    bsz, t, d = x.shape
    n_ctx = ctx.shape[1]
    depth = w_ada.shape[0]
    na_heads = na_rel_bias.shape[1]
    na_w = na_heads * HEAD_DIM
    gla_kw = GLA_HEADS * GLA_DK
    gla_vw = GLA_HEADS * GLA_DV
    diff_w = w_out.shape[1] - na_w - gla_vw
    diff_heads = diff_w // DIFF_DV
    diff_qk = diff_heads * 2 * DIFF_DK
    rows = t // GRID_W
    assert t % (NA_ROWS_PER_STEP * GRID_W) == 0 and rows >= NA_KEY_ROWS
    assert bsz + 1 <= 8

    o_gla = 3 * na_w
    o_gate = o_gla + 2 * gla_kw + 2 * gla_vw
    o_diff = o_gate + 2 * GLA_GATE_RANK
    gla_col0 = 3 * na_w
    diffv_col0 = gla_col0 + 2 * gla_kw + 2 * gla_vw
    gate_col0 = diffv_col0 + diff_w

    cc = jnp.zeros((8, d), F32).at[:bsz].set(c).at[bsz].set(c_ctx)
    mods_all = _mods(cc, w_ada, b_ada)
    rope_tabs = _rope_tables(t)
    xc = ctx.reshape(1, bsz * n_ctx, d)
    zeros_state = jnp.zeros((bsz, GLA_HEADS, GLA_DV, LANES), F32)

    tm_in = _pick(t, 1024)
    tm_out = _pick(t, 512)
    tm_ctx = _pick(bsz * n_ctx, 512)
    tq = _pick(t, 512)
    tkc = min(2048, t // 2)
    nc_lat = _pick(t // GLA_CHUNK, 8)
    nc_ctx = _pick(n_ctx // GLA_CHUNK, 4)

    for l in range(depth):
        need_ctx = l < depth - 1
        mods = mods_all[l].reshape(8 * 6, 1, d)
        wl = w_in[l]
        w_qk = wl[:, o_diff:o_diff + 2 * diff_qk].astype(BF16)
        w_rest = jnp.concatenate(
            [wl[:, :o_gate], wl[:, o_diff + 2 * diff_qk:], wl[:, o_gate:o_diff],
             jnp.zeros((d, LANES - 2 * GLA_GATE_RANK), F32)], axis=1).astype(BF16)
        w_o = w_out[l].astype(BF16)
        w1 = w_mlp_in[l].astype(BF16)
        w2 = w_mlp_out[l].astype(BF16)
        g1 = g_pre_mix[l].reshape(1, d)
        g2 = g_post_mix[l].reshape(1, d)
        g3 = g_pre_mlp[l].reshape(1, d)
        g4 = g_post_mlp[l].reshape(1, d)
        wg = jnp.zeros((2, LANES, gla_kw), F32)
        wg = wg.at[0, :GLA_GATE_RANK].set(gla_w_gate[l, 0]).at[1, GLA_GATE_RANK:2 * GLA_GATE_RANK].set(gla_w_gate[l, 1])
        wg = wg.astype(BF16)
        bg = gla_b_gate[l].reshape(2, 1, gla_kw)
        gla_gn = gla_g_norm[l].reshape(1, GLA_DV)
        diff_gn = diff_g_norm[l].reshape(1, DIFF_DV)
        lam_init = 0.8 - 0.6 * math.exp(-0.3 * l)

        qk_x = _inproj(x, g1, mods, 0, w_qk, tm_in, diff_qk, rope_tabs, 1, DIFF_Q_SCALE)
        p_x = _inproj(x, g1, mods, 0, w_rest, tm_in, w_rest.shape[1] // 3)
        qk_c = _inproj(xc, g1, mods, bsz, w_qk, tm_ctx, diff_qk, None, 1, DIFF_Q_SCALE).reshape(bsz, n_ctx, -1)
        p_c = _inproj(xc, g1, mods, bsz, w_rest, tm_ctx, w_rest.shape[1] // 3).reshape(bsz, n_ctx, -1)

        ya = _na_attention(p_x, p_c, _na_bias_table(na_rel_bias[l], rows))

        of_c, s_f = _gla_sweep(p_c, wg[0], bg[0], zeros_state, nc_ctx, gla_col0, gate_col0, False)
        yb_c, s_b = _gla_sweep(p_c, wg[1], bg[1], zeros_state, nc_ctx, gla_col0, gate_col0, True, of_c, gla_gn)
        of_x, _ = _gla_sweep(p_x, wg[0], bg[0], s_f, nc_lat, gla_col0, gate_col0, False)
        yb, _ = _gla_sweep(p_x, wg[1], bg[1], s_b, nc_lat, gla_col0, gate_col0, True, of_x, gla_gn)

        vt_x = p_x[:, :, diffv_col0:diffv_col0 + diff_w].reshape(bsz, t // tkc, tkc, diff_heads, DIFF_DV)
        vt_x = _with_ones_row(vt_x.transpose(0, 3, 1, 4, 2))
        vt_c = p_c[:, :, diffv_col0:diffv_col0 + diff_w].reshape(bsz, n_ctx, diff_heads, DIFF_DV)
        vt_c = _with_ones_row(vt_c.transpose(0, 2, 3, 1))
        yd = _diff_attention(qk_x, vt_x, qk_c, vt_c, diff_lambda[l], diff_gn, lam_init, tq)

        x = _outproj(ya, yb, yd, w_o, x, g2, mods, 0, tm_out)
        x = _mlp(x, g3, g4, mods, 0, w1, w2, tm_out, 512)

        if need_ctx:
            ya_c = _ctx_attention(p_c, na_heads // 2)
            yd_c = _diff_attention_ctx(qk_c, p_c, diff_lambda[l], diff_gn, lam_init, diffv_col0)
            flat = lambda z: z.reshape(1, bsz * n_ctx, -1)
            xc = _outproj(flat(ya_c), flat(yb_c), flat(yd_c), w_o, xc, g2, mods, bsz, tm_ctx)
            xc = _mlp(xc, g3, g4, mods, bsz, w1, w2, tm_ctx, 512)
    return x
```

```python
import functools
import math

import jax
import jax.numpy as jnp
from jax import lax
from jax.experimental import pallas as pl
from jax.experimental.pallas import tpu as pltpu

F32 = jnp.float32
BF16 = jnp.bfloat16

GRID_W = 64
EPS = 1e-6
NEG_INF = -1e30
HEAD_DIM = 64
NA_WIN_R = 8
NA_WIN_C = 16
GLA_HEADS = 4
GLA_DK = 64
GLA_DV = 128
GLA_GATE_RANK = 16
GLA_NORMALIZER = 16.0
GLA_CHUNK = 64
DIFF_DK = 64
DIFF_DV = 128
ROPE_BASE = 10000.0
DIFF_Q_SCALE = DIFF_DK ** -0.5 * math.log2(math.e)

LANES = 128
SUBLANES = 8
MXU_N = 256
DIFF_VT_ROWS = 128 + 16
SOFTMAX_ROWS = 64
VMEM_LIMIT = 56 * 1024 * 1024

MLP_HIDDEN_TILE = 1024
OUTPROJ_ROW_GROUP = 256
NA_ROWS_PER_STEP = 8
NA_KEY_ROWS = NA_ROWS_PER_STEP + NA_WIN_R


def _params(*sem):
    return pltpu.CompilerParams(dimension_semantics=sem, vmem_limit_bytes=VMEM_LIMIT)


def _nt_dot(a, b):
    return lax.dot_general(a, b, (((1,), (1,)), ((), ())), preferred_element_type=F32)


def _rms(y):
    return y * lax.rsqrt(jnp.mean(y * y, axis=-1, keepdims=True) + EPS)


def _mods_kernel(c_ref, w_ref, b_ref, o_ref):
    c = c_ref[...]
    s = (c * jax.nn.sigmoid(c)).astype(BF16)
    o_ref[0] = jnp.dot(s, w_ref[0].astype(BF16), preferred_element_type=F32) + b_ref[0]


def _mods(cc, w_ada, b_ada):
    depth, d, n = w_ada.shape
    tn = 1024
    return pl.pallas_call(
        _mods_kernel,
        out_shape=jax.ShapeDtypeStruct((depth, 8, n), F32),
        grid=(depth, n // tn),
        in_specs=[
            pl.BlockSpec((8, d), lambda l, j: (0, 0)),
            pl.BlockSpec((1, d, tn), lambda l, j: (l, 0, j)),
            pl.BlockSpec((1, 1, tn), lambda l, j: (l, 0, j)),
        ],
        out_specs=pl.BlockSpec((1, 8, tn), lambda l, j: (l, 0, j)),
        compiler_params=_params("arbitrary", "arbitrary"),
        name="ada_mods",
    )(cc, w_ada, b_ada.reshape(depth, 1, n))


def _modulate(x, g_ref, shift_ref, scale_ref):
    return (_rms(x) * g_ref[...] * (1.0 + scale_ref[0]) + shift_ref[0]).astype(BF16)


def _premix_kernel(x_ref, g_ref, shift_ref, scale_ref, o_ref):
    o_ref[0] = _modulate(x_ref[0], g_ref, shift_ref, scale_ref)


def _premix(x, g, mods, mod_row0, tm):
    b, t, d = x.shape
    return pl.pallas_call(
        _premix_kernel,
        out_shape=jax.ShapeDtypeStruct((b, t, d), BF16),
        grid=(b, t // tm),
        in_specs=[
            pl.BlockSpec((1, tm, d), lambda bi, i: (bi, i, 0)),
            pl.BlockSpec((1, d), lambda bi, i: (0, 0)),
            pl.BlockSpec((1, 1, d), lambda bi, i: ((bi + mod_row0) * 6 + 0, 0, 0)),
            pl.BlockSpec((1, 1, d), lambda bi, i: ((bi + mod_row0) * 6 + 1, 0, 0)),
        ],
        out_specs=pl.BlockSpec((1, tm, d), lambda bi, i: (bi, i, 0)),
        compiler_params=_params("parallel", "parallel"),
        name="premix",
    )(x, g, mods, mods)


def _inproj_kernel(h_ref, w_ref, *rest, rope, q_tiles, q_scale):
    if rope:
        cos_ref, sin_ref, o_ref = rest
    else:
        (o_ref,) = rest
    acc = jnp.dot(h_ref[0], w_ref[...], preferred_element_type=F32)
    if q_tiles:
        acc = acc * jnp.where(pl.program_id(2) < q_tiles, q_scale, 1.0)
    if not rope:
        o_ref[0] = acc.astype(BF16)
        return
    cos = cos_ref[...]
    sin = sin_ref[...]
    lower = (lax.broadcasted_iota(jnp.int32, cos.shape, 1) % DIFF_DK) < DIFF_DK // 2
    for c in range(acc.shape[1] // LANES):
        a = acc[:, c * LANES:(c + 1) * LANES]
        partner = jnp.where(lower, pltpu.roll(a, LANES - DIFF_DK // 2, 1), pltpu.roll(a, DIFF_DK // 2, 1))
        o_ref[0, :, c * LANES:(c + 1) * LANES] = (a * cos + partner * sin).astype(BF16)


def _inproj(h, w, tm, tn, rope_tabs=None, q_tiles=0, q_scale=1.0):
    b, t, d = h.shape
    n = w.shape[1]
    rope = rope_tabs is not None
    in_specs = [
        pl.BlockSpec((1, tm, d), lambda bi, i, j: (bi, i, 0)),
        pl.BlockSpec((d, tn), lambda bi, i, j: (0, j)),
    ]
    args = [h, w]
    if rope:
        in_specs += [pl.BlockSpec((tm, LANES), lambda bi, i, j: (i, 0))] * 2
        args += list(rope_tabs)
    return pl.pallas_call(
        functools.partial(_inproj_kernel, rope=rope, q_tiles=q_tiles, q_scale=q_scale),
        out_shape=jax.ShapeDtypeStruct((b, t, n), BF16),
        grid=(b, t // tm, n // tn),
        in_specs=in_specs,
        out_specs=pl.BlockSpec((1, tm, tn), lambda bi, i, j: (bi, i, j)),
        compiler_params=_params("parallel", "parallel", "parallel"),
        name="inproj_rope" if rope else "inproj",
    )(*args)


def _na_bias_table(rel_bias, rows):
    rq, kwin = NA_ROWS_PER_STEP, NA_KEY_ROWS
    nb = rows // rq
    h = rel_bias.shape[0]
    blocks = jnp.array([0, min(1, nb - 1), nb - 1], jnp.int32)
    r = blocks[:, None] * rq + jnp.arange(rq, dtype=jnp.int32)[None, :]
    kst = jnp.clip(blocks * rq - NA_WIN_R // 2, 0, rows - kwin)
    kr = kst[:, None] + jnp.arange(kwin, dtype=jnp.int32)[None, :]
    s_r = jnp.clip(r - NA_WIN_R // 2, 0, rows - NA_WIN_R)
    row_ok = (kr[:, None, :] >= s_r[:, :, None]) & (kr[:, None, :] < s_r[:, :, None] + NA_WIN_R)
    dr = jnp.clip(kr[:, None, :] - r[:, :, None] + (NA_WIN_R - 1), 0, 2 * NA_WIN_R - 2)
    col = jnp.arange(GRID_W, dtype=jnp.int32)
    c0 = jnp.clip(col - NA_WIN_C // 2, 0, GRID_W - NA_WIN_C)
    col_ok = (col[None, :] >= c0[:, None]) & (col[None, :] < c0[:, None] + NA_WIN_C)
    dc = jnp.clip(col[None, :] - col[:, None] + (NA_WIN_C - 1), 0, 2 * NA_WIN_C - 2)
    pick_dc = (dc[:, :, None] == jnp.arange(2 * NA_WIN_C - 1)[None, None, :]).astype(F32)
    pick_dr = (dr[..., None] == jnp.arange(2 * NA_WIN_R - 1)[None, None, None, :]).astype(F32)
    by_col = jnp.einsum('hrd,qkd->hrqk', rel_bias.astype(F32), pick_dc, precision=lax.Precision.HIGHEST)
    bias = jnp.einsum('vqlr,hrab->vhqalb', pick_dr, by_col, precision=lax.Precision.HIGHEST)
    ok = row_ok[:, None, :, None, :, None] & col_ok[None, None, None, :, None, :]
    bias = jnp.where(ok, bias, NEG_INF)
    return bias.reshape(3, h, rq * GRID_W, kwin * GRID_W)


def _na_kernel(q_ref, k_ref, v_ref, kc_ref, vc_ref, bias_ref, o_ref, *, rows):
    i = pl.program_id(2)
    kstart = jnp.clip(i * NA_ROWS_PER_STEP - NA_WIN_R // 2, 0, rows - NA_KEY_ROWS)
    off = pl.multiple_of(kstart * GRID_W, GRID_W)
    kw = k_ref[0, pl.ds(off, NA_KEY_ROWS * GRID_W), :]
    vw = v_ref[0, pl.ds(off, NA_KEY_ROWS * GRID_W), :]
    q = q_ref[0]
    kc = kc_ref[0]
    vc = vc_ref[0]
    scale = HEAD_DIM ** -0.5
    lane = lax.broadcasted_iota(jnp.int32, q.shape, 1)
    outs = []
    for hh in range(LANES // HEAD_DIM):
        own = (lane >= hh * HEAD_DIM) & (lane < (hh + 1) * HEAD_DIM)
        qh = jnp.where(own, q, 0.0).astype(BF16)
        s_loc = _nt_dot(qh, kw) * scale + bias_ref[0, hh]
        s_ctx = _nt_dot(qh, kc) * scale
        m = jnp.maximum(jnp.max(s_loc, axis=-1, keepdims=True), jnp.max(s_ctx, axis=-1, keepdims=True))
        p_loc = jnp.exp(s_loc - m)
        p_ctx = jnp.exp(s_ctx - m)
        l = jnp.sum(p_loc, axis=-1, keepdims=True) + jnp.sum(p_ctx, axis=-1, keepdims=True)
        o = jnp.dot(p_loc.astype(BF16), vw, preferred_element_type=F32)
        o = o + jnp.dot(p_ctx.astype(BF16), vc, preferred_element_type=F32)
        outs.append(o / l)
    o_ref[0] = jnp.where(lane < HEAD_DIM, outs[0], outs[1]).astype(BF16)


def _na_attention(p_lat, p_ctx, bias_tbl):
    b, t, _ = p_lat.shape
    n = p_ctx.shape[1]
    rows = t // GRID_W
    nb = rows // NA_ROWS_PER_STEP
    hp = bias_tbl.shape[1] // 2
    tq = NA_ROWS_PER_STEP * GRID_W

    def variant(i):
        return jnp.where(i == 0, 0, jnp.where(i == nb - 1, 2, 1))

    return pl.pallas_call(
        functools.partial(_na_kernel, rows=rows),
        out_shape=jax.ShapeDtypeStruct((b, t, hp * LANES), BF16),
        grid=(b, hp, nb),
        in_specs=[
            pl.BlockSpec((1, tq, LANES), lambda bi, h, i: (bi, i, h)),
            pl.BlockSpec((1, t, LANES), lambda bi, h, i: (bi, 0, hp + h)),
            pl.BlockSpec((1, t, LANES), lambda bi, h, i: (bi, 0, 2 * hp + h)),
            pl.BlockSpec((1, n, LANES), lambda bi, h, i: (bi, 0, hp + h)),
            pl.BlockSpec((1, n, LANES), lambda bi, h, i: (bi, 0, 2 * hp + h)),
            pl.BlockSpec((1, 2, tq, NA_KEY_ROWS * GRID_W), lambda bi, h, i: (variant(i), h, 0, 0)),
        ],
        out_specs=pl.BlockSpec((1, tq, LANES), lambda bi, h, i: (bi, i, h)),
        compiler_params=_params("parallel", "parallel", "arbitrary"),
        name="na_attention",
    )(p_lat, p_lat, p_lat, p_ctx, p_ctx, bias_tbl)


def _ctx_attn_kernel(q_ref, k_ref, v_ref, o_ref):
    q, k, v = q_ref[0], k_ref[0], v_ref[0]
    scale = HEAD_DIM ** -0.5
    lane = lax.broadcasted_iota(jnp.int32, q.shape, 1)
    outs = []
    for hh in range(LANES // HEAD_DIM):
        own = (lane >= hh * HEAD_DIM) & (lane < (hh + 1) * HEAD_DIM)
        s = _nt_dot(jnp.where(own, q, 0.0).astype(BF16), k) * scale
        p = jnp.exp(s - jnp.max(s, axis=-1, keepdims=True))
        l = jnp.sum(p, axis=-1, keepdims=True)
        outs.append(jnp.dot(p.astype(BF16), v, preferred_element_type=F32) / l)
    o_ref[0] = jnp.where(lane < HEAD_DIM, outs[0], outs[1]).astype(BF16)


def _ctx_attention(p_ctx, hp):
    b, n, _ = p_ctx.shape
    return pl.pallas_call(
        _ctx_attn_kernel,
        out_shape=jax.ShapeDtypeStruct((b, n, hp * LANES), BF16),
        grid=(b, hp),
        in_specs=[
            pl.BlockSpec((1, n, LANES), lambda bi, h: (bi, 0, h)),
            pl.BlockSpec((1, n, LANES), lambda bi, h: (bi, 0, hp + h)),
            pl.BlockSpec((1, n, LANES), lambda bi, h: (bi, 0, 2 * hp + h)),
        ],
        out_specs=pl.BlockSpec((1, n, LANES), lambda bi, h: (bi, 0, h)),
        compiler_params=_params("parallel", "parallel"),
        name="ctx_attention",
    )(p_ctx, p_ctx, p_ctx)


def _gla_kernel(*refs, backward, nc):
    if backward:
        (q_ref, k_ref, v_ref, a_ref, wg_ref, bg_ref, s0_ref, of_ref, g_ref, gn_ref,
         o_ref, sout_ref, s_scr) = refs
    else:
        q_ref, k_ref, v_ref, a_ref, wg_ref, bg_ref, s0_ref, o_ref, sout_ref, s_scr = refs
    j = pl.program_id(1)
    tb = nc * GLA_CHUNK
    kw = GLA_HEADS * GLA_DK

    @pl.when(j == 0)
    def _():
        s_scr[...] = s0_ref[0]

    pre = jnp.dot(a_ref[0], wg_ref[...], preferred_element_type=F32) + bg_ref[...]
    log_a = (jnp.minimum(pre, 0.0) - jnp.log1p(jnp.exp(-jnp.abs(pre)))) * (1.0 / GLA_NORMALIZER)

    pos = lax.broadcasted_iota(jnp.int32, (tb, kw), 0) % GLA_CHUNK
    bsum = log_a
    step = 1
    while step < GLA_CHUNK:
        if backward:
            bsum = bsum + jnp.where(pos < GLA_CHUNK - step, pltpu.roll(bsum, tb - step, 0), 0.0)
        else:
            bsum = bsum + jnp.where(pos >= step, pltpu.roll(bsum, step, 0), 0.0)
        step *= 2

    b3 = bsum.reshape(nc, GLA_CHUNK, kw)
    b_end = b3[:, 0:1, :] if backward else b3[:, GLA_CHUNK - 1:GLA_CHUNK, :]
    q3 = q_ref[0].astype(F32).reshape(nc, GLA_CHUNK, kw)
    k3 = k_ref[0].astype(F32).reshape(nc, GLA_CHUNK, kw)
    q_in = q3 * (GLA_DK ** -0.5) * jnp.exp(b3)
    k_out = (k3 * jnp.exp(-b3)).astype(BF16)
    k_dec = (k3 * jnp.exp(b_end - b3)).astype(BF16)
    decay = jnp.exp(b_end)
    v3 = v_ref[0].reshape(nc, GLA_CHUNK, GLA_HEADS * GLA_DV)

    t_idx = lax.broadcasted_iota(jnp.int32, (nc, GLA_CHUNK, GLA_CHUNK), 1)
    s_idx = lax.broadcasted_iota(jnp.int32, (nc, GLA_CHUNK, GLA_CHUNK), 2)
    tri = (s_idx > t_idx) if backward else (s_idx <= t_idx)
    eye = (lax.broadcasted_iota(jnp.int32, (nc, GLA_DV, GLA_DV), 1)
           == lax.broadcasted_iota(jnp.int32, (nc, GLA_DV, GLA_DV), 2)).astype(BF16)
    lane = lax.broadcasted_iota(jnp.int32, (nc, GLA_CHUNK, LANES), 2)
    order = range(nc - 1, -1, -1) if backward else range(nc)

    outs = []
    for h in range(GLA_HEADS):
        pair = slice((h // 2) * LANES, (h // 2 + 1) * LANES)
        own = (lane >= GLA_DK) if h % 2 else (lane < GLA_DK)
        qm = jnp.where(own, q_in[:, :, pair], 0.0).astype(BF16)
        vh = v3[:, :, h * GLA_DV:(h + 1) * GLA_DV]
        att = jnp.einsum('ctd,csd->cts', qm, k_out[:, :, pair], preferred_element_type=F32)
        att = jnp.where(tri, att, 0.0).astype(BF16)
        o_h = jnp.einsum('cts,csv->ctv', att, vh, preferred_element_type=F32)
        vt = jnp.einsum('cvw,csw->cvs', eye, vh, preferred_element_type=F32).astype(BF16)
        ut = jnp.einsum('cvs,csd->cvd', vt, k_dec[:, :, pair], preferred_element_type=F32)
        s_h = s_scr[h]
        before = [None] * nc
        for c in order:
            before[c] = s_h
            s_h = s_h * decay[c, :, pair] + ut[c]
        s_scr[h] = s_h
        s_all = jnp.stack(before, axis=0).astype(BF16)
        o_h = o_h + jnp.einsum('ctd,cvd->ctv', qm, s_all, preferred_element_type=F32)
        outs.append(o_h.reshape(tb, GLA_DV))

    @pl.when(j == pl.num_programs(1) - 1)
    def _():
        sout_ref[0] = s_scr[...]

    if not backward:
        o_ref[0] = jnp.concatenate(outs, axis=-1)
        return
    of = of_ref[0]
    g = g_ref[0].astype(F32)
    fin = []
    for h in range(GLA_HEADS):
        sl = slice(h * GLA_DV, (h + 1) * GLA_DV)
        y = _rms(of[:, sl] + outs[h]) * gn_ref[...]
        gh = g[:, sl]
        fin.append(y * (gh * jax.nn.sigmoid(gh)))
    o_ref[0] = jnp.concatenate(fin, axis=-1).astype(BF16)


def _gla_sweep(p, wg, bg, s0, nc, col0, a_col, backward, o_fwd=None, g_norm=None):
    b, n, _ = p.shape
    tb = nc * GLA_CHUNK
    nblk = n // tb
    kw = GLA_HEADS * GLA_DK
    vw = GLA_HEADS * GLA_DV
    cq, ck, cv, cg, ca = col0 // kw, col0 // kw + 1, (col0 + 2 * kw) // vw, (col0 + 2 * kw) // vw + 1, a_col // LANES

    def blk(j):
        return nblk - 1 - j if backward else j

    in_specs = [
        pl.BlockSpec((1, tb, kw), lambda bi, j: (bi, blk(j), cq)),
        pl.BlockSpec((1, tb, kw), lambda bi, j: (bi, blk(j), ck)),
        pl.BlockSpec((1, tb, vw), lambda bi, j: (bi, blk(j), cv)),
        pl.BlockSpec((1, tb, LANES), lambda bi, j: (bi, blk(j), ca)),
        pl.BlockSpec((LANES, kw), lambda bi, j: (0, 0)),
        pl.BlockSpec((1, kw), lambda bi, j: (0, 0)),
        pl.BlockSpec((1, GLA_HEADS, GLA_DV, LANES), lambda bi, j: (bi, 0, 0, 0)),
    ]
    args = [p, p, p, p, wg, bg, s0]
    if backward:
        in_specs += [
            pl.BlockSpec((1, tb, vw), lambda bi, j: (bi, blk(j), 0)),
            pl.BlockSpec((1, tb, vw), lambda bi, j: (bi, blk(j), cg)),
            pl.BlockSpec((1, GLA_DV), lambda bi, j: (0, 0)),
        ]
        args += [o_fwd, p, g_norm]
    return pl.pallas_call(
        functools.partial(_gla_kernel, backward=backward, nc=nc),
        out_shape=(jax.ShapeDtypeStruct((b, n, vw), BF16 if backward else F32),
                   jax.ShapeDtypeStruct((b, GLA_HEADS, GLA_DV, LANES), F32)),
        grid=(b, nblk),
        in_specs=in_specs,
        out_specs=(pl.BlockSpec((1, tb, vw), lambda bi, j: (bi, blk(j), 0)),
                   pl.BlockSpec((1, GLA_HEADS, GLA_DV, LANES), lambda bi, j: (bi, 0, 0, 0))),
        scratch_shapes=[pltpu.VMEM((GLA_HEADS, GLA_DV, LANES), F32)],
        compiler_params=_params("parallel", "arbitrary"),
        name="gla_bwd" if backward else "gla_fwd",
    )(*args)


def _diff_lambda(lam_ref, lam_init):
    lp = lam_ref[...]
    return (jnp.exp(jnp.sum(lp[0:1] * lp[1:2], axis=-1, keepdims=True))
            - jnp.exp(jnp.sum(lp[2:3] * lp[3:4], axis=-1, keepdims=True)) + lam_init)


def _branch_queries(q):
    lane = lax.broadcasted_iota(jnp.int32, q.shape, 1)
    return jnp.concatenate([jnp.where(lane < DIFF_DK, q, 0.0), jnp.where(lane >= DIFF_DK, q, 0.0)], axis=0)


def _diff_ctx_kernel(q_ref, k_ref, v_ref, lam_ref, gn_ref, o_ref, *, lam_init):
    tq = q_ref.shape[1]
    s = _nt_dot(_branch_queries(q_ref[0]), k_ref[0])
    p = jnp.exp2(s - jnp.max(s, axis=-1, keepdims=True))
    a = jnp.dot(p.astype(BF16), v_ref[0], preferred_element_type=F32) / jnp.sum(p, axis=-1, keepdims=True)
    o = a[0:tq] - _diff_lambda(lam_ref, lam_init) * a[tq:2 * tq]
    o_ref[0] = (_rms(o) * gn_ref[...] * (1.0 - lam_init)).astype(BF16)


def _diff_attention_ctx(qk_c, p_c, lam_params, g_norm, lam_init, v_col0):
    b, n, w = qk_c.shape
    heads = w // (2 * LANES)
    vb = v_col0 // LANES
    return pl.pallas_call(
        functools.partial(_diff_ctx_kernel, lam_init=lam_init),
        out_shape=jax.ShapeDtypeStruct((b, n, heads * DIFF_DV), BF16),
        grid=(b, heads),
        in_specs=[
            pl.BlockSpec((1, n, LANES), lambda bi, h: (bi, 0, h)),
            pl.BlockSpec((1, n, LANES), lambda bi, h: (bi, 0, heads + h)),
            pl.BlockSpec((1, n, LANES), lambda bi, h: (bi, 0, vb + h)),
            pl.BlockSpec((4, DIFF_DK), lambda bi, h: (0, 0)),
            pl.BlockSpec((1, DIFF_DV), lambda bi, h: (0, 0)),
        ],
        out_specs=pl.BlockSpec((1, n, LANES), lambda bi, h: (bi, 0, h)),
        compiler_params=_params("parallel", "parallel"),
        name="diff_attention_ctx",
    )(qk_c, qk_c, p_c, lam_params, g_norm)


def _diff_kernel(q_ref, k_ref, vt_ref, kc_ref, vct_ref, lam_ref, gn_ref, o_ref,
                 qs, s_buf, c_buf, p_buf, a_buf, m_scr, acc, *, nch, lam_init):
    tq = q_ref.shape[1]
    tkc = k_ref.shape[1] // nch

    def stage(value_chunk, softmax_on, score_chunk):
        for n in range(2 * tq // MXU_N):
            sl = slice(n * MXU_N, (n + 1) * MXU_N)
            if value_chunk is not None:
                pv = jnp.dot(vt_ref[0, 0, value_chunk], p_buf[:, sl], preferred_element_type=F32)
                acc[:, sl] = acc[:, sl] * a_buf[:, sl] + pv
            if softmax_on:
                m_old = m_scr[:, sl]
                m_new = jnp.maximum(m_old, c_buf[:, sl])
                a_buf[:, sl] = jnp.exp2(m_old - m_new)
                m_scr[:, sl] = m_new
                for r in range(tkc // SOFTMAX_ROWS):
                    rows = slice(r * SOFTMAX_ROWS, (r + 1) * SOFTMAX_ROWS)
                    p_buf[rows, sl] = jnp.exp2(s_buf[rows, sl] - m_new).astype(BF16)
            if score_chunk is not None:
                k = k_ref[0, pl.ds(pl.multiple_of(score_chunk * tkc, tkc), tkc), :]
                s = _nt_dot(k, qs[sl, :])
                s_buf[:, sl] = s
                c_buf[:, sl] = jnp.max(s, axis=0, keepdims=True)

    qs[...] = _branch_queries(q_ref[0])
    sc = _nt_dot(kc_ref[0], qs[...])
    mc = jnp.max(sc, axis=0, keepdims=True)
    pc = jnp.exp2(sc - mc)
    m_scr[...] = mc
    acc[...] = jnp.dot(vct_ref[0, 0], pc.astype(BF16), preferred_element_type=F32)
    stage(None, False, 0)
    stage(None, True, 1)

    def steady(c, carry):
        stage(c - 1, True, c + 1)
        return carry

    lax.fori_loop(1, nch - 1, steady, 0)
    stage(nch - 2, True, None)
    stage(nch - 1, False, None)

    a = acc[0:DIFF_DV, :] / acc[DIFF_DV:DIFF_DV + 1, :]
    o = a[:, 0:tq] - _diff_lambda(lam_ref, lam_init) * a[:, tq:2 * tq]
    y = o * lax.rsqrt(jnp.mean(o * o, axis=0, keepdims=True) + EPS) * gn_ref[...] * (1.0 - lam_init)
    o_ref[0] = y.T.astype(BF16)


def _diff_attention(qk_x, vt_x, qk_c, vt_c, lam_params, g_norm, lam_init, tq):
    b, t, w = qk_x.shape
    heads = w // (2 * LANES)
    n_ctx = qk_c.shape[1]
    nch, tkc = vt_x.shape[2], vt_x.shape[4]
    assert nch >= 2 and nch * tkc == t
    return pl.pallas_call(
        functools.partial(_diff_kernel, nch=nch, lam_init=lam_init),
        out_shape=jax.ShapeDtypeStruct((b, t, heads * DIFF_DV), BF16),
        grid=(b, heads, t // tq),
        in_specs=[
            pl.BlockSpec((1, tq, LANES), lambda bi, h, i: (bi, i, h)),
            pl.BlockSpec((1, t, LANES), lambda bi, h, i: (bi, 0, heads + h)),
            pl.BlockSpec((1, 1, nch, DIFF_VT_ROWS, tkc), lambda bi, h, i: (bi, h, 0, 0, 0)),
            pl.BlockSpec((1, n_ctx, LANES), lambda bi, h, i: (bi, 0, heads + h)),
            pl.BlockSpec((1, 1, DIFF_VT_ROWS, n_ctx), lambda bi, h, i: (bi, h, 0, 0)),
            pl.BlockSpec((4, DIFF_DK), lambda bi, h, i: (0, 0)),
            pl.BlockSpec((DIFF_DV, 1), lambda bi, h, i: (0, 0)),
        ],
        out_specs=pl.BlockSpec((1, tq, LANES), lambda bi, h, i: (bi, i, h)),
        scratch_shapes=[
            pltpu.VMEM((2 * tq, LANES), BF16),
            pltpu.VMEM((tkc, 2 * tq), F32),
            pltpu.VMEM((1, 2 * tq), F32),
            pltpu.VMEM((tkc, 2 * tq), BF16),
            pltpu.VMEM((1, 2 * tq), F32),
            pltpu.VMEM((1, 2 * tq), F32),
            pltpu.VMEM((DIFF_VT_ROWS, 2 * tq), F32),
        ],
        compiler_params=_params("parallel", "parallel", "arbitrary"),
        name="diff_attention",
    )(qk_x, qk_x, vt_x, qk_c, vt_c, lam_params, g_norm.reshape(DIFF_DV, 1))


def _outproj_kernel(ya_ref, yb_ref, yd_ref, w_ref, x_ref, g_ref, gate_ref, gmlp_ref, shift_ref, scale_ref,
                    o_ref, h_ref):
    wa, wb = ya_ref.shape[2], yb_ref.shape[2]
    for r in range(x_ref.shape[1] // OUTPROJ_ROW_GROUP):
        rows = slice(r * OUTPROJ_ROW_GROUP, (r + 1) * OUTPROJ_ROW_GROUP)
        y = jnp.dot(ya_ref[0, rows, :], w_ref[0:wa, :], preferred_element_type=F32)
        y = y + jnp.dot(yb_ref[0, rows, :], w_ref[wa:wa + wb, :], preferred_element_type=F32)
        y = y + jnp.dot(yd_ref[0, rows, :], w_ref[wa + wb:, :], preferred_element_type=F32)
        x_new = x_ref[0, rows, :] + gate_ref[0] * (_rms(y) * g_ref[...])
        o_ref[0, rows, :] = x_new
        h_ref[0, rows, :] = _modulate(x_new, gmlp_ref, shift_ref, scale_ref)


def _outproj(ya, yb, yd, w, x, g_post, g_mlp, mods, mod_row0, tm):
    b, t, d = x.shape

    def mod(comp):
        return pl.BlockSpec((1, 1, d), lambda bi, i: ((bi + mod_row0) * 6 + comp, 0, 0))

    row = pl.BlockSpec((1, d), lambda bi, i: (0, 0))
    tile = pl.BlockSpec((1, tm, d), lambda bi, i: (bi, i, 0))
    return pl.pallas_call(
        _outproj_kernel,
        out_shape=(jax.ShapeDtypeStruct((b, t, d), F32), jax.ShapeDtypeStruct((b, t, d), BF16)),
        grid=(b, t // tm),
        in_specs=[
            pl.BlockSpec((1, tm, ya.shape[2]), lambda bi, i: (bi, i, 0)),
            pl.BlockSpec((1, tm, yb.shape[2]), lambda bi, i: (bi, i, 0)),
            pl.BlockSpec((1, tm, yd.shape[2]), lambda bi, i: (bi, i, 0)),
            pl.BlockSpec(w.shape, lambda bi, i: (0, 0)),
            tile, row, mod(2), row, mod(3), mod(4),
        ],
        out_specs=(tile, tile),
        compiler_params=_params("parallel", "parallel"),
        name="outproj",
    )(ya, yb, yd, w, x, g_post, mods, g_mlp, mods, mods)


def _mlp_kernel(x_ref, h_ref, w1_ref, w2_ref, gpost_ref, gate_ref, *rest, emit_next):
    if emit_next:
        gnext_ref, shift_ref, scale_ref, o_ref, hnext_ref, acc = rest
    else:
        o_ref, acc = rest
    k = pl.program_id(2)

    @pl.when(k == 0)
    def _():
        acc[...] = jnp.zeros(acc.shape, F32)

    u = jnp.maximum(jnp.dot(h_ref[0], w1_ref[...], preferred_element_type=F32), 0.0)
    acc[...] += jnp.dot((u * u).astype(BF16), w2_ref[...], preferred_element_type=F32)

    @pl.when(k == pl.num_programs(2) - 1)
    def _():
        x_new = x_ref[0] + gate_ref[0] * (_rms(acc[...]) * gpost_ref[...])
        o_ref[0] = x_new
        if emit_next:
            hnext_ref[0] = _modulate(x_new, gnext_ref, shift_ref, scale_ref)


def _mlp(x, h, gpost, mods, mod_row0, w1, w2, tm, th, g_next=None, mods_next=None):
    b, t, d = x.shape
    hid = w1.shape[1]
    emit_next = g_next is not None

    def mod(comp):
        return pl.BlockSpec((1, 1, d), lambda bi, i, k: ((bi + mod_row0) * 6 + comp, 0, 0))

    row = pl.BlockSpec((1, d), lambda bi, i, k: (0, 0))
    tile = pl.BlockSpec((1, tm, d), lambda bi, i, k: (bi, i, 0))
    in_specs = [
        tile, tile,
        pl.BlockSpec((d, th), lambda bi, i, k: (0, k)),
        pl.BlockSpec((th, d), lambda bi, i, k: (k, 0)),
        row, mod(5),
    ]
    args = [x, h, w1, w2, gpost, mods]
    out_shape = [jax.ShapeDtypeStruct((b, t, d), F32)]
    out_specs = [tile]
    if emit_next:
        in_specs += [row, mod(0), mod(1)]
        args += [g_next, mods_next, mods_next]
        out_shape.append(jax.ShapeDtypeStruct((b, t, d), BF16))
        out_specs.append(tile)
    out = pl.pallas_call(
        functools.partial(_mlp_kernel, emit_next=emit_next),
        out_shape=tuple(out_shape),
        grid=(b, t // tm, hid // th),
        in_specs=in_specs,
        out_specs=tuple(out_specs),
        scratch_shapes=[pltpu.VMEM((tm, d), F32)],
        compiler_params=_params("parallel", "parallel", "arbitrary"),
        name="mlp",
    )(*args)
    return out if emit_next else (out[0], None)


def _rope_tables(t):
    tok = jnp.arange(t, dtype=jnp.int32)
    row = (tok // GRID_W).astype(F32)
    col = (tok % GRID_W).astype(F32)
    n_freq = DIFF_DK // 4
    inv = ROPE_BASE ** (-jnp.arange(n_freq, dtype=F32) / n_freq)
    ang = jnp.concatenate([row[:, None] * inv, col[:, None] * inv], axis=-1)
    cos, sin = jnp.cos(ang), jnp.sin(ang)
    reps = LANES // DIFF_DK
    return (jnp.tile(jnp.concatenate([cos, cos], axis=-1), (1, reps)),
            jnp.tile(jnp.concatenate([-sin, sin], axis=-1), (1, reps)))


def _with_ones_row(vt):
    pad = jnp.zeros(vt.shape[:-2] + (DIFF_VT_ROWS - DIFF_DV, vt.shape[-1]), vt.dtype).at[..., 0, :].set(1.0)
    return jnp.concatenate([vt, pad], axis=-2)


def _pick(n, pref):
    return pref if n % pref == 0 else n


def kernel(x, c, ctx, c_ctx, w_ada, b_ada, g_pre_mix, g_post_mix, g_pre_mlp, g_post_mlp, w_in, w_out,
           na_rel_bias, gla_w_gate, gla_b_gate, gla_g_norm, diff_lambda, diff_g_norm, w_mlp_in, w_mlp_out):
    bsz, t, d = x.shape
    n_ctx = ctx.shape[1]
    depth = w_ada.shape[0]
    na_heads = na_rel_bias.shape[1]
    na_w = na_heads * HEAD_DIM
    gla_kw = GLA_HEADS * GLA_DK
    gla_vw = GLA_HEADS * GLA_DV
    diff_w = w_out.shape[1] - na_w - gla_vw
    diff_heads = diff_w // DIFF_DV
    diff_qk = diff_heads * 2 * DIFF_DK
    rows = t // GRID_W
    assert t % (NA_ROWS_PER_STEP * GRID_W) == 0 and rows >= NA_KEY_ROWS
    assert bsz + 1 <= 8

    o_gla = 3 * na_w
    o_gate = o_gla + 2 * gla_kw + 2 * gla_vw
    o_diff = o_gate + 2 * GLA_GATE_RANK
    gla_col0 = 3 * na_w
    diffv_col0 = gla_col0 + 2 * gla_kw + 2 * gla_vw
    gate_col0 = diffv_col0 + diff_w

    cc = jnp.zeros((8, d), F32).at[:bsz].set(c).at[bsz].set(c_ctx)
    mods_all = _mods(cc, w_ada, b_ada)
    rope_tabs = _rope_tables(t)
    xc = ctx.reshape(1, bsz * n_ctx, d)
    zeros_state = jnp.zeros((bsz, GLA_HEADS, GLA_DV, LANES), F32)

    tm_in = _pick(t, 1024)
    tm_out = _pick(t, 512)
    tm_ctx = _pick(bsz * n_ctx, 512)
    tq = _pick(t, 512)
    tkc = min(2048, t // 2)
    nc_lat = _pick(t // GLA_CHUNK, 8)
    nc_ctx = _pick(n_ctx // GLA_CHUNK, 4)

    mods_of = [mods_all[l].reshape(8 * 6, 1, d) for l in range(depth)]
    h_x = _premix(x, g_pre_mix[0].reshape(1, d), mods_of[0], 0, tm_out)
    h_c = _premix(xc, g_pre_mix[0].reshape(1, d), mods_of[0], bsz, tm_ctx)

    for l in range(depth):
        need_ctx = l < depth - 1
        mods = mods_of[l]
        g_next = g_pre_mix[l + 1].reshape(1, d) if need_ctx else None
        mods_next = mods_of[l + 1] if need_ctx else None
        wl = w_in[l]
        w_qk = wl[:, o_diff:o_diff + 2 * diff_qk].astype(BF16)
        w_rest = jnp.concatenate(
            [wl[:, :o_gate], wl[:, o_diff + 2 * diff_qk:], wl[:, o_gate:o_diff],
             jnp.zeros((d, LANES - 2 * GLA_GATE_RANK), F32)], axis=1).astype(BF16)
        w_o = w_out[l].astype(BF16)
        w1 = w_mlp_in[l].astype(BF16)
        w2 = w_mlp_out[l].astype(BF16)
        g2 = g_post_mix[l].reshape(1, d)
        g3 = g_pre_mlp[l].reshape(1, d)
        g4 = g_post_mlp[l].reshape(1, d)
        wg = jnp.zeros((2, LANES, gla_kw), F32)
        wg = wg.at[0, :GLA_GATE_RANK].set(gla_w_gate[l, 0]).at[1, GLA_GATE_RANK:2 * GLA_GATE_RANK].set(gla_w_gate[l, 1])
        wg = wg.astype(BF16)
        bg = gla_b_gate[l].reshape(2, 1, gla_kw)
        gla_gn = gla_g_norm[l].reshape(1, GLA_DV)
        diff_gn = diff_g_norm[l].reshape(1, DIFF_DV)
        lam_init = 0.8 - 0.6 * math.exp(-0.3 * l)

        qk_x = _inproj(h_x, w_qk, tm_in, diff_qk, rope_tabs, 1, DIFF_Q_SCALE)
        p_x = _inproj(h_x, w_rest, tm_in, w_rest.shape[1] // 3)
        qk_c = _inproj(h_c, w_qk, tm_ctx, diff_qk, None, 1, DIFF_Q_SCALE).reshape(bsz, n_ctx, -1)
        p_c = _inproj(h_c, w_rest, tm_ctx, w_rest.shape[1] // 3).reshape(bsz, n_ctx, -1)

        ya = _na_attention(p_x, p_c, _na_bias_table(na_rel_bias[l], rows))

        of_c, s_f = _gla_sweep(p_c, wg[0], bg[0], zeros_state, nc_ctx, gla_col0, gate_col0, False)
        yb_c, s_b = _gla_sweep(p_c, wg[1], bg[1], zeros_state, nc_ctx, gla_col0, gate_col0, True, of_c, gla_gn)
        of_x, _ = _gla_sweep(p_x, wg[0], bg[0], s_f, nc_lat, gla_col0, gate_col0, False)
        yb, _ = _gla_sweep(p_x, wg[1], bg[1], s_b, nc_lat, gla_col0, gate_col0, True, of_x, gla_gn)

        vt_x = p_x[:, :, diffv_col0:diffv_col0 + diff_w].reshape(bsz, t // tkc, tkc, diff_heads, DIFF_DV)
        vt_x = _with_ones_row(vt_x.transpose(0, 3, 1, 4, 2))
        vt_c = p_c[:, :, diffv_col0:diffv_col0 + diff_w].reshape(bsz, n_ctx, diff_heads, DIFF_DV)
        vt_c = _with_ones_row(vt_c.transpose(0, 2, 3, 1))
        yd = _diff_attention(qk_x, vt_x, qk_c, vt_c, diff_lambda[l], diff_gn, lam_init, tq)

        x, hm_x = _outproj(ya, yb, yd, w_o, x, g2, g3, mods, 0, tm_out)
        x, h_x = _mlp(x, hm_x, g4, mods, 0, w1, w2, tm_out, MLP_HIDDEN_TILE, g_next, mods_next)

        if need_ctx:
            ya_c = _ctx_attention(p_c, na_heads // 2)
            yd_c = _diff_attention_ctx(qk_c, p_c, diff_lambda[l], diff_gn, lam_init, diffv_col0)
            flat = lambda z: z.reshape(1, bsz * n_ctx, -1)
            xc, hm_c = _outproj(flat(ya_c), flat(yb_c), flat(yd_c), w_o, xc, g2, g3, mods, bsz, tm_ctx)
            xc, h_c = _mlp(xc, hm_c, g4, mods, bsz, w1, w2, tm_ctx, MLP_HIDDEN_TILE, g_next, mods_next)
    return x
```

```python
import functools
import math

import jax
import jax.numpy as jnp
from jax import lax
from jax.experimental import pallas as pl
from jax.experimental.pallas import tpu as pltpu

F32 = jnp.float32
BF16 = jnp.bfloat16

GRID_W = 64
EPS = 1e-6
NEG_INF = -1e30
HEAD_DIM = 64
NA_WIN_R = 8
NA_WIN_C = 16
GLA_HEADS = 4
GLA_DK = 64
GLA_DV = 128
GLA_GATE_RANK = 16
GLA_NORMALIZER = 16.0
GLA_CHUNK = 64
DIFF_DK = 64
DIFF_DV = 128
ROPE_BASE = 10000.0
DIFF_Q_SCALE = DIFF_DK ** -0.5 * math.log2(math.e)

LANES = 128
MXU_N = 256
DIFF_VT_ROWS = 128 + 16
VMEM_LIMIT = 56 * 1024 * 1024

MLP_HIDDEN_TILE = 1024
ROW_GROUP = 256
NA_ROWS_PER_STEP = 8
NA_KEY_ROWS = NA_ROWS_PER_STEP + NA_WIN_R


def _params(*sem):
    return pltpu.CompilerParams(dimension_semantics=sem, vmem_limit_bytes=VMEM_LIMIT)


def _nt_dot(a, b):
    return lax.dot_general(a, b, (((1,), (1,)), ((), ())), preferred_element_type=F32)


def _rms(y):
    return y * lax.rsqrt(jnp.mean(y * y, axis=-1, keepdims=True) + EPS)


def _mods_kernel(c_ref, w_ref, b_ref, o_ref):
    c = c_ref[...]
    s = (c * jax.nn.sigmoid(c)).astype(BF16)
    o_ref[0] = jnp.dot(s, w_ref[0].astype(BF16), preferred_element_type=F32) + b_ref[0]


def _mods(cc, w_ada, b_ada):
    depth, d, n = w_ada.shape
    tn = 1024
    return pl.pallas_call(
        _mods_kernel,
        out_shape=jax.ShapeDtypeStruct((depth, 8, n), F32),
        grid=(depth, n // tn),
        in_specs=[
            pl.BlockSpec((8, d), lambda l, j: (0, 0)),
            pl.BlockSpec((1, d, tn), lambda l, j: (l, 0, j)),
            pl.BlockSpec((1, 1, tn), lambda l, j: (l, 0, j)),
        ],
        out_specs=pl.BlockSpec((1, 8, tn), lambda l, j: (l, 0, j)),
        compiler_params=_params("arbitrary", "arbitrary"),
        name="ada_mods",
    )(cc, w_ada, b_ada.reshape(depth, 1, n))


def _modulate(x, g_ref, shift_ref, scale_ref):
    return (_rms(x) * g_ref[...] * (1.0 + scale_ref[0]) + shift_ref[0]).astype(BF16)


def _premix_kernel(x_ref, g_ref, shift_ref, scale_ref, o_ref):
    o_ref[0] = _modulate(x_ref[0], g_ref, shift_ref, scale_ref)


def _premix(x, g, mods, mod_row0, tm):
    b, t, d = x.shape
    return pl.pallas_call(
        _premix_kernel,
        out_shape=jax.ShapeDtypeStruct((b, t, d), BF16),
        grid=(b, t // tm),
        in_specs=[
            pl.BlockSpec((1, tm, d), lambda bi, i: (bi, i, 0)),
            pl.BlockSpec((1, d), lambda bi, i: (0, 0)),
            pl.BlockSpec((1, 1, d), lambda bi, i: ((bi + mod_row0) * 6 + 0, 0, 0)),
            pl.BlockSpec((1, 1, d), lambda bi, i: ((bi + mod_row0) * 6 + 1, 0, 0)),
        ],
        out_specs=pl.BlockSpec((1, tm, d), lambda bi, i: (bi, i, 0)),
        compiler_params=_params("parallel", "parallel"),
        name="premix",
    )(x, g, mods, mods)


def _inproj_kernel(h_ref, w_ref, *rest, rope, q_tiles, q_scale):
    if rope:
        cos_ref, sin_ref, o_ref = rest
    else:
        (o_ref,) = rest
    for r in range(h_ref.shape[1] // ROW_GROUP):
        rows = slice(r * ROW_GROUP, (r + 1) * ROW_GROUP)
        acc = jnp.dot(h_ref[0, rows, :], w_ref[...], preferred_element_type=F32)
        if q_tiles:
            acc = acc * jnp.where(pl.program_id(2) < q_tiles, q_scale, 1.0)
        if not rope:
            o_ref[0, rows, :] = acc.astype(BF16)
            continue
        cos = cos_ref[rows, :]
        sin = sin_ref[rows, :]
        lower = (lax.broadcasted_iota(jnp.int32, cos.shape, 1) % DIFF_DK) < DIFF_DK // 2
        for c in range(acc.shape[1] // LANES):
            a = acc[:, c * LANES:(c + 1) * LANES]
            partner = jnp.where(lower, pltpu.roll(a, LANES - DIFF_DK // 2, 1), pltpu.roll(a, DIFF_DK // 2, 1))
            o_ref[0, rows, c * LANES:(c + 1) * LANES] = (a * cos + partner * sin).astype(BF16)


def _inproj(h, w, tm, tn, rope_tabs=None, q_tiles=0, q_scale=1.0):
    b, t, d = h.shape
    n = w.shape[1]
    rope = rope_tabs is not None
    in_specs = [
        pl.BlockSpec((1, tm, d), lambda bi, i, j: (bi, i, 0)),
        pl.BlockSpec((d, tn), lambda bi, i, j: (0, j)),
    ]
    args = [h, w]
    if rope:
        in_specs += [pl.BlockSpec((tm, LANES), lambda bi, i, j: (i, 0))] * 2
        args += list(rope_tabs)
    return pl.pallas_call(
        functools.partial(_inproj_kernel, rope=rope, q_tiles=q_tiles, q_scale=q_scale),
        out_shape=jax.ShapeDtypeStruct((b, t, n), BF16),
        grid=(b, t // tm, n // tn),
        in_specs=in_specs,
        out_specs=pl.BlockSpec((1, tm, tn), lambda bi, i, j: (bi, i, j)),
        compiler_params=_params("parallel", "parallel", "parallel"),
        name="inproj_rope" if rope else "inproj",
    )(*args)


def _na_bias_table(rel_bias, rows):
    rq, kwin = NA_ROWS_PER_STEP, NA_KEY_ROWS
    nb = rows // rq
    h = rel_bias.shape[0]
    blocks = jnp.array([0, min(1, nb - 1), nb - 1], jnp.int32)
    r = blocks[:, None] * rq + jnp.arange(rq, dtype=jnp.int32)[None, :]
    kst = jnp.clip(blocks * rq - NA_WIN_R // 2, 0, rows - kwin)
    kr = kst[:, None] + jnp.arange(kwin, dtype=jnp.int32)[None, :]
    s_r = jnp.clip(r - NA_WIN_R // 2, 0, rows - NA_WIN_R)
    row_ok = (kr[:, None, :] >= s_r[:, :, None]) & (kr[:, None, :] < s_r[:, :, None] + NA_WIN_R)
    dr = jnp.clip(kr[:, None, :] - r[:, :, None] + (NA_WIN_R - 1), 0, 2 * NA_WIN_R - 2)
    col = jnp.arange(GRID_W, dtype=jnp.int32)
    c0 = jnp.clip(col - NA_WIN_C // 2, 0, GRID_W - NA_WIN_C)
    col_ok = (col[None, :] >= c0[:, None]) & (col[None, :] < c0[:, None] + NA_WIN_C)
    dc = jnp.clip(col[None, :] - col[:, None] + (NA_WIN_C - 1), 0, 2 * NA_WIN_C - 2)
    pick_dc = (dc[:, :, None] == jnp.arange(2 * NA_WIN_C - 1)[None, None, :]).astype(F32)
    pick_dr = (dr[..., None] == jnp.arange(2 * NA_WIN_R - 1)[None, None, None, :]).astype(F32)
    by_col = jnp.einsum('hrd,qkd->hrqk', rel_bias.astype(F32), pick_dc, precision=lax.Precision.HIGHEST)
    bias = jnp.einsum('vqlr,hrab->vhqalb', pick_dr, by_col, precision=lax.Precision.HIGHEST)
    ok = row_ok[:, None, :, None, :, None] & col_ok[None, None, None, :, None, :]
    bias = jnp.where(ok, bias, NEG_INF)
    return bias.reshape(3, h, rq * GRID_W, kwin * GRID_W)


def _na_kernel(q_ref, qn_ref, k_ref, v_ref, kc_ref, vc_ref, bias_ref, o_ref, s_loc, s_ctx, *, rows):
    i = pl.program_id(2)
    window = NA_KEY_ROWS * GRID_W
    kc = kc_ref[0]
    vc = vc_ref[0]
    lane = lax.broadcasted_iota(jnp.int32, q_ref.shape[1:], 1)

    def window_start(step):
        kstart = jnp.clip(step * NA_ROWS_PER_STEP - NA_WIN_R // 2, 0, rows - NA_KEY_ROWS)
        return pl.multiple_of(kstart * GRID_W, GRID_W)

    def raw_scores(q, step, hh):
        own = (lane >= hh * HEAD_DIM) & (lane < (hh + 1) * HEAD_DIM)
        qh = jnp.where(own, q * (HEAD_DIM ** -0.5), 0.0).astype(BF16)
        s_loc[hh] = _nt_dot(qh, k_ref[0, pl.ds(window_start(step), window), :])
        s_ctx[hh] = _nt_dot(qh, kc)

    @pl.when(i == 0)
    def _():
        for hh in range(LANES // HEAD_DIM):
            raw_scores(q_ref[0], 0, hh)

    following = jnp.minimum(i + 1, pl.num_programs(2) - 1)
    vw = v_ref[0, pl.ds(window_start(i), window), :]
    outs = []
    for hh in range(LANES // HEAD_DIM):
        sl = s_loc[hh] + bias_ref[0, hh]
        sc = s_ctx[hh]
        m = jnp.maximum(jnp.max(sl, axis=-1, keepdims=True), jnp.max(sc, axis=-1, keepdims=True))
        p_loc = jnp.exp(sl - m)
        p_ctx = jnp.exp(sc - m)
        l = jnp.sum(p_loc, axis=-1, keepdims=True) + jnp.sum(p_ctx, axis=-1, keepdims=True)
        raw_scores(qn_ref[0], following, hh)
        o = jnp.dot(p_loc.astype(BF16), vw, preferred_element_type=F32)
        o = o + jnp.dot(p_ctx.astype(BF16), vc, preferred_element_type=F32)
        outs.append(o / l)
    o_ref[0] = jnp.where(lane < HEAD_DIM, outs[0], outs[1]).astype(BF16)


def _na_attention(p_lat, p_ctx, bias_tbl):
    b, t, _ = p_lat.shape
    n = p_ctx.shape[1]
    rows = t // GRID_W
    nb = rows // NA_ROWS_PER_STEP
    hp = bias_tbl.shape[1] // 2
    tq = NA_ROWS_PER_STEP * GRID_W

    def variant(i):
        return jnp.where(i == 0, 0, jnp.where(i == nb - 1, 2, 1))

    return pl.pallas_call(
        functools.partial(_na_kernel, rows=rows),
        out_shape=jax.ShapeDtypeStruct((b, t, hp * LANES), BF16),
        grid=(b, hp, nb),
        in_specs=[
            pl.BlockSpec((1, tq, LANES), lambda bi, h, i: (bi, i, h)),
            pl.BlockSpec((1, tq, LANES), lambda bi, h, i: (bi, jnp.minimum(i + 1, nb - 1), h)),
            pl.BlockSpec((1, t, LANES), lambda bi, h, i: (bi, 0, hp + h)),
            pl.BlockSpec((1, t, LANES), lambda bi, h, i: (bi, 0, 2 * hp + h)),
            pl.BlockSpec((1, n, LANES), lambda bi, h, i: (bi, 0, hp + h)),
            pl.BlockSpec((1, n, LANES), lambda bi, h, i: (bi, 0, 2 * hp + h)),
            pl.BlockSpec((1, 2, tq, NA_KEY_ROWS * GRID_W), lambda bi, h, i: (variant(i), h, 0, 0)),
        ],
        out_specs=pl.BlockSpec((1, tq, LANES), lambda bi, h, i: (bi, i, h)),
        scratch_shapes=[
            pltpu.VMEM((LANES // HEAD_DIM, tq, NA_KEY_ROWS * GRID_W), F32),
            pltpu.VMEM((LANES // HEAD_DIM, tq, n), F32),
        ],
        compiler_params=_params("parallel", "parallel", "arbitrary"),
        name="na_attention",
    )(p_lat, p_lat, p_lat, p_lat, p_ctx, p_ctx, bias_tbl)


def _ctx_attn_kernel(q_ref, k_ref, v_ref, o_ref):
    q, k, v = q_ref[0], k_ref[0], v_ref[0]
    scale = HEAD_DIM ** -0.5
    lane = lax.broadcasted_iota(jnp.int32, q.shape, 1)
    outs = []
    for hh in range(LANES // HEAD_DIM):
        own = (lane >= hh * HEAD_DIM) & (lane < (hh + 1) * HEAD_DIM)
        s = _nt_dot(jnp.where(own, q, 0.0).astype(BF16), k) * scale
        p = jnp.exp(s - jnp.max(s, axis=-1, keepdims=True))
        l = jnp.sum(p, axis=-1, keepdims=True)
        outs.append(jnp.dot(p.astype(BF16), v, preferred_element_type=F32) / l)
    o_ref[0] = jnp.where(lane < HEAD_DIM, outs[0], outs[1]).astype(BF16)


def _ctx_attention(p_ctx, hp):
    b, n, _ = p_ctx.shape
    return pl.pallas_call(
        _ctx_attn_kernel,
        out_shape=jax.ShapeDtypeStruct((b, n, hp * LANES), BF16),
        grid=(b, hp),
        in_specs=[
            pl.BlockSpec((1, n, LANES), lambda bi, h: (bi, 0, h)),
            pl.BlockSpec((1, n, LANES), lambda bi, h: (bi, 0, hp + h)),
            pl.BlockSpec((1, n, LANES), lambda bi, h: (bi, 0, 2 * hp + h)),
        ],
        out_specs=pl.BlockSpec((1, n, LANES), lambda bi, h: (bi, 0, h)),
        compiler_params=_params("parallel", "parallel"),
        name="ctx_attention",
    )(p_ctx, p_ctx, p_ctx)


def _gla_kernel(*refs, backward, nc):
    if backward:
        (q_ref, k_ref, v_ref, a_ref, wg_ref, bg_ref, s0_ref, of_ref, g_ref, gn_ref,
         o_ref, sout_ref, s_scr) = refs
    else:
        q_ref, k_ref, v_ref, a_ref, wg_ref, bg_ref, s0_ref, o_ref, sout_ref, s_scr = refs
    j = pl.program_id(1)
    tb = nc * GLA_CHUNK
    kw = GLA_HEADS * GLA_DK

    @pl.when(j == 0)
    def _():
        s_scr[...] = s0_ref[0]

    pre = jnp.dot(a_ref[0], wg_ref[...], preferred_element_type=F32) + bg_ref[...]
    log_a = (jnp.minimum(pre, 0.0) - jnp.log1p(jnp.exp(-jnp.abs(pre)))) * (1.0 / GLA_NORMALIZER)

    pos = lax.broadcasted_iota(jnp.int32, (tb, kw), 0) % GLA_CHUNK
    bsum = log_a
    step = 1
    while step < GLA_CHUNK:
        if backward:
            bsum = bsum + jnp.where(pos < GLA_CHUNK - step, pltpu.roll(bsum, tb - step, 0), 0.0)
        else:
            bsum = bsum + jnp.where(pos >= step, pltpu.roll(bsum, step, 0), 0.0)
        step *= 2

    b3 = bsum.reshape(nc, GLA_CHUNK, kw)
    b_end = b3[:, 0:1, :] if backward else b3[:, GLA_CHUNK - 1:GLA_CHUNK, :]
    q3 = q_ref[0].astype(F32).reshape(nc, GLA_CHUNK, kw)
    k3 = k_ref[0].astype(F32).reshape(nc, GLA_CHUNK, kw)
    q_in = q3 * (GLA_DK ** -0.5) * jnp.exp(b3)
    k_out = (k3 * jnp.exp(-b3)).astype(BF16)
    k_dec = (k3 * jnp.exp(b_end - b3)).astype(BF16)
    decay = jnp.exp(b_end)
    v3 = v_ref[0].reshape(nc, GLA_CHUNK, GLA_HEADS * GLA_DV)

    t_idx = lax.broadcasted_iota(jnp.int32, (nc, GLA_CHUNK, GLA_CHUNK), 1)
    s_idx = lax.broadcasted_iota(jnp.int32, (nc, GLA_CHUNK, GLA_CHUNK), 2)
    tri = (s_idx > t_idx) if backward else (s_idx <= t_idx)
    eye = (lax.broadcasted_iota(jnp.int32, (nc, GLA_DV, GLA_DV), 1)
           == lax.broadcasted_iota(jnp.int32, (nc, GLA_DV, GLA_DV), 2)).astype(BF16)
    lane = lax.broadcasted_iota(jnp.int32, (nc, GLA_CHUNK, LANES), 2)
    order = range(nc - 1, -1, -1) if backward else range(nc)

    outs = []
    for h in range(GLA_HEADS):
        pair = slice((h // 2) * LANES, (h // 2 + 1) * LANES)
        own = (lane >= GLA_DK) if h % 2 else (lane < GLA_DK)
        qm = jnp.where(own, q_in[:, :, pair], 0.0).astype(BF16)
        vh = v3[:, :, h * GLA_DV:(h + 1) * GLA_DV]
        att = jnp.einsum('ctd,csd->cts', qm, k_out[:, :, pair], preferred_element_type=F32)
        att = jnp.where(tri, att, 0.0).astype(BF16)
        o_h = jnp.einsum('cts,csv->ctv', att, vh, preferred_element_type=F32)
        vt = jnp.einsum('cvw,csw->cvs', eye, vh, preferred_element_type=F32).astype(BF16)
        ut = jnp.einsum('cvs,csd->cvd', vt, k_dec[:, :, pair], preferred_element_type=F32)
        s_h = s_scr[h]
        before = [None] * nc
        for c in order:
            before[c] = s_h
            s_h = s_h * decay[c, :, pair] + ut[c]
        s_scr[h] = s_h
        s_all = jnp.stack(before, axis=0).astype(BF16)
        o_h = o_h + jnp.einsum('ctd,cvd->ctv', qm, s_all, preferred_element_type=F32)
        outs.append(o_h.reshape(tb, GLA_DV))

    @pl.when(j == pl.num_programs(1) - 1)
    def _():
        sout_ref[0] = s_scr[...]

    if not backward:
        o_ref[0] = jnp.concatenate(outs, axis=-1)
        return
    of = of_ref[0]
    g = g_ref[0].astype(F32)
    fin = []
    for h in range(GLA_HEADS):
        sl = slice(h * GLA_DV, (h + 1) * GLA_DV)
        y = _rms(of[:, sl] + outs[h]) * gn_ref[...]
        gh = g[:, sl]
        fin.append(y * (gh * jax.nn.sigmoid(gh)))
    o_ref[0] = jnp.concatenate(fin, axis=-1).astype(BF16)


def _gla_sweep(p, wg, bg, s0, nc, col0, a_col, backward, o_fwd=None, g_norm=None):
    b, n, _ = p.shape
    tb = nc * GLA_CHUNK
    nblk = n // tb
    kw = GLA_HEADS * GLA_DK
    vw = GLA_HEADS * GLA_DV
    cq, ck, cv, cg, ca = col0 // kw, col0 // kw + 1, (col0 + 2 * kw) // vw, (col0 + 2 * kw) // vw + 1, a_col // LANES

    def blk(j):
        return nblk - 1 - j if backward else j

    in_specs = [
        pl.BlockSpec((1, tb, kw), lambda bi, j: (bi, blk(j), cq)),
        pl.BlockSpec((1, tb, kw), lambda bi, j: (bi, blk(j), ck)),
        pl.BlockSpec((1, tb, vw), lambda bi, j: (bi, blk(j), cv)),
        pl.BlockSpec((1, tb, LANES), lambda bi, j: (bi, blk(j), ca)),
        pl.BlockSpec((LANES, kw), lambda bi, j: (0, 0)),
        pl.BlockSpec((1, kw), lambda bi, j: (0, 0)),
        pl.BlockSpec((1, GLA_HEADS, GLA_DV, LANES), lambda bi, j: (bi, 0, 0, 0)),
    ]
    args = [p, p, p, p, wg, bg, s0]
    if backward:
        in_specs += [
            pl.BlockSpec((1, tb, vw), lambda bi, j: (bi, blk(j), 0)),
            pl.BlockSpec((1, tb, vw), lambda bi, j: (bi, blk(j), cg)),
            pl.BlockSpec((1, GLA_DV), lambda bi, j: (0, 0)),
        ]
        args += [o_fwd, p, g_norm]
    return pl.pallas_call(
        functools.partial(_gla_kernel, backward=backward, nc=nc),
        out_shape=(jax.ShapeDtypeStruct((b, n, vw), BF16 if backward else F32),
                   jax.ShapeDtypeStruct((b, GLA_HEADS, GLA_DV, LANES), F32)),
        grid=(b, nblk),
        in_specs=in_specs,
        out_specs=(pl.BlockSpec((1, tb, vw), lambda bi, j: (bi, blk(j), 0)),
                   pl.BlockSpec((1, GLA_HEADS, GLA_DV, LANES), lambda bi, j: (bi, 0, 0, 0))),
        scratch_shapes=[pltpu.VMEM((GLA_HEADS, GLA_DV, LANES), F32)],
        compiler_params=_params("parallel", "arbitrary"),
        name="gla_bwd" if backward else "gla_fwd",
    )(*args)


def _diff_lambda(lam_ref, lam_init):
    lp = lam_ref[...]
    return (jnp.exp(jnp.sum(lp[0:1] * lp[1:2], axis=-1, keepdims=True))
            - jnp.exp(jnp.sum(lp[2:3] * lp[3:4], axis=-1, keepdims=True)) + lam_init)


def _branch_queries(q):
    lane = lax.broadcasted_iota(jnp.int32, q.shape, 1)
    return jnp.concatenate([jnp.where(lane < DIFF_DK, q, 0.0), jnp.where(lane >= DIFF_DK, q, 0.0)], axis=0)


def _diff_ctx_kernel(q_ref, k_ref, v_ref, lam_ref, gn_ref, o_ref, *, lam_init):
    tq = q_ref.shape[1]
    s = _nt_dot(_branch_queries(q_ref[0]), k_ref[0])
    p = jnp.exp2(s - jnp.max(s, axis=-1, keepdims=True))
    a = jnp.dot(p.astype(BF16), v_ref[0], preferred_element_type=F32) / jnp.sum(p, axis=-1, keepdims=True)
    o = a[0:tq] - _diff_lambda(lam_ref, lam_init) * a[tq:2 * tq]
    o_ref[0] = (_rms(o) * gn_ref[...] * (1.0 - lam_init)).astype(BF16)


def _diff_attention_ctx(qk_c, p_c, lam_params, g_norm, lam_init, v_col0):
    b, n, w = qk_c.shape
    heads = w // (2 * LANES)
    vb = v_col0 // LANES
    return pl.pallas_call(
        functools.partial(_diff_ctx_kernel, lam_init=lam_init),
        out_shape=jax.ShapeDtypeStruct((b, n, heads * DIFF_DV), BF16),
        grid=(b, heads),
        in_specs=[
            pl.BlockSpec((1, n, LANES), lambda bi, h: (bi, 0, h)),
            pl.BlockSpec((1, n, LANES), lambda bi, h: (bi, 0, heads + h)),
            pl.BlockSpec((1, n, LANES), lambda bi, h: (bi, 0, vb + h)),
            pl.BlockSpec((4, DIFF_DK), lambda bi, h: (0, 0)),
            pl.BlockSpec((1, DIFF_DV), lambda bi, h: (0, 0)),
        ],
        out_specs=pl.BlockSpec((1, n, LANES), lambda bi, h: (bi, 0, h)),
        compiler_params=_params("parallel", "parallel"),
        name="diff_attention_ctx",
    )(qk_c, qk_c, p_c, lam_params, g_norm)


def _diff_kernel(q_ref, k_ref, vt_ref, kc_ref, vct_ref, lam_ref, gn_ref, o_ref,
                 qs, s_buf, c_buf, p_buf, a_buf, m_scr, acc, *, nch, lam_init):
    tq = q_ref.shape[1]
    tkc = k_ref.shape[1] // nch

    def stage(value_chunk, softmax_on, score_chunk):
        for n in range(2 * tq // MXU_N):
            sl = slice(n * MXU_N, (n + 1) * MXU_N)
            if value_chunk is not None:
                pv = jnp.dot(vt_ref[0, 0, value_chunk], p_buf[:, sl], preferred_element_type=F32)
                acc[:, sl] = acc[:, sl] * a_buf[:, sl] + pv
            if softmax_on:
                m_old = m_scr[:, sl]
                m_new = jnp.maximum(m_old, c_buf[:, sl])
                a_buf[:, sl] = jnp.exp2(m_old - m_new)
                m_scr[:, sl] = m_new
                p_buf[:, sl] = jnp.exp2(s_buf[:, sl] - m_new).astype(BF16)
            if score_chunk is not None:
                k = k_ref[0, pl.ds(pl.multiple_of(score_chunk * tkc, tkc), tkc), :]
                s = _nt_dot(k, qs[sl, :])
                s_buf[:, sl] = s
                c_buf[:, sl] = jnp.max(s, axis=0, keepdims=True)

    qs[...] = _branch_queries(q_ref[0])
    sc = _nt_dot(kc_ref[0], qs[...])
    mc = jnp.max(sc, axis=0, keepdims=True)
    pc = jnp.exp2(sc - mc)
    m_scr[...] = mc
    acc[...] = jnp.dot(vct_ref[0, 0], pc.astype(BF16), preferred_element_type=F32)
    stage(None, False, 0)
    stage(None, True, 1)

    def steady(c, carry):
        stage(c - 1, True, c + 1)
        return carry

    lax.fori_loop(1, nch - 1, steady, 0, unroll=2 if nch % 2 == 0 else 1)
    stage(nch - 2, True, None)
    stage(nch - 1, False, None)

    a = acc[0:DIFF_DV, :] / acc[DIFF_DV:DIFF_DV + 1, :]
    o = a[:, 0:tq] - _diff_lambda(lam_ref, lam_init) * a[:, tq:2 * tq]
    y = o * lax.rsqrt(jnp.mean(o * o, axis=0, keepdims=True) + EPS) * gn_ref[...] * (1.0 - lam_init)
    o_ref[0] = y.T.astype(BF16)


def _diff_attention(qk_x, vt_x, qk_c, vt_c, lam_params, g_norm, lam_init, tq):
    b, t, w = qk_x.shape
    heads = w // (2 * LANES)
    n_ctx = qk_c.shape[1]
    nch, tkc = vt_x.shape[2], vt_x.shape[4]
    assert nch >= 2 and nch * tkc == t
    return pl.pallas_call(
        functools.partial(_diff_kernel, nch=nch, lam_init=lam_init),
        out_shape=jax.ShapeDtypeStruct((b, t, heads * DIFF_DV), BF16),
        grid=(b, heads, t // tq),
        in_specs=[
            pl.BlockSpec((1, tq, LANES), lambda bi, h, i: (bi, i, h)),
            pl.BlockSpec((1, t, LANES), lambda bi, h, i: (bi, 0, heads + h)),
            pl.BlockSpec((1, 1, nch, DIFF_VT_ROWS, tkc), lambda bi, h, i: (bi, h, 0, 0, 0)),
            pl.BlockSpec((1, n_ctx, LANES), lambda bi, h, i: (bi, 0, heads + h)),
            pl.BlockSpec((1, 1, DIFF_VT_ROWS, n_ctx), lambda bi, h, i: (bi, h, 0, 0)),
            pl.BlockSpec((4, DIFF_DK), lambda bi, h, i: (0, 0)),
            pl.BlockSpec((DIFF_DV, 1), lambda bi, h, i: (0, 0)),
        ],
        out_specs=pl.BlockSpec((1, tq, LANES), lambda bi, h, i: (bi, i, h)),
        scratch_shapes=[
            pltpu.VMEM((2 * tq, LANES), BF16),
            pltpu.VMEM((tkc, 2 * tq), F32),
            pltpu.VMEM((1, 2 * tq), F32),
            pltpu.VMEM((tkc, 2 * tq), BF16),
            pltpu.VMEM((1, 2 * tq), F32),
            pltpu.VMEM((1, 2 * tq), F32),
            pltpu.VMEM((DIFF_VT_ROWS, 2 * tq), F32),
        ],
        compiler_params=_params("parallel", "parallel", "arbitrary"),
        name="diff_attention",
    )(qk_x, qk_x, vt_x, qk_c, vt_c, lam_params, g_norm.reshape(DIFF_DV, 1))


def _outproj_kernel(ya_ref, yb_ref, yd_ref, w_ref, x_ref, g_ref, gate_ref, gmlp_ref, shift_ref, scale_ref,
                    o_ref, h_ref):
    wa, wb = ya_ref.shape[2], yb_ref.shape[2]
    for r in range(x_ref.shape[1] // ROW_GROUP):
        rows = slice(r * ROW_GROUP, (r + 1) * ROW_GROUP)
        y = jnp.dot(ya_ref[0, rows, :], w_ref[0:wa, :], preferred_element_type=F32)
        y = y + jnp.dot(yb_ref[0, rows, :], w_ref[wa:wa + wb, :], preferred_element_type=F32)
        y = y + jnp.dot(yd_ref[0, rows, :], w_ref[wa + wb:, :], preferred_element_type=F32)
        x_new = x_ref[0, rows, :] + gate_ref[0] * (_rms(y) * g_ref[...])
        o_ref[0, rows, :] = x_new
        h_ref[0, rows, :] = _modulate(x_new, gmlp_ref, shift_ref, scale_ref)


def _outproj(ya, yb, yd, w, x, g_post, g_mlp, mods, mod_row0, tm):
    b, t, d = x.shape

    def mod(comp):
        return pl.BlockSpec((1, 1, d), lambda bi, i: ((bi + mod_row0) * 6 + comp, 0, 0))

    row = pl.BlockSpec((1, d), lambda bi, i: (0, 0))
    tile = pl.BlockSpec((1, tm, d), lambda bi, i: (bi, i, 0))
    return pl.pallas_call(
        _outproj_kernel,
        out_shape=(jax.ShapeDtypeStruct((b, t, d), F32), jax.ShapeDtypeStruct((b, t, d), BF16)),
        grid=(b, t // tm),
        in_specs=[
            pl.BlockSpec((1, tm, ya.shape[2]), lambda bi, i: (bi, i, 0)),
            pl.BlockSpec((1, tm, yb.shape[2]), lambda bi, i: (bi, i, 0)),
            pl.BlockSpec((1, tm, yd.shape[2]), lambda bi, i: (bi, i, 0)),
            pl.BlockSpec(w.shape, lambda bi, i: (0, 0)),
            tile, row, mod(2), row, mod(3), mod(4),
        ],
        out_specs=(tile, tile),
        compiler_params=_params("parallel", "parallel"),
        name="outproj",
    )(ya, yb, yd, w, x, g_post, mods, g_mlp, mods, mods)


def _mlp_kernel(x_ref, h_ref, w1_ref, w2_ref, gpost_ref, gate_ref, *rest, emit_next):
    if emit_next:
        gnext_ref, shift_ref, scale_ref, o_ref, hnext_ref, acc = rest
    else:
        o_ref, acc = rest
    k = pl.program_id(2)

    @pl.when(k == 0)
    def _():
        acc[...] = jnp.zeros(acc.shape, F32)

    def hidden_tile(rows):
        u = jnp.maximum(jnp.dot(h_ref[0, rows, :], w1_ref[...], preferred_element_type=F32), 0.0)
        return jnp.dot((u * u).astype(BF16), w2_ref[...], preferred_element_type=F32)

    last = pl.num_programs(2) - 1

    @pl.when(k < last)
    def _():
        acc[...] += hidden_tile(slice(None))

    @pl.when(k == last)
    def _():
        for r in range(acc.shape[0] // ROW_GROUP):
            rows = slice(r * ROW_GROUP, (r + 1) * ROW_GROUP)
            y = acc[rows, :] + hidden_tile(rows)
            x_new = x_ref[0, rows, :] + gate_ref[0] * (_rms(y) * gpost_ref[...])
            o_ref[0, rows, :] = x_new
            if emit_next:
                hnext_ref[0, rows, :] = _modulate(x_new, gnext_ref, shift_ref, scale_ref)


def _mlp(x, h, gpost, mods, mod_row0, w1, w2, tm, th, g_next=None, mods_next=None):
    b, t, d = x.shape
    hid = w1.shape[1]
    emit_next = g_next is not None

    def mod(comp):
        return pl.BlockSpec((1, 1, d), lambda bi, i, k: ((bi + mod_row0) * 6 + comp, 0, 0))

    row = pl.BlockSpec((1, d), lambda bi, i, k: (0, 0))
    tile = pl.BlockSpec((1, tm, d), lambda bi, i, k: (bi, i, 0))
    in_specs = [
        tile, tile,
        pl.BlockSpec((d, th), lambda bi, i, k: (0, k)),
        pl.BlockSpec((th, d), lambda bi, i, k: (k, 0)),
        row, mod(5),
    ]
    args = [x, h, w1, w2, gpost, mods]
    out_shape = [jax.ShapeDtypeStruct((b, t, d), F32)]
    out_specs = [tile]
    if emit_next:
        in_specs += [row, mod(0), mod(1)]
        args += [g_next, mods_next, mods_next]
        out_shape.append(jax.ShapeDtypeStruct((b, t, d), BF16))
        out_specs.append(tile)
    out = pl.pallas_call(
        functools.partial(_mlp_kernel, emit_next=emit_next),
        out_shape=tuple(out_shape),
        grid=(b, t // tm, hid // th),
        in_specs=in_specs,
        out_specs=tuple(out_specs),
        scratch_shapes=[pltpu.VMEM((tm, d), F32)],
        compiler_params=_params("parallel", "parallel", "arbitrary"),
        name="mlp",
    )(*args)
    return out if emit_next else (out[0], None)


def _rope_tables(t):
    tok = jnp.arange(t, dtype=jnp.int32)
    row = (tok // GRID_W).astype(F32)
    col = (tok % GRID_W).astype(F32)
    n_freq = DIFF_DK // 4
    inv = ROPE_BASE ** (-jnp.arange(n_freq, dtype=F32) / n_freq)
    ang = jnp.concatenate([row[:, None] * inv, col[:, None] * inv], axis=-1)
    cos, sin = jnp.cos(ang), jnp.sin(ang)
    reps = LANES // DIFF_DK
    return (jnp.tile(jnp.concatenate([cos, cos], axis=-1), (1, reps)),
            jnp.tile(jnp.concatenate([-sin, sin], axis=-1), (1, reps)))


def _with_ones_row(vt):
    pad = jnp.zeros(vt.shape[:-2] + (DIFF_VT_ROWS - DIFF_DV, vt.shape[-1]), vt.dtype).at[..., 0, :].set(1.0)
    return jnp.concatenate([vt, pad], axis=-2)


def _pick(n, pref):
    return pref if n % pref == 0 else n


def kernel(x, c, ctx, c_ctx, w_ada, b_ada, g_pre_mix, g_post_mix, g_pre_mlp, g_post_mlp, w_in, w_out,
           na_rel_bias, gla_w_gate, gla_b_gate, gla_g_norm, diff_lambda, diff_g_norm, w_mlp_in, w_mlp_out):
    bsz, t, d = x.shape
    n_ctx = ctx.shape[1]
    depth = w_ada.shape[0]
    na_heads = na_rel_bias.shape[1]
    na_w = na_heads * HEAD_DIM
    gla_kw = GLA_HEADS * GLA_DK
    gla_vw = GLA_HEADS * GLA_DV
    diff_w = w_out.shape[1] - na_w - gla_vw
    diff_heads = diff_w // DIFF_DV
    diff_qk = diff_heads * 2 * DIFF_DK
    rows = t // GRID_W
    assert t % (NA_ROWS_PER_STEP * GRID_W) == 0 and rows >= NA_KEY_ROWS
    assert bsz + 1 <= 8

    o_gla = 3 * na_w
    o_gate = o_gla + 2 * gla_kw + 2 * gla_vw
    o_diff = o_gate + 2 * GLA_GATE_RANK
    gla_col0 = 3 * na_w
    diffv_col0 = gla_col0 + 2 * gla_kw + 2 * gla_vw
    gate_col0 = diffv_col0 + diff_w

    cc = jnp.zeros((8, d), F32).at[:bsz].set(c).at[bsz].set(c_ctx)
    mods_all = _mods(cc, w_ada, b_ada)
    rope_tabs = _rope_tables(t)
    xc = ctx.reshape(1, bsz * n_ctx, d)
    zeros_state = jnp.zeros((bsz, GLA_HEADS, GLA_DV, LANES), F32)

    tm_in = _pick(t, 1024)
    tm_out = _pick(t, 512)
    tm_ctx = _pick(bsz * n_ctx, 512)
    tq = _pick(t, 512)
    tkc = min(2048, t // 2)
    nc_lat = _pick(t // GLA_CHUNK, 8)
    nc_ctx = _pick(n_ctx // GLA_CHUNK, 4)

    mods_of = [mods_all[l].reshape(8 * 6, 1, d) for l in range(depth)]
    h_x = _premix(x, g_pre_mix[0].reshape(1, d), mods_of[0], 0, tm_out)
    h_c = _premix(xc, g_pre_mix[0].reshape(1, d), mods_of[0], bsz, tm_ctx)

    for l in range(depth):
        need_ctx = l < depth - 1
        mods = mods_of[l]
        g_next = g_pre_mix[l + 1].reshape(1, d) if need_ctx else None
        mods_next = mods_of[l + 1] if need_ctx else None
        wl = w_in[l]
        w_qk = wl[:, o_diff:o_diff + 2 * diff_qk].astype(BF16)
        w_rest = jnp.concatenate(
            [wl[:, :o_gate], wl[:, o_diff + 2 * diff_qk:], wl[:, o_gate:o_diff],
             jnp.zeros((d, LANES - 2 * GLA_GATE_RANK), F32)], axis=1).astype(BF16)
        w_o = w_out[l].astype(BF16)
        w1 = w_mlp_in[l].astype(BF16)
        w2 = w_mlp_out[l].astype(BF16)
        g2 = g_post_mix[l].reshape(1, d)
        g3 = g_pre_mlp[l].reshape(1, d)
        g4 = g_post_mlp[l].reshape(1, d)
        wg = jnp.zeros((2, LANES, gla_kw), F32)
        wg = wg.at[0, :GLA_GATE_RANK].set(gla_w_gate[l, 0]).at[1, GLA_GATE_RANK:2 * GLA_GATE_RANK].set(gla_w_gate[l, 1])
        wg = wg.astype(BF16)
        bg = gla_b_gate[l].reshape(2, 1, gla_kw)
        gla_gn = gla_g_norm[l].reshape(1, GLA_DV)
        diff_gn = diff_g_norm[l].reshape(1, DIFF_DV)
        lam_init = 0.8 - 0.6 * math.exp(-0.3 * l)

        qk_x = _inproj(h_x, w_qk, tm_in, diff_qk, rope_tabs, 1, DIFF_Q_SCALE)
        p_x = _inproj(h_x, w_rest, tm_in, w_rest.shape[1] // 3)
        qk_c = _inproj(h_c, w_qk, tm_ctx, diff_qk, None, 1, DIFF_Q_SCALE).reshape(bsz, n_ctx, -1)
        p_c = _inproj(h_c, w_rest, tm_ctx, w_rest.shape[1] // 3).reshape(bsz, n_ctx, -1)

        ya = _na_attention(p_x, p_c, _na_bias_table(na_rel_bias[l], rows))

        of_c, s_f = _gla_sweep(p_c, wg[0], bg[0], zeros_state, nc_ctx, gla_col0, gate_col0, False)
        yb_c, s_b = _gla_sweep(p_c, wg[1], bg[1], zeros_state, nc_ctx, gla_col0, gate_col0, True, of_c, gla_gn)
        of_x, _ = _gla_sweep(p_x, wg[0], bg[0], s_f, nc_lat, gla_col0, gate_col0, False)
        yb, _ = _gla_sweep(p_x, wg[1], bg[1], s_b, nc_lat, gla_col0, gate_col0, True, of_x, gla_gn)

        vt_x = p_x[:, :, diffv_col0:diffv_col0 + diff_w].reshape(bsz, t // tkc, tkc, diff_heads, DIFF_DV)
        vt_x = _with_ones_row(vt_x.transpose(0, 3, 1, 4, 2))
        vt_c = p_c[:, :, diffv_col0:diffv_col0 + diff_w].reshape(bsz, n_ctx, diff_heads, DIFF_DV)
        vt_c = _with_ones_row(vt_c.transpose(0, 2, 3, 1))
        yd = _diff_attention(qk_x, vt_x, qk_c, vt_c, diff_lambda[l], diff_gn, lam_init, tq)

        x, hm_x = _outproj(ya, yb, yd, w_o, x, g2, g3, mods, 0, tm_out)
        x, h_x = _mlp(x, hm_x, g4, mods, 0, w1, w2, tm_out, MLP_HIDDEN_TILE, g_next, mods_next)

        if need_ctx:
            ya_c = _ctx_attention(p_c, na_heads // 2)
            yd_c = _diff_attention_ctx(qk_c, p_c, diff_lambda[l], diff_gn, lam_init, diffv_col0)
            flat = lambda z: z.reshape(1, bsz * n_ctx, -1)
            xc, hm_c = _outproj(flat(ya_c), flat(yb_c), flat(yd_c), w_o, xc, g2, g3, mods, bsz, tm_ctx)
            xc, h_c = _mlp(xc, hm_c, g4, mods, bsz, w1, w2, tm_ctx, MLP_HIDDEN_TILE, g_next, mods_next)
    return x
```

```python
import functools
import math

import jax
import jax.numpy as jnp
from jax import lax
from jax.experimental import pallas as pl
from jax.experimental.pallas import tpu as pltpu

F32 = jnp.float32
BF16 = jnp.bfloat16

GRID_W = 64
EPS = 1e-6
NEG_INF = -1e30
HEAD_DIM = 64
NA_WIN_R = 8
NA_WIN_C = 16
GLA_HEADS = 4
GLA_DK = 64
GLA_DV = 128
GLA_GATE_RANK = 16
GLA_NORMALIZER = 16.0
GLA_CHUNK = 64
DIFF_DK = 64
DIFF_DV = 128
ROPE_BASE = 10000.0
DIFF_Q_SCALE = DIFF_DK ** -0.5 * math.log2(math.e)

LANES = 128
MXU_N = 256
DIFF_VT_ROWS = 128 + 16
VMEM_LIMIT = 56 * 1024 * 1024

MLP_HIDDEN_TILE = 1024
ROW_GROUP = 256
NA_ROWS_PER_STEP = 8
NA_KEY_ROWS = NA_ROWS_PER_STEP + NA_WIN_R


def _params(*sem):
    return pltpu.CompilerParams(dimension_semantics=sem, vmem_limit_bytes=VMEM_LIMIT)


def _nt_dot(a, b):
    return lax.dot_general(a, b, (((1,), (1,)), ((), ())), preferred_element_type=F32)


def _rms(y):
    return y * lax.rsqrt(jnp.mean(y * y, axis=-1, keepdims=True) + EPS)


def _mods_kernel(c_ref, w_ref, b_ref, o_ref):
    c = c_ref[...]
    s = (c * jax.nn.sigmoid(c)).astype(BF16)
    o_ref[0] = jnp.dot(s, w_ref[0].astype(BF16), preferred_element_type=F32) + b_ref[0]


def _mods(cc, w_ada, b_ada):
    depth, d, n = w_ada.shape
    tn = 1024
    return pl.pallas_call(
        _mods_kernel,
        out_shape=jax.ShapeDtypeStruct((depth, 8, n), F32),
        grid=(depth, n // tn),
        in_specs=[
            pl.BlockSpec((8, d), lambda l, j: (0, 0)),
            pl.BlockSpec((1, d, tn), lambda l, j: (l, 0, j)),
            pl.BlockSpec((1, 1, tn), lambda l, j: (l, 0, j)),
        ],
        out_specs=pl.BlockSpec((1, 8, tn), lambda l, j: (l, 0, j)),
        compiler_params=_params("arbitrary", "arbitrary"),
        name="ada_mods",
    )(cc, w_ada, b_ada.reshape(depth, 1, n))


def _modulate(x, g_ref, shift_ref, scale_ref):
    return (_rms(x) * g_ref[...] * (1.0 + scale_ref[0]) + shift_ref[0]).astype(BF16)


def _premix_kernel(x_ref, g_ref, shift_ref, scale_ref, o_ref):
    o_ref[0] = _modulate(x_ref[0], g_ref, shift_ref, scale_ref)


def _premix(x, g, mods, mod_row0, tm):
    b, t, d = x.shape
    return pl.pallas_call(
        _premix_kernel,
        out_shape=jax.ShapeDtypeStruct((b, t, d), BF16),
        grid=(b, t // tm),
        in_specs=[
            pl.BlockSpec((1, tm, d), lambda bi, i: (bi, i, 0)),
            pl.BlockSpec((1, d), lambda bi, i: (0, 0)),
            pl.BlockSpec((1, 1, d), lambda bi, i: ((bi + mod_row0) * 6 + 0, 0, 0)),
            pl.BlockSpec((1, 1, d), lambda bi, i: ((bi + mod_row0) * 6 + 1, 0, 0)),
        ],
        out_specs=pl.BlockSpec((1, tm, d), lambda bi, i: (bi, i, 0)),
        compiler_params=_params("parallel", "parallel"),
        name="premix",
    )(x, g, mods, mods)


def _inproj_kernel(h_ref, w_ref, *rest, rope, q_tiles, q_scale):
    if rope:
        cos_ref, sin_ref, o_ref = rest
    else:
        (o_ref,) = rest
    for r in range(h_ref.shape[1] // ROW_GROUP):
        rows = slice(r * ROW_GROUP, (r + 1) * ROW_GROUP)
        acc = jnp.dot(h_ref[0, rows, :], w_ref[...], preferred_element_type=F32)
        if q_tiles:
            acc = acc * jnp.where(pl.program_id(2) < q_tiles, q_scale, 1.0)
        if not rope:
            o_ref[0, rows, :] = acc.astype(BF16)
            continue
        cos = cos_ref[rows, :]
        sin = sin_ref[rows, :]
        lower = (lax.broadcasted_iota(jnp.int32, cos.shape, 1) % DIFF_DK) < DIFF_DK // 2
        for c in range(acc.shape[1] // LANES):
            a = acc[:, c * LANES:(c + 1) * LANES]
            partner = jnp.where(lower, pltpu.roll(a, LANES - DIFF_DK // 2, 1), pltpu.roll(a, DIFF_DK // 2, 1))
            o_ref[0, rows, c * LANES:(c + 1) * LANES] = (a * cos + partner * sin).astype(BF16)


def _inproj(h, w, tm, tn, rope_tabs=None, q_tiles=0, q_scale=1.0):
    b, t, d = h.shape
    n = w.shape[1]
    rope = rope_tabs is not None
    in_specs = [
        pl.BlockSpec((1, tm, d), lambda bi, i, j: (bi, i, 0)),
        pl.BlockSpec((d, tn), lambda bi, i, j: (0, j)),
    ]
    args = [h, w]
    if rope:
        in_specs += [pl.BlockSpec((tm, LANES), lambda bi, i, j: (i, 0))] * 2
        args += list(rope_tabs)
    return pl.pallas_call(
        functools.partial(_inproj_kernel, rope=rope, q_tiles=q_tiles, q_scale=q_scale),
        out_shape=jax.ShapeDtypeStruct((b, t, n), BF16),
        grid=(b, t // tm, n // tn),
        in_specs=in_specs,
        out_specs=pl.BlockSpec((1, tm, tn), lambda bi, i, j: (bi, i, j)),
        compiler_params=_params("parallel", "parallel", "parallel"),
        name="inproj_rope" if rope else "inproj",
    )(*args)


def _na_bias_table(rel_bias, rows):
    rq, kwin = NA_ROWS_PER_STEP, NA_KEY_ROWS
    nb = rows // rq
    h = rel_bias.shape[0]
    blocks = jnp.array([0, min(1, nb - 1), nb - 1], jnp.int32)
    r = blocks[:, None] * rq + jnp.arange(rq, dtype=jnp.int32)[None, :]
    kst = jnp.clip(blocks * rq - NA_WIN_R // 2, 0, rows - kwin)
    kr = kst[:, None] + jnp.arange(kwin, dtype=jnp.int32)[None, :]
    s_r = jnp.clip(r - NA_WIN_R // 2, 0, rows - NA_WIN_R)
    row_ok = (kr[:, None, :] >= s_r[:, :, None]) & (kr[:, None, :] < s_r[:, :, None] + NA_WIN_R)
    dr = jnp.clip(kr[:, None, :] - r[:, :, None] + (NA_WIN_R - 1), 0, 2 * NA_WIN_R - 2)
    col = jnp.arange(GRID_W, dtype=jnp.int32)
    c0 = jnp.clip(col - NA_WIN_C // 2, 0, GRID_W - NA_WIN_C)
    col_ok = (col[None, :] >= c0[:, None]) & (col[None, :] < c0[:, None] + NA_WIN_C)
    dc = jnp.clip(col[None, :] - col[:, None] + (NA_WIN_C - 1), 0, 2 * NA_WIN_C - 2)
    pick_dc = (dc[:, :, None] == jnp.arange(2 * NA_WIN_C - 1)[None, None, :]).astype(F32)
    pick_dr = (dr[..., None] == jnp.arange(2 * NA_WIN_R - 1)[None, None, None, :]).astype(F32)
    by_col = jnp.einsum('hrd,qkd->hrqk', rel_bias.astype(F32), pick_dc, precision=lax.Precision.HIGHEST)
    bias = jnp.einsum('vqlr,hrab->vhqalb', pick_dr, by_col, precision=lax.Precision.HIGHEST)
    ok = row_ok[:, None, :, None, :, None] & col_ok[None, None, None, :, None, :]
    bias = jnp.where(ok, bias, NEG_INF)
    return bias.reshape(3, h, rq * GRID_W, kwin * GRID_W)


def _na_kernel(q_ref, qn_ref, k_ref, v_ref, kc_ref, vc_ref, bias_ref, o_ref, s_loc, s_ctx, *, rows):
    i = pl.program_id(2)
    window = NA_KEY_ROWS * GRID_W
    kc = kc_ref[0]
    vc = vc_ref[0]
    lane = lax.broadcasted_iota(jnp.int32, q_ref.shape[1:], 1)

    def window_start(step):
        kstart = jnp.clip(step * NA_ROWS_PER_STEP - NA_WIN_R // 2, 0, rows - NA_KEY_ROWS)
        return pl.multiple_of(kstart * GRID_W, GRID_W)

    def raw_scores(q, step, hh):
        own = (lane >= hh * HEAD_DIM) & (lane < (hh + 1) * HEAD_DIM)
        qh = jnp.where(own, q * (HEAD_DIM ** -0.5), 0.0).astype(BF16)
        s_loc[hh] = _nt_dot(qh, k_ref[0, pl.ds(window_start(step), window), :])
        s_ctx[hh] = _nt_dot(qh, kc)

    @pl.when(i == 0)
    def _():
        for hh in range(LANES // HEAD_DIM):
            raw_scores(q_ref[0], 0, hh)

    following = jnp.minimum(i + 1, pl.num_programs(2) - 1)
    vw = v_ref[0, pl.ds(window_start(i), window), :]
    outs = []
    for hh in range(LANES // HEAD_DIM):
        sl = s_loc[hh] + bias_ref[0, hh]
        sc = s_ctx[hh]
        m = jnp.maximum(jnp.max(sl, axis=-1, keepdims=True), jnp.max(sc, axis=-1, keepdims=True))
        p_loc = jnp.exp(sl - m)
        p_ctx = jnp.exp(sc - m)
        l = jnp.sum(p_loc, axis=-1, keepdims=True) + jnp.sum(p_ctx, axis=-1, keepdims=True)
        raw_scores(qn_ref[0], following, hh)
        o = jnp.dot(p_loc.astype(BF16), vw, preferred_element_type=F32)
        o = o + jnp.dot(p_ctx.astype(BF16), vc, preferred_element_type=F32)
        outs.append(o / l)
    o_ref[0] = jnp.where(lane < HEAD_DIM, outs[0], outs[1]).astype(BF16)


def _na_attention(p_lat, p_ctx, bias_tbl):
    b, t, _ = p_lat.shape
    n = p_ctx.shape[1]
    rows = t // GRID_W
    nb = rows // NA_ROWS_PER_STEP
    hp = bias_tbl.shape[1] // 2
    tq = NA_ROWS_PER_STEP * GRID_W

    def variant(i):
        return jnp.where(i == 0, 0, jnp.where(i == nb - 1, 2, 1))

    return pl.pallas_call(
        functools.partial(_na_kernel, rows=rows),
        out_shape=jax.ShapeDtypeStruct((b, t, hp * LANES), BF16),
        grid=(b, hp, nb),
        in_specs=[
            pl.BlockSpec((1, tq, LANES), lambda bi, h, i: (bi, i, h)),
            pl.BlockSpec((1, tq, LANES), lambda bi, h, i: (bi, jnp.minimum(i + 1, nb - 1), h)),
            pl.BlockSpec((1, t, LANES), lambda bi, h, i: (bi, 0, hp + h)),
            pl.BlockSpec((1, t, LANES), lambda bi, h, i: (bi, 0, 2 * hp + h)),
            pl.BlockSpec((1, n, LANES), lambda bi, h, i: (bi, 0, hp + h)),
            pl.BlockSpec((1, n, LANES), lambda bi, h, i: (bi, 0, 2 * hp + h)),
            pl.BlockSpec((1, 2, tq, NA_KEY_ROWS * GRID_W), lambda bi, h, i: (variant(i), h, 0, 0)),
        ],
        out_specs=pl.BlockSpec((1, tq, LANES), lambda bi, h, i: (bi, i, h)),
        scratch_shapes=[
            pltpu.VMEM((LANES // HEAD_DIM, tq, NA_KEY_ROWS * GRID_W), F32),
            pltpu.VMEM((LANES // HEAD_DIM, tq, n), F32),
        ],
        compiler_params=_params("parallel", "parallel", "arbitrary"),
        name="na_attention",
    )(p_lat, p_lat, p_lat, p_lat, p_ctx, p_ctx, bias_tbl)


def _ctx_attn_kernel(q_ref, k_ref, v_ref, o_ref):
    q, k, v = q_ref[0], k_ref[0], v_ref[0]
    scale = HEAD_DIM ** -0.5
    lane = lax.broadcasted_iota(jnp.int32, q.shape, 1)
    outs = []
    for hh in range(LANES // HEAD_DIM):
        own = (lane >= hh * HEAD_DIM) & (lane < (hh + 1) * HEAD_DIM)
        s = _nt_dot(jnp.where(own, q, 0.0).astype(BF16), k) * scale
        p = jnp.exp(s - jnp.max(s, axis=-1, keepdims=True))
        l = jnp.sum(p, axis=-1, keepdims=True)
        outs.append(jnp.dot(p.astype(BF16), v, preferred_element_type=F32) / l)
    o_ref[0] = jnp.where(lane < HEAD_DIM, outs[0], outs[1]).astype(BF16)


def _ctx_attention(p_ctx, hp):
    b, n, _ = p_ctx.shape
    return pl.pallas_call(
        _ctx_attn_kernel,
        out_shape=jax.ShapeDtypeStruct((b, n, hp * LANES), BF16),
        grid=(b, hp),
        in_specs=[
            pl.BlockSpec((1, n, LANES), lambda bi, h: (bi, 0, h)),
            pl.BlockSpec((1, n, LANES), lambda bi, h: (bi, 0, hp + h)),
            pl.BlockSpec((1, n, LANES), lambda bi, h: (bi, 0, 2 * hp + h)),
        ],
        out_specs=pl.BlockSpec((1, n, LANES), lambda bi, h: (bi, 0, h)),
        compiler_params=_params("parallel", "parallel"),
        name="ctx_attention",
    )(p_ctx, p_ctx, p_ctx)


def _gla_kernel(*refs, backward, nc):
    if backward:
        (q_ref, k_ref, v_ref, a_ref, wg_ref, bg_ref, s0_ref, of_ref, g_ref, gn_ref,
         o_ref, sout_ref, s_scr) = refs
    else:
        q_ref, k_ref, v_ref, a_ref, wg_ref, bg_ref, s0_ref, o_ref, sout_ref, s_scr = refs
    j = pl.program_id(1)
    tb = nc * GLA_CHUNK
    kw = GLA_HEADS * GLA_DK

    @pl.when(j == 0)
    def _():
        s_scr[...] = s0_ref[0]

    pre = jnp.dot(a_ref[0], wg_ref[...], preferred_element_type=F32) + bg_ref[...]
    log_a = (jnp.minimum(pre, 0.0) - jnp.log1p(jnp.exp(-jnp.abs(pre)))) * (1.0 / GLA_NORMALIZER)

    pos = lax.broadcasted_iota(jnp.int32, (tb, kw), 0) % GLA_CHUNK
    bsum = log_a
    step = 1
    while step < GLA_CHUNK:
        if backward:
            bsum = bsum + jnp.where(pos < GLA_CHUNK - step, pltpu.roll(bsum, tb - step, 0), 0.0)
        else:
            bsum = bsum + jnp.where(pos >= step, pltpu.roll(bsum, step, 0), 0.0)
        step *= 2

    b3 = bsum.reshape(nc, GLA_CHUNK, kw)
    b_end = b3[:, 0:1, :] if backward else b3[:, GLA_CHUNK - 1:GLA_CHUNK, :]
    q3 = q_ref[0].astype(F32).reshape(nc, GLA_CHUNK, kw)
    k3 = k_ref[0].astype(F32).reshape(nc, GLA_CHUNK, kw)
    q_in = q3 * (GLA_DK ** -0.5) * jnp.exp(b3)
    k_out = (k3 * jnp.exp(-b3)).astype(BF16)
    k_dec = (k3 * jnp.exp(b_end - b3)).astype(BF16)
    decay = jnp.exp(b_end)
    v3 = v_ref[0].reshape(nc, GLA_CHUNK, GLA_HEADS * GLA_DV)

    t_idx = lax.broadcasted_iota(jnp.int32, (nc, GLA_CHUNK, GLA_CHUNK), 1)
    s_idx = lax.broadcasted_iota(jnp.int32, (nc, GLA_CHUNK, GLA_CHUNK), 2)
    tri = (s_idx > t_idx) if backward else (s_idx <= t_idx)
    eye = (lax.broadcasted_iota(jnp.int32, (nc, GLA_DV, GLA_DV), 1)
           == lax.broadcasted_iota(jnp.int32, (nc, GLA_DV, GLA_DV), 2)).astype(BF16)
    lane = lax.broadcasted_iota(jnp.int32, (nc, GLA_CHUNK, LANES), 2)
    order = range(nc - 1, -1, -1) if backward else range(nc)

    outs = []
    for h in range(GLA_HEADS):
        pair = slice((h // 2) * LANES, (h // 2 + 1) * LANES)
        own = (lane >= GLA_DK) if h % 2 else (lane < GLA_DK)
        qm = jnp.where(own, q_in[:, :, pair], 0.0).astype(BF16)
        vh = v3[:, :, h * GLA_DV:(h + 1) * GLA_DV]
        att = jnp.einsum('ctd,csd->cts', qm, k_out[:, :, pair], preferred_element_type=F32)
        att = jnp.where(tri, att, 0.0).astype(BF16)
        o_h = jnp.einsum('cts,csv->ctv', att, vh, preferred_element_type=F32)
        vt = jnp.einsum('cvw,csw->cvs', eye, vh, preferred_element_type=F32).astype(BF16)
        ut = jnp.einsum('cvs,csd->cvd', vt, k_dec[:, :, pair], preferred_element_type=F32)
        s_h = s_scr[h]
        before = [None] * nc
        for c in order:
            before[c] = s_h
            s_h = s_h * decay[c, :, pair] + ut[c]
        s_scr[h] = s_h
        s_all = jnp.stack(before, axis=0).astype(BF16)
        o_h = o_h + jnp.einsum('ctd,cvd->ctv', qm, s_all, preferred_element_type=F32)
        outs.append(o_h.reshape(tb, GLA_DV))

    @pl.when(j == pl.num_programs(1) - 1)
    def _():
        sout_ref[0] = s_scr[...]

    if not backward:
        o_ref[0] = jnp.concatenate(outs, axis=-1)
        return
    of = of_ref[0]
    g = g_ref[0].astype(F32)
    fin = []
    for h in range(GLA_HEADS):
        sl = slice(h * GLA_DV, (h + 1) * GLA_DV)
        y = _rms(of[:, sl] + outs[h]) * gn_ref[...]
        gh = g[:, sl]
        fin.append(y * (gh * jax.nn.sigmoid(gh)))
    o_ref[0] = jnp.concatenate(fin, axis=-1).astype(BF16)


def _gla_sweep(p, wg, bg, s0, nc, col0, a_col, backward, o_fwd=None, g_norm=None):
    b, n, _ = p.shape
    tb = nc * GLA_CHUNK
    nblk = n // tb
    kw = GLA_HEADS * GLA_DK
    vw = GLA_HEADS * GLA_DV
    cq, ck, cv, cg, ca = col0 // kw, col0 // kw + 1, (col0 + 2 * kw) // vw, (col0 + 2 * kw) // vw + 1, a_col // LANES

    def blk(j):
        return nblk - 1 - j if backward else j

    in_specs = [
        pl.BlockSpec((1, tb, kw), lambda bi, j: (bi, blk(j), cq)),
        pl.BlockSpec((1, tb, kw), lambda bi, j: (bi, blk(j), ck)),
        pl.BlockSpec((1, tb, vw), lambda bi, j: (bi, blk(j), cv)),
        pl.BlockSpec((1, tb, LANES), lambda bi, j: (bi, blk(j), ca)),
        pl.BlockSpec((LANES, kw), lambda bi, j: (0, 0)),
        pl.BlockSpec((1, kw), lambda bi, j: (0, 0)),
        pl.BlockSpec((1, GLA_HEADS, GLA_DV, LANES), lambda bi, j: (bi, 0, 0, 0)),
    ]
    args = [p, p, p, p, wg, bg, s0]
    if backward:
        in_specs += [
            pl.BlockSpec((1, tb, vw), lambda bi, j: (bi, blk(j), 0)),
            pl.BlockSpec((1, tb, vw), lambda bi, j: (bi, blk(j), cg)),
            pl.BlockSpec((1, GLA_DV), lambda bi, j: (0, 0)),
        ]
        args += [o_fwd, p, g_norm]
    return pl.pallas_call(
        functools.partial(_gla_kernel, backward=backward, nc=nc),
        out_shape=(jax.ShapeDtypeStruct((b, n, vw), BF16 if backward else F32),
                   jax.ShapeDtypeStruct((b, GLA_HEADS, GLA_DV, LANES), F32)),
        grid=(b, nblk),
        in_specs=in_specs,
        out_specs=(pl.BlockSpec((1, tb, vw), lambda bi, j: (bi, blk(j), 0)),
                   pl.BlockSpec((1, GLA_HEADS, GLA_DV, LANES), lambda bi, j: (bi, 0, 0, 0))),
        scratch_shapes=[pltpu.VMEM((GLA_HEADS, GLA_DV, LANES), F32)],
        compiler_params=_params("parallel", "arbitrary"),
        name="gla_bwd" if backward else "gla_fwd",
    )(*args)


def _diff_lambda(lam_ref, lam_init):
    lp = lam_ref[...]
    return (jnp.exp(jnp.sum(lp[0:1] * lp[1:2], axis=-1, keepdims=True))
            - jnp.exp(jnp.sum(lp[2:3] * lp[3:4], axis=-1, keepdims=True)) + lam_init)


def _branch_queries(q):
    lane = lax.broadcasted_iota(jnp.int32, q.shape, 1)
    return jnp.concatenate([jnp.where(lane < DIFF_DK, q, 0.0), jnp.where(lane >= DIFF_DK, q, 0.0)], axis=0)


def _diff_ctx_kernel(q_ref, k_ref, v_ref, lam_ref, gn_ref, o_ref, *, lam_init):
    tq = q_ref.shape[1]
    s = _nt_dot(_branch_queries(q_ref[0]), k_ref[0])
    p = jnp.exp2(s - jnp.max(s, axis=-1, keepdims=True))
    a = jnp.dot(p.astype(BF16), v_ref[0], preferred_element_type=F32) / jnp.sum(p, axis=-1, keepdims=True)
    o = a[0:tq] - _diff_lambda(lam_ref, lam_init) * a[tq:2 * tq]
    o_ref[0] = (_rms(o) * gn_ref[...] * (1.0 - lam_init)).astype(BF16)


def _diff_attention_ctx(qk_c, p_c, lam_params, g_norm, lam_init, v_col0):
    b, n, w = qk_c.shape
    heads = w // (2 * LANES)
    vb = v_col0 // LANES
    return pl.pallas_call(
        functools.partial(_diff_ctx_kernel, lam_init=lam_init),
        out_shape=jax.ShapeDtypeStruct((b, n, heads * DIFF_DV), BF16),
        grid=(b, heads),
        in_specs=[
            pl.BlockSpec((1, n, LANES), lambda bi, h: (bi, 0, h)),
            pl.BlockSpec((1, n, LANES), lambda bi, h: (bi, 0, heads + h)),
            pl.BlockSpec((1, n, LANES), lambda bi, h: (bi, 0, vb + h)),
            pl.BlockSpec((4, DIFF_DK), lambda bi, h: (0, 0)),
            pl.BlockSpec((1, DIFF_DV), lambda bi, h: (0, 0)),
        ],
        out_specs=pl.BlockSpec((1, n, LANES), lambda bi, h: (bi, 0, h)),
        compiler_params=_params("parallel", "parallel"),
        name="diff_attention_ctx",
    )(qk_c, qk_c, p_c, lam_params, g_norm)


def _diff_kernel(q_ref, k_ref, v_ref, kc_ref, vc_ref, lam_ref, gn_ref, o_ref,
                 qs, vt, vct, s_buf, c_buf, p_buf, a_buf, m_scr, acc, *, nch, lam_init):
    tq = q_ref.shape[1]
    tkc = k_ref.shape[1] // nch

    def transposed_with_ones(v):
        pad_rows = lax.broadcasted_iota(jnp.int32, (DIFF_VT_ROWS - DIFF_DV, v.shape[0]), 0)
        return jnp.concatenate([v.astype(F32).T.astype(BF16), jnp.where(pad_rows == 0, 1.0, 0.0).astype(BF16)],
                               axis=0)

    @pl.when(pl.program_id(2) == 0)
    def _():
        for c in range(nch):
            vt[c] = transposed_with_ones(v_ref[0, c * tkc:(c + 1) * tkc, :])
        vct[...] = transposed_with_ones(vc_ref[0])

    def stage(value_chunk, softmax_on, score_chunk):
        for n in range(2 * tq // MXU_N):
            sl = slice(n * MXU_N, (n + 1) * MXU_N)
            if value_chunk is not None:
                pv = jnp.dot(vt[value_chunk], p_buf[:, sl], preferred_element_type=F32)
                acc[:, sl] = acc[:, sl] * a_buf[:, sl] + pv
            if softmax_on:
                m_old = m_scr[:, sl]
                m_new = jnp.maximum(m_old, c_buf[:, sl])
                a_buf[:, sl] = jnp.exp2(m_old - m_new)
                m_scr[:, sl] = m_new
                p_buf[:, sl] = jnp.exp2(s_buf[:, sl] - m_new).astype(BF16)
            if score_chunk is not None:
                k = k_ref[0, pl.ds(pl.multiple_of(score_chunk * tkc, tkc), tkc), :]
                s = _nt_dot(k, qs[sl, :])
                s_buf[:, sl] = s
                c_buf[:, sl] = jnp.max(s, axis=0, keepdims=True)

    qs[...] = _branch_queries(q_ref[0])
    sc = _nt_dot(kc_ref[0], qs[...])
    mc = jnp.max(sc, axis=0, keepdims=True)
    pc = jnp.exp2(sc - mc)
    m_scr[...] = mc
    acc[...] = jnp.dot(vct[...], pc.astype(BF16), preferred_element_type=F32)
    stage(None, False, 0)
    stage(None, True, 1)

    def steady(c, carry):
        stage(c - 1, True, c + 1)
        return carry

    lax.fori_loop(1, nch - 1, steady, 0, unroll=2 if nch % 2 == 0 else 1)
    stage(nch - 2, True, None)
    stage(nch - 1, False, None)

    a = acc[0:DIFF_DV, :] / acc[DIFF_DV:DIFF_DV + 1, :]
    o = a[:, 0:tq] - _diff_lambda(lam_ref, lam_init) * a[:, tq:2 * tq]
    y = o * lax.rsqrt(jnp.mean(o * o, axis=0, keepdims=True) + EPS) * gn_ref[...] * (1.0 - lam_init)
    o_ref[0] = y.T.astype(BF16)


def _diff_attention(qk_x, p_x, qk_c, p_c, lam_params, g_norm, lam_init, tq, tkc, v_col0):
    b, t, w = qk_x.shape
    heads = w // (2 * LANES)
    n_ctx = qk_c.shape[1]
    nch = t // tkc
    vb = v_col0 // LANES
    assert nch >= 2 and nch * tkc == t
    return pl.pallas_call(
        functools.partial(_diff_kernel, nch=nch, lam_init=lam_init),
        out_shape=jax.ShapeDtypeStruct((b, t, heads * DIFF_DV), BF16),
        grid=(b, heads, t // tq),
        in_specs=[
            pl.BlockSpec((1, tq, LANES), lambda bi, h, i: (bi, i, h)),
            pl.BlockSpec((1, t, LANES), lambda bi, h, i: (bi, 0, heads + h)),
            pl.BlockSpec((1, t, LANES), lambda bi, h, i: (bi, 0, vb + h)),
            pl.BlockSpec((1, n_ctx, LANES), lambda bi, h, i: (bi, 0, heads + h)),
            pl.BlockSpec((1, n_ctx, LANES), lambda bi, h, i: (bi, 0, vb + h)),
            pl.BlockSpec((4, DIFF_DK), lambda bi, h, i: (0, 0)),
            pl.BlockSpec((DIFF_DV, 1), lambda bi, h, i: (0, 0)),
        ],
        out_specs=pl.BlockSpec((1, tq, LANES), lambda bi, h, i: (bi, i, h)),
        scratch_shapes=[
            pltpu.VMEM((2 * tq, LANES), BF16),
            pltpu.VMEM((nch, DIFF_VT_ROWS, tkc), BF16),
            pltpu.VMEM((DIFF_VT_ROWS, n_ctx), BF16),
            pltpu.VMEM((tkc, 2 * tq), F32),
            pltpu.VMEM((1, 2 * tq), F32),
            pltpu.VMEM((tkc, 2 * tq), BF16),
            pltpu.VMEM((1, 2 * tq), F32),
            pltpu.VMEM((1, 2 * tq), F32),
            pltpu.VMEM((DIFF_VT_ROWS, 2 * tq), F32),
        ],
        compiler_params=_params("parallel", "parallel", "arbitrary"),
        name="diff_attention",
    )(qk_x, qk_x, p_x, qk_c, p_c, lam_params, g_norm.reshape(DIFF_DV, 1))


def _outproj_kernel(ya_ref, yb_ref, yd_ref, w_ref, x_ref, g_ref, gate_ref, gmlp_ref, shift_ref, scale_ref,
                    o_ref, h_ref):
    wa, wb = ya_ref.shape[2], yb_ref.shape[2]
    for r in range(x_ref.shape[1] // ROW_GROUP):
        rows = slice(r * ROW_GROUP, (r + 1) * ROW_GROUP)
        y = jnp.dot(ya_ref[0, rows, :], w_ref[0:wa, :], preferred_element_type=F32)
        y = y + jnp.dot(yb_ref[0, rows, :], w_ref[wa:wa + wb, :], preferred_element_type=F32)
        y = y + jnp.dot(yd_ref[0, rows, :], w_ref[wa + wb:, :], preferred_element_type=F32)
        x_new = x_ref[0, rows, :] + gate_ref[0] * (_rms(y) * g_ref[...])
        o_ref[0, rows, :] = x_new
        h_ref[0, rows, :] = _modulate(x_new, gmlp_ref, shift_ref, scale_ref)


def _outproj(ya, yb, yd, w, x, g_post, g_mlp, mods, mod_row0, tm):
    b, t, d = x.shape

    def mod(comp):
        return pl.BlockSpec((1, 1, d), lambda bi, i: ((bi + mod_row0) * 6 + comp, 0, 0))

    row = pl.BlockSpec((1, d), lambda bi, i: (0, 0))
    tile = pl.BlockSpec((1, tm, d), lambda bi, i: (bi, i, 0))
    return pl.pallas_call(
        _outproj_kernel,
        out_shape=(jax.ShapeDtypeStruct((b, t, d), F32), jax.ShapeDtypeStruct((b, t, d), BF16)),
        grid=(b, t // tm),
        in_specs=[
            pl.BlockSpec((1, tm, ya.shape[2]), lambda bi, i: (bi, i, 0)),
            pl.BlockSpec((1, tm, yb.shape[2]), lambda bi, i: (bi, i, 0)),
            pl.BlockSpec((1, tm, yd.shape[2]), lambda bi, i: (bi, i, 0)),
            pl.BlockSpec(w.shape, lambda bi, i: (0, 0)),
            tile, row, mod(2), row, mod(3), mod(4),
        ],
        out_specs=(tile, tile),
        compiler_params=_params("parallel", "parallel"),
        name="outproj",
    )(ya, yb, yd, w, x, g_post, mods, g_mlp, mods, mods)


def _mlp_kernel(x_ref, h_ref, w1_ref, w2_ref, gpost_ref, gate_ref, *rest, emit_next):
    if emit_next:
        gnext_ref, shift_ref, scale_ref, o_ref, hnext_ref, acc = rest
    else:
        o_ref, acc = rest
    k = pl.program_id(2)

    @pl.when(k == 0)
    def _():
        acc[...] = jnp.zeros(acc.shape, F32)

    def hidden_tile(rows):
        u = jnp.maximum(jnp.dot(h_ref[0, rows, :], w1_ref[...], preferred_element_type=F32), 0.0)
        return jnp.dot((u * u).astype(BF16), w2_ref[...], preferred_element_type=F32)

    last = pl.num_programs(2) - 1

    @pl.when(k < last)
    def _():
        acc[...] += hidden_tile(slice(None))

    @pl.when(k == last)
    def _():
        for r in range(acc.shape[0] // ROW_GROUP):
            rows = slice(r * ROW_GROUP, (r + 1) * ROW_GROUP)
            y = acc[rows, :] + hidden_tile(rows)
            x_new = x_ref[0, rows, :] + gate_ref[0] * (_rms(y) * gpost_ref[...])
            o_ref[0, rows, :] = x_new
            if emit_next:
                hnext_ref[0, rows, :] = _modulate(x_new, gnext_ref, shift_ref, scale_ref)


def _mlp(x, h, gpost, mods, mod_row0, w1, w2, tm, th, g_next=None, mods_next=None):
    b, t, d = x.shape
    hid = w1.shape[1]
    emit_next = g_next is not None

    def mod(comp):
        return pl.BlockSpec((1, 1, d), lambda bi, i, k: ((bi + mod_row0) * 6 + comp, 0, 0))

    row = pl.BlockSpec((1, d), lambda bi, i, k: (0, 0))
    tile = pl.BlockSpec((1, tm, d), lambda bi, i, k: (bi, i, 0))
    in_specs = [
        tile, tile,
        pl.BlockSpec((d, th), lambda bi, i, k: (0, k)),
        pl.BlockSpec((th, d), lambda bi, i, k: (k, 0)),
        row, mod(5),
    ]
    args = [x, h, w1, w2, gpost, mods]
    out_shape = [jax.ShapeDtypeStruct((b, t, d), F32)]
    out_specs = [tile]
    if emit_next:
        in_specs += [row, mod(0), mod(1)]
        args += [g_next, mods_next, mods_next]
        out_shape.append(jax.ShapeDtypeStruct((b, t, d), BF16))
        out_specs.append(tile)
    out = pl.pallas_call(
        functools.partial(_mlp_kernel, emit_next=emit_next),
        out_shape=tuple(out_shape),
        grid=(b, t // tm, hid // th),
        in_specs=in_specs,
        out_specs=tuple(out_specs),
        scratch_shapes=[pltpu.VMEM((tm, d), F32)],
        compiler_params=_params("parallel", "parallel", "arbitrary"),
        name="mlp",
    )(*args)
    return out if emit_next else (out[0], None)


def _rope_tables(t):
    tok = jnp.arange(t, dtype=jnp.int32)
    row = (tok // GRID_W).astype(F32)
    col = (tok % GRID_W).astype(F32)
    n_freq = DIFF_DK // 4
    inv = ROPE_BASE ** (-jnp.arange(n_freq, dtype=F32) / n_freq)
    ang = jnp.concatenate([row[:, None] * inv, col[:, None] * inv], axis=-1)
    cos, sin = jnp.cos(ang), jnp.sin(ang)
    reps = LANES // DIFF_DK
    return (jnp.tile(jnp.concatenate([cos, cos], axis=-1), (1, reps)),
            jnp.tile(jnp.concatenate([-sin, sin], axis=-1), (1, reps)))


def _pick(n, pref):
    return pref if n % pref == 0 else n


def kernel(x, c, ctx, c_ctx, w_ada, b_ada, g_pre_mix, g_post_mix, g_pre_mlp, g_post_mlp, w_in, w_out,
           na_rel_bias, gla_w_gate, gla_b_gate, gla_g_norm, diff_lambda, diff_g_norm, w_mlp_in, w_mlp_out):
    bsz, t, d = x.shape
    n_ctx = ctx.shape[1]
    depth = w_ada.shape[0]
    na_heads = na_rel_bias.shape[1]
    na_w = na_heads * HEAD_DIM
    gla_kw = GLA_HEADS * GLA_DK
    gla_vw = GLA_HEADS * GLA_DV
    diff_w = w_out.shape[1] - na_w - gla_vw
    diff_heads = diff_w // DIFF_DV
    diff_qk = diff_heads * 2 * DIFF_DK
    rows = t // GRID_W
    assert t % (NA_ROWS_PER_STEP * GRID_W) == 0 and rows >= NA_KEY_ROWS
    assert bsz + 1 <= 8

    o_gla = 3 * na_w
    o_gate = o_gla + 2 * gla_kw + 2 * gla_vw
    o_diff = o_gate + 2 * GLA_GATE_RANK
    gla_col0 = 3 * na_w
    diffv_col0 = gla_col0 + 2 * gla_kw + 2 * gla_vw
    gate_col0 = diffv_col0 + diff_w

    cc = jnp.zeros((8, d), F32).at[:bsz].set(c).at[bsz].set(c_ctx)
    mods_all = _mods(cc, w_ada, b_ada)
    rope_tabs = _rope_tables(t)
    xc = ctx.reshape(1, bsz * n_ctx, d)
    zeros_state = jnp.zeros((bsz, GLA_HEADS, GLA_DV, LANES), F32)

    tm_in = _pick(t, 1024)
    tm_out = _pick(t, 512)
    tm_ctx = _pick(bsz * n_ctx, 512)
    tq = _pick(t, 512)
    tkc = min(2048, t // 2)
    nc_lat = _pick(t // GLA_CHUNK, 16)
    nc_ctx = _pick(n_ctx // GLA_CHUNK, 4)

    mods_of = [mods_all[l].reshape(8 * 6, 1, d) for l in range(depth)]
    h_x = _premix(x, g_pre_mix[0].reshape(1, d), mods_of[0], 0, tm_out)
    h_c = _premix(xc, g_pre_mix[0].reshape(1, d), mods_of[0], bsz, tm_ctx)

    for l in range(depth):
        need_ctx = l < depth - 1
        mods = mods_of[l]
        g_next = g_pre_mix[l + 1].reshape(1, d) if need_ctx else None
        mods_next = mods_of[l + 1] if need_ctx else None
        wl = w_in[l].astype(BF16)
        w_qk = wl[:, o_diff:o_diff + 2 * diff_qk]
        w_rest = jnp.concatenate(
            [wl[:, :o_gate], wl[:, o_diff + 2 * diff_qk:], wl[:, o_gate:o_diff],
             jnp.zeros((d, LANES - 2 * GLA_GATE_RANK), BF16)], axis=1)
        w_o = w_out[l].astype(BF16)
        w1 = w_mlp_in[l].astype(BF16)
        w2 = w_mlp_out[l].astype(BF16)
        g2 = g_post_mix[l].reshape(1, d)
        g3 = g_pre_mlp[l].reshape(1, d)
        g4 = g_post_mlp[l].reshape(1, d)
        wg = jnp.zeros((2, LANES, gla_kw), F32)
        wg = wg.at[0, :GLA_GATE_RANK].set(gla_w_gate[l, 0]).at[1, GLA_GATE_RANK:2 * GLA_GATE_RANK].set(gla_w_gate[l, 1])
        wg = wg.astype(BF16)
        bg = gla_b_gate[l].reshape(2, 1, gla_kw)
        gla_gn = gla_g_norm[l].reshape(1, GLA_DV)
        diff_gn = diff_g_norm[l].reshape(1, DIFF_DV)
        lam_init = 0.8 - 0.6 * math.exp(-0.3 * l)

        qk_x = _inproj(h_x, w_qk, tm_in, diff_qk, rope_tabs, 1, DIFF_Q_SCALE)
        p_x = _inproj(h_x, w_rest, tm_in, w_rest.shape[1] // 3)
        qk_c = _inproj(h_c, w_qk, tm_ctx, diff_qk, None, 1, DIFF_Q_SCALE).reshape(bsz, n_ctx, -1)
        p_c = _inproj(h_c, w_rest, tm_ctx, w_rest.shape[1] // 3).reshape(bsz, n_ctx, -1)

        ya = _na_attention(p_x, p_c, _na_bias_table(na_rel_bias[l], rows))

        of_c, s_f = _gla_sweep(p_c, wg[0], bg[0], zeros_state, nc_ctx, gla_col0, gate_col0, False)
        yb_c, s_b = _gla_sweep(p_c, wg[1], bg[1], zeros_state, nc_ctx, gla_col0, gate_col0, True, of_c, gla_gn)
        of_x, _ = _gla_sweep(p_x, wg[0], bg[0], s_f, nc_lat, gla_col0, gate_col0, False)
        yb, _ = _gla_sweep(p_x, wg[1], bg[1], s_b, nc_lat, gla_col0, gate_col0, True, of_x, gla_gn)

        yd = _diff_attention(qk_x, p_x, qk_c, p_c, diff_lambda[l], diff_gn, lam_init, tq, tkc, diffv_col0)

        x, hm_x = _outproj(ya, yb, yd, w_o, x, g2, g3, mods, 0, tm_out)
        x, h_x = _mlp(x, hm_x, g4, mods, 0, w1, w2, tm_out, MLP_HIDDEN_TILE, g_next, mods_next)

        if need_ctx:
            ya_c = _ctx_attention(p_c, na_heads // 2)
            yd_c = _diff_attention_ctx(qk_c, p_c, diff_lambda[l], diff_gn, lam_init, diffv_col0)
            flat = lambda z: z.reshape(1, bsz * n_ctx, -1)
            xc, hm_c = _outproj(flat(ya_c), flat(yb_c), flat(yd_c), w_o, xc, g2, g3, mods, bsz, tm_ctx)
            xc, h_c = _mlp(xc, hm_c, g4, mods, bsz, w1, w2, tm_ctx, MLP_HIDDEN_TILE, g_next, mods_next)
    return x
```

```python
import functools
import math

import jax
import jax.numpy as jnp
from jax import lax
from jax.experimental import pallas as pl
from jax.experimental.pallas import tpu as pltpu

F32 = jnp.float32
BF16 = jnp.bfloat16

GRID_W = 64
EPS = 1e-6
NEG_INF = -1e30
HEAD_DIM = 64
NA_WIN_R = 8
NA_WIN_C = 16
GLA_HEADS = 4
GLA_DK = 64
GLA_DV = 128
GLA_GATE_RANK = 16
GLA_NORMALIZER = 16.0
GLA_CHUNK = 64
DIFF_DK = 64
DIFF_DV = 128
ROPE_BASE = 10000.0
DIFF_Q_SCALE = DIFF_DK ** -0.5 * math.log2(math.e)

LANES = 128
MXU_N = 256
DIFF_VT_ROWS = 128 + 16
VMEM_LIMIT = 56 * 1024 * 1024

MLP_HIDDEN_TILE = 1024
ROW_GROUP = 256
NA_ROWS_PER_STEP = 8
NA_KEY_ROWS = NA_ROWS_PER_STEP + NA_WIN_R


def _params(*sem):
    return pltpu.CompilerParams(dimension_semantics=sem, vmem_limit_bytes=VMEM_LIMIT)


def _nt_dot(a, b):
    return lax.dot_general(a, b, (((1,), (1,)), ((), ())), preferred_element_type=F32)


def _rms(y):
    return y * lax.rsqrt(jnp.mean(y * y, axis=-1, keepdims=True) + EPS)


def _mods_kernel(c_ref, w_ref, b_ref, o_ref):
    c = c_ref[...]
    s = (c * jax.nn.sigmoid(c)).astype(BF16)
    o_ref[0] = jnp.dot(s, w_ref[0].astype(BF16), preferred_element_type=F32) + b_ref[0]


def _mods(cc, w_ada, b_ada):
    depth, d, n = w_ada.shape
    tn = 1024
    return pl.pallas_call(
        _mods_kernel,
        out_shape=jax.ShapeDtypeStruct((depth, 8, n), F32),
        grid=(depth, n // tn),
        in_specs=[
            pl.BlockSpec((8, d), lambda l, j: (0, 0)),
            pl.BlockSpec((1, d, tn), lambda l, j: (l, 0, j)),
            pl.BlockSpec((1, 1, tn), lambda l, j: (l, 0, j)),
        ],
        out_specs=pl.BlockSpec((1, 8, tn), lambda l, j: (l, 0, j)),
        compiler_params=_params("arbitrary", "arbitrary"),
        name="ada_mods",
    )(cc, w_ada, b_ada.reshape(depth, 1, n))


def _modulate(x, g_ref, shift_ref, scale_ref):
    return (_rms(x) * g_ref[...] * (1.0 + scale_ref[0]) + shift_ref[0]).astype(BF16)


def _premix_kernel(x_ref, g_ref, shift_ref, scale_ref, o_ref):
    o_ref[0] = _modulate(x_ref[0], g_ref, shift_ref, scale_ref)


def _premix(x, g, mods, mod_row0, tm):
    b, t, d = x.shape
    return pl.pallas_call(
        _premix_kernel,
        out_shape=jax.ShapeDtypeStruct((b, t, d), BF16),
        grid=(b, t // tm),
        in_specs=[
            pl.BlockSpec((1, tm, d), lambda bi, i: (bi, i, 0)),
            pl.BlockSpec((1, d), lambda bi, i: (0, 0)),
            pl.BlockSpec((1, 1, d), lambda bi, i: ((bi + mod_row0) * 6 + 0, 0, 0)),
            pl.BlockSpec((1, 1, d), lambda bi, i: ((bi + mod_row0) * 6 + 1, 0, 0)),
        ],
        out_specs=pl.BlockSpec((1, tm, d), lambda bi, i: (bi, i, 0)),
        compiler_params=_params("parallel", "parallel"),
        name="premix",
    )(x, g, mods, mods)


def _inproj_kernel(h_ref, w_ref, *rest, rope, q_tiles, q_scale):
    if rope:
        cos_ref, sin_ref, o_ref = rest
    else:
        (o_ref,) = rest
    for r in range(h_ref.shape[1] // ROW_GROUP):
        rows = slice(r * ROW_GROUP, (r + 1) * ROW_GROUP)
        acc = jnp.dot(h_ref[0, rows, :], w_ref[...], preferred_element_type=F32)
        if q_tiles:
            acc = acc * jnp.where(pl.program_id(2) < q_tiles, q_scale, 1.0)
        if not rope:
            o_ref[0, rows, :] = acc.astype(BF16)
            continue
        cos = cos_ref[rows, :]
        sin = sin_ref[rows, :]
        lower = (lax.broadcasted_iota(jnp.int32, cos.shape, 1) % DIFF_DK) < DIFF_DK // 2
        for c in range(acc.shape[1] // LANES):
            a = acc[:, c * LANES:(c + 1) * LANES]
            partner = jnp.where(lower, pltpu.roll(a, LANES - DIFF_DK // 2, 1), pltpu.roll(a, DIFF_DK // 2, 1))
            o_ref[0, rows, c * LANES:(c + 1) * LANES] = (a * cos + partner * sin).astype(BF16)


def _inproj(h, w, tm, tn, rope_tabs=None, q_tiles=0, q_scale=1.0):
    b, t, d = h.shape
    n = w.shape[1]
    rope = rope_tabs is not None
    in_specs = [
        pl.BlockSpec((1, tm, d), lambda bi, i, j: (bi, i, 0)),
        pl.BlockSpec((d, tn), lambda bi, i, j: (0, j)),
    ]
    args = [h, w]
    if rope:
        in_specs += [pl.BlockSpec((tm, LANES), lambda bi, i, j: (i, 0))] * 2
        args += list(rope_tabs)
    return pl.pallas_call(
        functools.partial(_inproj_kernel, rope=rope, q_tiles=q_tiles, q_scale=q_scale),
        out_shape=jax.ShapeDtypeStruct((b, t, n), BF16),
        grid=(b, t // tm, n // tn),
        in_specs=in_specs,
        out_specs=pl.BlockSpec((1, tm, tn), lambda bi, i, j: (bi, i, j)),
        compiler_params=_params("parallel", "parallel", "parallel"),
        name="inproj_rope" if rope else "inproj",
    )(*args)


def _na_bias_table(rel_bias, rows):
    rq, kwin = NA_ROWS_PER_STEP, NA_KEY_ROWS
    nb = rows // rq
    h = rel_bias.shape[0]
    blocks = jnp.array([0, min(1, nb - 1), nb - 1], jnp.int32)
    r = blocks[:, None] * rq + jnp.arange(rq, dtype=jnp.int32)[None, :]
    kst = jnp.clip(blocks * rq - NA_WIN_R // 2, 0, rows - kwin)
    kr = kst[:, None] + jnp.arange(kwin, dtype=jnp.int32)[None, :]
    s_r = jnp.clip(r - NA_WIN_R // 2, 0, rows - NA_WIN_R)
    row_ok = (kr[:, None, :] >= s_r[:, :, None]) & (kr[:, None, :] < s_r[:, :, None] + NA_WIN_R)
    dr = jnp.clip(kr[:, None, :] - r[:, :, None] + (NA_WIN_R - 1), 0, 2 * NA_WIN_R - 2)
    col = jnp.arange(GRID_W, dtype=jnp.int32)
    c0 = jnp.clip(col - NA_WIN_C // 2, 0, GRID_W - NA_WIN_C)
    col_ok = (col[None, :] >= c0[:, None]) & (col[None, :] < c0[:, None] + NA_WIN_C)
    dc = jnp.clip(col[None, :] - col[:, None] + (NA_WIN_C - 1), 0, 2 * NA_WIN_C - 2)
    pick_dc = (dc[:, :, None] == jnp.arange(2 * NA_WIN_C - 1)[None, None, :]).astype(F32)
    pick_dr = (dr[..., None] == jnp.arange(2 * NA_WIN_R - 1)[None, None, None, :]).astype(F32)
    by_col = jnp.einsum('hrd,qkd->hrqk', rel_bias.astype(F32), pick_dc, precision=lax.Precision.HIGHEST)
    bias = jnp.einsum('vqlr,hrab->vhqalb', pick_dr, by_col, precision=lax.Precision.HIGHEST)
    ok = row_ok[:, None, :, None, :, None] & col_ok[None, None, None, :, None, :]
    bias = jnp.where(ok, bias, NEG_INF)
    return bias.reshape(3, h, rq * GRID_W, kwin * GRID_W)


def _na_kernel(q_ref, qn_ref, k_ref, v_ref, kc_ref, vc_ref, bias_ref, o_ref, s_loc, s_ctx, *, rows):
    i = pl.program_id(2)
    window = NA_KEY_ROWS * GRID_W
    kc = kc_ref[0]
    vc = vc_ref[0]
    lane = lax.broadcasted_iota(jnp.int32, q_ref.shape[1:], 1)

    def window_start(step):
        kstart = jnp.clip(step * NA_ROWS_PER_STEP - NA_WIN_R // 2, 0, rows - NA_KEY_ROWS)
        return pl.multiple_of(kstart * GRID_W, GRID_W)

    def raw_scores(q, step, hh):
        own = (lane >= hh * HEAD_DIM) & (lane < (hh + 1) * HEAD_DIM)
        qh = jnp.where(own, q * (HEAD_DIM ** -0.5), 0.0).astype(BF16)
        s_loc[hh] = _nt_dot(qh, k_ref[0, pl.ds(window_start(step), window), :])
        s_ctx[hh] = _nt_dot(qh, kc)

    @pl.when(i == 0)
    def _():
        for hh in range(LANES // HEAD_DIM):
            raw_scores(q_ref[0], 0, hh)

    following = jnp.minimum(i + 1, pl.num_programs(2) - 1)
    vw = v_ref[0, pl.ds(window_start(i), window), :]
    outs = []
    for hh in range(LANES // HEAD_DIM):
        sl = s_loc[hh] + bias_ref[0, hh]
        sc = s_ctx[hh]
        m = jnp.maximum(jnp.max(sl, axis=-1, keepdims=True), jnp.max(sc, axis=-1, keepdims=True))
        p_loc = jnp.exp(sl - m)
        p_ctx = jnp.exp(sc - m)
        l = jnp.sum(p_loc, axis=-1, keepdims=True) + jnp.sum(p_ctx, axis=-1, keepdims=True)
        raw_scores(qn_ref[0], following, hh)
        o = jnp.dot(p_loc.astype(BF16), vw, preferred_element_type=F32)
        o = o + jnp.dot(p_ctx.astype(BF16), vc, preferred_element_type=F32)
        outs.append(o / l)
    o_ref[0] = jnp.where(lane < HEAD_DIM, outs[0], outs[1]).astype(BF16)


def _na_attention(p_lat, p_ctx, bias_tbl):
    b, t, _ = p_lat.shape
    n = p_ctx.shape[1]
    rows = t // GRID_W
    nb = rows // NA_ROWS_PER_STEP
    hp = bias_tbl.shape[1] // 2
    tq = NA_ROWS_PER_STEP * GRID_W

    def variant(i):
        return jnp.where(i == 0, 0, jnp.where(i == nb - 1, 2, 1))

    return pl.pallas_call(
        functools.partial(_na_kernel, rows=rows),
        out_shape=jax.ShapeDtypeStruct((b, t, hp * LANES), BF16),
        grid=(b, hp, nb),
        in_specs=[
            pl.BlockSpec((1, tq, LANES), lambda bi, h, i: (bi, i, h)),
            pl.BlockSpec((1, tq, LANES), lambda bi, h, i: (bi, jnp.minimum(i + 1, nb - 1), h)),
            pl.BlockSpec((1, t, LANES), lambda bi, h, i: (bi, 0, hp + h)),
            pl.BlockSpec((1, t, LANES), lambda bi, h, i: (bi, 0, 2 * hp + h)),
            pl.BlockSpec((1, n, LANES), lambda bi, h, i: (bi, 0, hp + h)),
            pl.BlockSpec((1, n, LANES), lambda bi, h, i: (bi, 0, 2 * hp + h)),
            pl.BlockSpec((1, 2, tq, NA_KEY_ROWS * GRID_W), lambda bi, h, i: (variant(i), h, 0, 0)),
        ],
        out_specs=pl.BlockSpec((1, tq, LANES), lambda bi, h, i: (bi, i, h)),
        scratch_shapes=[
            pltpu.VMEM((LANES // HEAD_DIM, tq, NA_KEY_ROWS * GRID_W), F32),
            pltpu.VMEM((LANES // HEAD_DIM, tq, n), F32),
        ],
        compiler_params=_params("parallel", "parallel", "arbitrary"),
        name="na_attention",
    )(p_lat, p_lat, p_lat, p_lat, p_ctx, p_ctx, bias_tbl)


def _ctx_attn_kernel(q_ref, k_ref, v_ref, o_ref):
    q, k, v = q_ref[0], k_ref[0], v_ref[0]
    scale = HEAD_DIM ** -0.5
    lane = lax.broadcasted_iota(jnp.int32, q.shape, 1)
    outs = []
    for hh in range(LANES // HEAD_DIM):
        own = (lane >= hh * HEAD_DIM) & (lane < (hh + 1) * HEAD_DIM)
        s = _nt_dot(jnp.where(own, q, 0.0).astype(BF16), k) * scale
        p = jnp.exp(s - jnp.max(s, axis=-1, keepdims=True))
        l = jnp.sum(p, axis=-1, keepdims=True)
        outs.append(jnp.dot(p.astype(BF16), v, preferred_element_type=F32) / l)
    o_ref[0] = jnp.where(lane < HEAD_DIM, outs[0], outs[1]).astype(BF16)


def _ctx_attention(p_ctx, hp):
    b, n, _ = p_ctx.shape
    return pl.pallas_call(
        _ctx_attn_kernel,
        out_shape=jax.ShapeDtypeStruct((b, n, hp * LANES), BF16),
        grid=(b, hp),
        in_specs=[
            pl.BlockSpec((1, n, LANES), lambda bi, h: (bi, 0, h)),
            pl.BlockSpec((1, n, LANES), lambda bi, h: (bi, 0, hp + h)),
            pl.BlockSpec((1, n, LANES), lambda bi, h: (bi, 0, 2 * hp + h)),
        ],
        out_specs=pl.BlockSpec((1, n, LANES), lambda bi, h: (bi, 0, h)),
        compiler_params=_params("parallel", "parallel"),
        name="ctx_attention",
    )(p_ctx, p_ctx, p_ctx)


def _gla_kernel(*refs, backward, nc):
    if backward:
        (q_ref, k_ref, v_ref, a_ref, wg_ref, bg_ref, s0_ref, of_ref, g_ref, gn_ref,
         o_ref, sout_ref, s_scr) = refs
    else:
        q_ref, k_ref, v_ref, a_ref, wg_ref, bg_ref, s0_ref, o_ref, sout_ref, s_scr = refs
    j = pl.program_id(1)
    tb = nc * GLA_CHUNK
    kw = GLA_HEADS * GLA_DK

    @pl.when(j == 0)
    def _():
        s_scr[...] = s0_ref[0]

    pre = jnp.dot(a_ref[0], wg_ref[...], preferred_element_type=F32) + bg_ref[...]
    log_a = (jnp.minimum(pre, 0.0) - jnp.log1p(jnp.exp(-jnp.abs(pre)))) * (1.0 / GLA_NORMALIZER)

    pos = lax.broadcasted_iota(jnp.int32, (tb, kw), 0) % GLA_CHUNK
    bsum = log_a
    step = 1
    while step < GLA_CHUNK:
        if backward:
            bsum = bsum + jnp.where(pos < GLA_CHUNK - step, pltpu.roll(bsum, tb - step, 0), 0.0)
        else:
            bsum = bsum + jnp.where(pos >= step, pltpu.roll(bsum, step, 0), 0.0)
        step *= 2

    b3 = bsum.reshape(nc, GLA_CHUNK, kw)
    b_end = b3[:, 0:1, :] if backward else b3[:, GLA_CHUNK - 1:GLA_CHUNK, :]
    q3 = q_ref[0].astype(F32).reshape(nc, GLA_CHUNK, kw)
    k3 = k_ref[0].astype(F32).reshape(nc, GLA_CHUNK, kw)
    q_in = q3 * (GLA_DK ** -0.5) * jnp.exp(b3)
    k_out = (k3 * jnp.exp(-b3)).astype(BF16)
    k_dec = (k3 * jnp.exp(b_end - b3)).astype(BF16)
    decay = jnp.exp(b_end)
    v3 = v_ref[0].reshape(nc, GLA_CHUNK, GLA_HEADS * GLA_DV)

    t_idx = lax.broadcasted_iota(jnp.int32, (nc, GLA_CHUNK, GLA_CHUNK), 1)
    s_idx = lax.broadcasted_iota(jnp.int32, (nc, GLA_CHUNK, GLA_CHUNK), 2)
    tri = (s_idx > t_idx) if backward else (s_idx <= t_idx)
    eye = (lax.broadcasted_iota(jnp.int32, (nc, GLA_DV, GLA_DV), 1)
           == lax.broadcasted_iota(jnp.int32, (nc, GLA_DV, GLA_DV), 2)).astype(BF16)
    lane = lax.broadcasted_iota(jnp.int32, (nc, GLA_CHUNK, LANES), 2)
    order = range(nc - 1, -1, -1) if backward else range(nc)

    outs = []
    for h in range(GLA_HEADS):
        pair = slice((h // 2) * LANES, (h // 2 + 1) * LANES)
        own = (lane >= GLA_DK) if h % 2 else (lane < GLA_DK)
        qm = jnp.where(own, q_in[:, :, pair], 0.0).astype(BF16)
        vh = v3[:, :, h * GLA_DV:(h + 1) * GLA_DV]
        att = jnp.einsum('ctd,csd->cts', qm, k_out[:, :, pair], preferred_element_type=F32)
        att = jnp.where(tri, att, 0.0).astype(BF16)
        o_h = jnp.einsum('cts,csv->ctv', att, vh, preferred_element_type=F32)
        vt = jnp.einsum('cvw,csw->cvs', eye, vh, preferred_element_type=F32).astype(BF16)
        ut = jnp.einsum('cvs,csd->cvd', vt, k_dec[:, :, pair], preferred_element_type=F32)
        s_h = s_scr[h]
        before = [None] * nc
        for c in order:
            before[c] = s_h
            s_h = s_h * decay[c, :, pair] + ut[c]
        s_scr[h] = s_h
        s_all = jnp.stack(before, axis=0).astype(BF16)
        o_h = o_h + jnp.einsum('ctd,cvd->ctv', qm, s_all, preferred_element_type=F32)
        outs.append(o_h.reshape(tb, GLA_DV))

    @pl.when(j == pl.num_programs(1) - 1)
    def _():
        sout_ref[0] = s_scr[...]

    if not backward:
        o_ref[0] = jnp.concatenate(outs, axis=-1)
        return
    of = of_ref[0]
    g = g_ref[0].astype(F32)
    fin = []
    for h in range(GLA_HEADS):
        sl = slice(h * GLA_DV, (h + 1) * GLA_DV)
        y = _rms(of[:, sl] + outs[h]) * gn_ref[...]
        gh = g[:, sl]
        fin.append(y * (gh * jax.nn.sigmoid(gh)))
    o_ref[0] = jnp.concatenate(fin, axis=-1).astype(BF16)


def _gla_sweep(p, wg, bg, s0, nc, col0, a_col, backward, o_fwd=None, g_norm=None):
    b, n, _ = p.shape
    tb = nc * GLA_CHUNK
    nblk = n // tb
    kw = GLA_HEADS * GLA_DK
    vw = GLA_HEADS * GLA_DV
    cq, ck, cv, cg, ca = col0 // kw, col0 // kw + 1, (col0 + 2 * kw) // vw, (col0 + 2 * kw) // vw + 1, a_col // LANES

    def blk(j):
        return nblk - 1 - j if backward else j

    in_specs = [
        pl.BlockSpec((1, tb, kw), lambda bi, j: (bi, blk(j), cq)),
        pl.BlockSpec((1, tb, kw), lambda bi, j: (bi, blk(j), ck)),
        pl.BlockSpec((1, tb, vw), lambda bi, j: (bi, blk(j), cv)),
        pl.BlockSpec((1, tb, LANES), lambda bi, j: (bi, blk(j), ca)),
        pl.BlockSpec((LANES, kw), lambda bi, j: (0, 0)),
        pl.BlockSpec((1, kw), lambda bi, j: (0, 0)),
        pl.BlockSpec((1, GLA_HEADS, GLA_DV, LANES), lambda bi, j: (bi, 0, 0, 0)),
    ]
    args = [p, p, p, p, wg, bg, s0]
    if backward:
        in_specs += [
            pl.BlockSpec((1, tb, vw), lambda bi, j: (bi, blk(j), 0)),
            pl.BlockSpec((1, tb, vw), lambda bi, j: (bi, blk(j), cg)),
            pl.BlockSpec((1, GLA_DV), lambda bi, j: (0, 0)),
        ]
        args += [o_fwd, p, g_norm]
    return pl.pallas_call(
        functools.partial(_gla_kernel, backward=backward, nc=nc),
        out_shape=(jax.ShapeDtypeStruct((b, n, vw), BF16 if backward else F32),
                   jax.ShapeDtypeStruct((b, GLA_HEADS, GLA_DV, LANES), F32)),
        grid=(b, nblk),
        in_specs=in_specs,
        out_specs=(pl.BlockSpec((1, tb, vw), lambda bi, j: (bi, blk(j), 0)),
                   pl.BlockSpec((1, GLA_HEADS, GLA_DV, LANES), lambda bi, j: (bi, 0, 0, 0))),
        scratch_shapes=[pltpu.VMEM((GLA_HEADS, GLA_DV, LANES), F32)],
        compiler_params=_params("parallel", "arbitrary"),
        name="gla_bwd" if backward else "gla_fwd",
    )(*args)


def _diff_lambda(lam_ref, lam_init):
    lp = lam_ref[...]
    return (jnp.exp(jnp.sum(lp[0:1] * lp[1:2], axis=-1, keepdims=True))
            - jnp.exp(jnp.sum(lp[2:3] * lp[3:4], axis=-1, keepdims=True)) + lam_init)


def _branch_queries(q):
    lane = lax.broadcasted_iota(jnp.int32, q.shape, 1)
    return jnp.concatenate([jnp.where(lane < DIFF_DK, q, 0.0), jnp.where(lane >= DIFF_DK, q, 0.0)], axis=0)


def _diff_ctx_kernel(q_ref, k_ref, v_ref, lam_ref, gn_ref, o_ref, *, lam_init):
    tq = q_ref.shape[1]
    s = _nt_dot(_branch_queries(q_ref[0]), k_ref[0])
    p = jnp.exp2(s - jnp.max(s, axis=-1, keepdims=True))
    a = jnp.dot(p.astype(BF16), v_ref[0], preferred_element_type=F32) / jnp.sum(p, axis=-1, keepdims=True)
    o = a[0:tq] - _diff_lambda(lam_ref, lam_init) * a[tq:2 * tq]
    o_ref[0] = (_rms(o) * gn_ref[...] * (1.0 - lam_init)).astype(BF16)


def _diff_attention_ctx(qk_c, p_c, lam_params, g_norm, lam_init, v_col0):
    b, n, w = qk_c.shape
    heads = w // (2 * LANES)
    vb = v_col0 // LANES
    return pl.pallas_call(
        functools.partial(_diff_ctx_kernel, lam_init=lam_init),
        out_shape=jax.ShapeDtypeStruct((b, n, heads * DIFF_DV), BF16),
        grid=(b, heads),
        in_specs=[
            pl.BlockSpec((1, n, LANES), lambda bi, h: (bi, 0, h)),
            pl.BlockSpec((1, n, LANES), lambda bi, h: (bi, 0, heads + h)),
            pl.BlockSpec((1, n, LANES), lambda bi, h: (bi, 0, vb + h)),
            pl.BlockSpec((4, DIFF_DK), lambda bi, h: (0, 0)),
            pl.BlockSpec((1, DIFF_DV), lambda bi, h: (0, 0)),
        ],
        out_specs=pl.BlockSpec((1, n, LANES), lambda bi, h: (bi, 0, h)),
        compiler_params=_params("parallel", "parallel"),
        name="diff_attention_ctx",
    )(qk_c, qk_c, p_c, lam_params, g_norm)


def _diff_kernel(q_ref, k_ref, v_ref, kc_ref, vc_ref, lam_ref, gn_ref, o_ref,
                 qs, vt, vct, s_buf, c_buf, p_buf, a_buf, m_scr, acc, *, nch, lam_init):
    tq = q_ref.shape[1]
    tkc = k_ref.shape[1] // nch

    def transposed_with_ones(v):
        pad_rows = lax.broadcasted_iota(jnp.int32, (DIFF_VT_ROWS - DIFF_DV, v.shape[0]), 0)
        return jnp.concatenate([v.astype(F32).T.astype(BF16), jnp.where(pad_rows == 0, 1.0, 0.0).astype(BF16)],
                               axis=0)

    @pl.when(pl.program_id(2) == 0)
    def _():
        for c in range(nch):
            vt[c] = transposed_with_ones(v_ref[0, c * tkc:(c + 1) * tkc, :])
        vct[...] = transposed_with_ones(vc_ref[0])

    def stage(value_chunk, softmax_on, score_chunk):
        for n in range(2 * tq // MXU_N):
            sl = slice(n * MXU_N, (n + 1) * MXU_N)
            if value_chunk is not None:
                pv = jnp.dot(vt[value_chunk], p_buf[:, sl], preferred_element_type=F32)
                acc[:, sl] = acc[:, sl] * a_buf[:, sl] + pv
            if softmax_on:
                m_old = m_scr[:, sl]
                m_new = jnp.maximum(m_old, c_buf[:, sl])
                a_buf[:, sl] = jnp.exp2(m_old - m_new)
                m_scr[:, sl] = m_new
                p_buf[:, sl] = jnp.exp2(s_buf[:, sl] - m_new).astype(BF16)
            if score_chunk is not None:
                k = k_ref[0, pl.ds(pl.multiple_of(score_chunk * tkc, tkc), tkc), :]
                s = _nt_dot(k, qs[sl, :])
                s_buf[:, sl] = s
                c_buf[:, sl] = jnp.max(s, axis=0, keepdims=True)

    qs[...] = _branch_queries(q_ref[0])
    sc = _nt_dot(kc_ref[0], qs[...])
    mc = jnp.max(sc, axis=0, keepdims=True)
    pc = jnp.exp2(sc - mc)
    m_scr[...] = mc
    acc[...] = jnp.dot(vct[...], pc.astype(BF16), preferred_element_type=F32)
    stage(None, False, 0)
    stage(None, True, 1)

    def steady(c, carry):
        stage(c - 1, True, c + 1)
        return carry

    lax.fori_loop(1, nch - 1, steady, 0, unroll=2 if nch % 2 == 0 else 1)
    stage(nch - 2, True, None)
    stage(nch - 1, False, None)

    a = acc[0:DIFF_DV, :] / acc[DIFF_DV:DIFF_DV + 1, :]
    o = a[:, 0:tq] - _diff_lambda(lam_ref, lam_init) * a[:, tq:2 * tq]
    y = o * lax.rsqrt(jnp.mean(o * o, axis=0, keepdims=True) + EPS) * gn_ref[...] * (1.0 - lam_init)
    o_ref[0] = y.T.astype(BF16)


def _diff_attention(qk_x, p_x, qk_c, p_c, lam_params, g_norm, lam_init, tq, tkc, v_col0):
    b, t, w = qk_x.shape
    heads = w // (2 * LANES)
    n_ctx = qk_c.shape[1]
    nch = t // tkc
    vb = v_col0 // LANES
    assert nch >= 2 and nch * tkc == t
    return pl.pallas_call(
        functools.partial(_diff_kernel, nch=nch, lam_init=lam_init),
        out_shape=jax.ShapeDtypeStruct((b, t, heads * DIFF_DV), BF16),
        grid=(b, heads, t // tq),
        in_specs=[
            pl.BlockSpec((1, tq, LANES), lambda bi, h, i: (bi, i, h)),
            pl.BlockSpec((1, t, LANES), lambda bi, h, i: (bi, 0, heads + h)),
            pl.BlockSpec((1, t, LANES), lambda bi, h, i: (bi, 0, vb + h)),
            pl.BlockSpec((1, n_ctx, LANES), lambda bi, h, i: (bi, 0, heads + h)),
            pl.BlockSpec((1, n_ctx, LANES), lambda bi, h, i: (bi, 0, vb + h)),
            pl.BlockSpec((4, DIFF_DK), lambda bi, h, i: (0, 0)),
            pl.BlockSpec((DIFF_DV, 1), lambda bi, h, i: (0, 0)),
        ],
        out_specs=pl.BlockSpec((1, tq, LANES), lambda bi, h, i: (bi, i, h)),
        scratch_shapes=[
            pltpu.VMEM((2 * tq, LANES), BF16),
            pltpu.VMEM((nch, DIFF_VT_ROWS, tkc), BF16),
            pltpu.VMEM((DIFF_VT_ROWS, n_ctx), BF16),
            pltpu.VMEM((tkc, 2 * tq), F32),
            pltpu.VMEM((1, 2 * tq), F32),
            pltpu.VMEM((tkc, 2 * tq), BF16),
            pltpu.VMEM((1, 2 * tq), F32),
            pltpu.VMEM((1, 2 * tq), F32),
            pltpu.VMEM((DIFF_VT_ROWS, 2 * tq), F32),
        ],
        compiler_params=_params("parallel", "parallel", "arbitrary"),
        name="diff_attention",
    )(qk_x, qk_x, p_x, qk_c, p_c, lam_params, g_norm.reshape(DIFF_DV, 1))


def _outproj_kernel(ya_ref, yb_ref, yd_ref, w_ref, x_ref, g_ref, gate_ref, gmlp_ref, shift_ref, scale_ref,
                    o_ref, h_ref):
    wa, wb = ya_ref.shape[2], yb_ref.shape[2]
    for r in range(x_ref.shape[1] // ROW_GROUP):
        rows = slice(r * ROW_GROUP, (r + 1) * ROW_GROUP)
        y = jnp.dot(ya_ref[0, rows, :], w_ref[0:wa, :], preferred_element_type=F32)
        y = y + jnp.dot(yb_ref[0, rows, :], w_ref[wa:wa + wb, :], preferred_element_type=F32)
        y = y + jnp.dot(yd_ref[0, rows, :], w_ref[wa + wb:, :], preferred_element_type=F32)
        x_new = x_ref[0, rows, :] + gate_ref[0] * (_rms(y) * g_ref[...])
        o_ref[0, rows, :] = x_new
        h_ref[0, rows, :] = _modulate(x_new, gmlp_ref, shift_ref, scale_ref)


def _outproj(ya, yb, yd, w, x, g_post, g_mlp, mods, mod_row0, tm):
    b, t, d = x.shape

    def mod(comp):
        return pl.BlockSpec((1, 1, d), lambda bi, i: ((bi + mod_row0) * 6 + comp, 0, 0))

    row = pl.BlockSpec((1, d), lambda bi, i: (0, 0))
    tile = pl.BlockSpec((1, tm, d), lambda bi, i: (bi, i, 0))
    return pl.pallas_call(
        _outproj_kernel,
        out_shape=(jax.ShapeDtypeStruct((b, t, d), F32), jax.ShapeDtypeStruct((b, t, d), BF16)),
        grid=(b, t // tm),
        in_specs=[
            pl.BlockSpec((1, tm, ya.shape[2]), lambda bi, i: (bi, i, 0)),
            pl.BlockSpec((1, tm, yb.shape[2]), lambda bi, i: (bi, i, 0)),
            pl.BlockSpec((1, tm, yd.shape[2]), lambda bi, i: (bi, i, 0)),
            pl.BlockSpec(w.shape, lambda bi, i: (0, 0)),
            tile, row, mod(2), row, mod(3), mod(4),
        ],
        out_specs=(tile, tile),
        compiler_params=_params("parallel", "parallel"),
        name="outproj",
    )(ya, yb, yd, w, x, g_post, mods, g_mlp, mods, mods)


def _mlp_kernel(x_ref, h_ref, w1_ref, w2_ref, gpost_ref, gate_ref, *rest, emit_next):
    if emit_next:
        gnext_ref, shift_ref, scale_ref, o_ref, hnext_ref, acc = rest
    else:
        o_ref, acc = rest
    k = pl.program_id(2)

    @pl.when(k == 0)
    def _():
        acc[...] = jnp.zeros(acc.shape, F32)

    def hidden_tile(rows):
        u = jnp.maximum(jnp.dot(h_ref[0, rows, :], w1_ref[...], preferred_element_type=F32), 0.0)
        return jnp.dot((u * u).astype(BF16), w2_ref[...], preferred_element_type=F32)

    last = pl.num_programs(2) - 1

    @pl.when(k < last)
    def _():
        acc[...] += hidden_tile(slice(None))

    @pl.when(k == last)
    def _():
        for r in range(acc.shape[0] // ROW_GROUP):
            rows = slice(r * ROW_GROUP, (r + 1) * ROW_GROUP)
            y = acc[rows, :] + hidden_tile(rows)
            x_new = x_ref[0, rows, :] + gate_ref[0] * (_rms(y) * gpost_ref[...])
            o_ref[0, rows, :] = x_new
            if emit_next:
                hnext_ref[0, rows, :] = _modulate(x_new, gnext_ref, shift_ref, scale_ref)


def _mlp(x, h, gpost, mods, mod_row0, w1, w2, tm, th, g_next=None, mods_next=None):
    b, t, d = x.shape
    hid = w1.shape[1]
    emit_next = g_next is not None

    def mod(comp):
        return pl.BlockSpec((1, 1, d), lambda bi, i, k: ((bi + mod_row0) * 6 + comp, 0, 0))

    row = pl.BlockSpec((1, d), lambda bi, i, k: (0, 0))
    tile = pl.BlockSpec((1, tm, d), lambda bi, i, k: (bi, i, 0))
    in_specs = [
        tile, tile,
        pl.BlockSpec((d, th), lambda bi, i, k: (0, k)),
        pl.BlockSpec((th, d), lambda bi, i, k: (k, 0)),
        row, mod(5),
    ]
    args = [x, h, w1, w2, gpost, mods]
    out_shape = [jax.ShapeDtypeStruct((b, t, d), F32)]
    out_specs = [tile]
    if emit_next:
        in_specs += [row, mod(0), mod(1)]
        args += [g_next, mods_next, mods_next]
        out_shape.append(jax.ShapeDtypeStruct((b, t, d), BF16))
        out_specs.append(tile)
    out = pl.pallas_call(
        functools.partial(_mlp_kernel, emit_next=emit_next),
        out_shape=tuple(out_shape),
        grid=(b, t // tm, hid // th),
        in_specs=in_specs,
        out_specs=tuple(out_specs),
        scratch_shapes=[pltpu.VMEM((tm, d), F32)],
        compiler_params=_params("parallel", "parallel", "arbitrary"),
        name="mlp",
    )(*args)
    return out if emit_next else (out[0], None)


def _rope_tables(t):
    tok = jnp.arange(t, dtype=jnp.int32)
    row = (tok // GRID_W).astype(F32)
    col = (tok % GRID_W).astype(F32)
    n_freq = DIFF_DK // 4
    inv = ROPE_BASE ** (-jnp.arange(n_freq, dtype=F32) / n_freq)
    ang = jnp.concatenate([row[:, None] * inv, col[:, None] * inv], axis=-1)
    cos, sin = jnp.cos(ang), jnp.sin(ang)
    reps = LANES // DIFF_DK
    return (jnp.tile(jnp.concatenate([cos, cos], axis=-1), (1, reps)),
            jnp.tile(jnp.concatenate([-sin, sin], axis=-1), (1, reps)))


def _pick(n, pref):
    return pref if n % pref == 0 else n


def kernel(x, c, ctx, c_ctx, w_ada, b_ada, g_pre_mix, g_post_mix, g_pre_mlp, g_post_mlp, w_in, w_out,
           na_rel_bias, gla_w_gate, gla_b_gate, gla_g_norm, diff_lambda, diff_g_norm, w_mlp_in, w_mlp_out):
    bsz, t, d = x.shape
    n_ctx = ctx.shape[1]
    depth = w_ada.shape[0]
    na_heads = na_rel_bias.shape[1]
    na_w = na_heads * HEAD_DIM
    gla_kw = GLA_HEADS * GLA_DK
    gla_vw = GLA_HEADS * GLA_DV
    diff_w = w_out.shape[1] - na_w - gla_vw
    diff_heads = diff_w // DIFF_DV
    diff_qk = diff_heads * 2 * DIFF_DK
    rows = t // GRID_W
    assert t % (NA_ROWS_PER_STEP * GRID_W) == 0 and rows >= NA_KEY_ROWS
    assert bsz + 1 <= 8

    o_gla = 3 * na_w
    o_gate = o_gla + 2 * gla_kw + 2 * gla_vw
    o_diff = o_gate + 2 * GLA_GATE_RANK
    gla_col0 = 3 * na_w
    diffv_col0 = gla_col0 + 2 * gla_kw + 2 * gla_vw
    gate_col0 = diffv_col0 + diff_w

    cc = jnp.zeros((8, d), F32).at[:bsz].set(c).at[bsz].set(c_ctx)
    mods_all = _mods(cc, w_ada, b_ada)
    rope_tabs = _rope_tables(t)
    xc = ctx.reshape(1, bsz * n_ctx, d)
    zeros_state = jnp.zeros((bsz, GLA_HEADS, GLA_DV, LANES), F32)

    tm_in = _pick(t, 1024)
    tm_out = _pick(t, 512)
    tm_ctx = _pick(bsz * n_ctx, 512)
    tq = _pick(t, 1024)
    tkc = min(1024, t // 2)
    nc_lat = _pick(t // GLA_CHUNK, 16)
    nc_ctx = _pick(n_ctx // GLA_CHUNK, 4)

    mods_of = [mods_all[l].reshape(8 * 6, 1, d) for l in range(depth)]
    h_x = _premix(x, g_pre_mix[0].reshape(1, d), mods_of[0], 0, tm_out)
    h_c = _premix(xc, g_pre_mix[0].reshape(1, d), mods_of[0], bsz, tm_ctx)

    for l in range(depth):
        need_ctx = l < depth - 1
        mods = mods_of[l]
        g_next = g_pre_mix[l + 1].reshape(1, d) if need_ctx else None
        mods_next = mods_of[l + 1] if need_ctx else None
        wl = w_in[l].astype(BF16)
        w_qk = wl[:, o_diff:o_diff + 2 * diff_qk]
        w_rest = jnp.concatenate(
            [wl[:, :o_gate], wl[:, o_diff + 2 * diff_qk:], wl[:, o_gate:o_diff],
             jnp.zeros((d, LANES - 2 * GLA_GATE_RANK), BF16)], axis=1)
        w_o = w_out[l].astype(BF16)
        w1 = w_mlp_in[l].astype(BF16)
        w2 = w_mlp_out[l].astype(BF16)
        g2 = g_post_mix[l].reshape(1, d)
        g3 = g_pre_mlp[l].reshape(1, d)
        g4 = g_post_mlp[l].reshape(1, d)
        wg = jnp.zeros((2, LANES, gla_kw), F32)
        wg = wg.at[0, :GLA_GATE_RANK].set(gla_w_gate[l, 0]).at[1, GLA_GATE_RANK:2 * GLA_GATE_RANK].set(gla_w_gate[l, 1])
        wg = wg.astype(BF16)
        bg = gla_b_gate[l].reshape(2, 1, gla_kw)
        gla_gn = gla_g_norm[l].reshape(1, GLA_DV)
        diff_gn = diff_g_norm[l].reshape(1, DIFF_DV)
        lam_init = 0.8 - 0.6 * math.exp(-0.3 * l)

        qk_x = _inproj(h_x, w_qk, tm_in, diff_qk, rope_tabs, 1, DIFF_Q_SCALE)
        p_x = _inproj(h_x, w_rest, tm_in, w_rest.shape[1] // 3)
        qk_c = _inproj(h_c, w_qk, tm_ctx, diff_qk, None, 1, DIFF_Q_SCALE).reshape(bsz, n_ctx, -1)
        p_c = _inproj(h_c, w_rest, tm_ctx, w_rest.shape[1] // 3).reshape(bsz, n_ctx, -1)

        ya = _na_attention(p_x, p_c, _na_bias_table(na_rel_bias[l], rows))

        of_c, s_f = _gla_sweep(p_c, wg[0], bg[0], zeros_state, nc_ctx, gla_col0, gate_col0, False)
        yb_c, s_b = _gla_sweep(p_c, wg[1], bg[1], zeros_state, nc_ctx, gla_col0, gate_col0, True, of_c, gla_gn)
        of_x, _ = _gla_sweep(p_x, wg[0], bg[0], s_f, nc_lat, gla_col0, gate_col0, False)
        yb, _ = _gla_sweep(p_x, wg[1], bg[1], s_b, nc_lat, gla_col0, gate_col0, True, of_x, gla_gn)

        yd = _diff_attention(qk_x, p_x, qk_c, p_c, diff_lambda[l], diff_gn, lam_init, tq, tkc, diffv_col0)

        x, hm_x = _outproj(ya, yb, yd, w_o, x, g2, g3, mods, 0, tm_out)
        x, h_x = _mlp(x, hm_x, g4, mods, 0, w1, w2, tm_out, MLP_HIDDEN_TILE, g_next, mods_next)

        if need_ctx:
            ya_c = _ctx_attention(p_c, na_heads // 2)
            yd_c = _diff_attention_ctx(qk_c, p_c, diff_lambda[l], diff_gn, lam_init, diffv_col0)
            flat = lambda z: z.reshape(1, bsz * n_ctx, -1)
            xc, hm_c = _outproj(flat(ya_c), flat(yb_c), flat(yd_c), w_o, xc, g2, g3, mods, bsz, tm_ctx)
            xc, h_c = _mlp(xc, hm_c, g4, mods, bsz, w1, w2, tm_ctx, MLP_HIDDEN_TILE, g_next, mods_next)
    return x
```

```python
import functools
import math

import jax
import jax.numpy as jnp
from jax import lax
from jax.experimental import pallas as pl
from jax.experimental.pallas import tpu as pltpu

F32 = jnp.float32
BF16 = jnp.bfloat16

GRID_W = 64
EPS = 1e-6
NEG_INF = -1e30
HEAD_DIM = 64
NA_WIN_R = 8
NA_WIN_C = 16
GLA_HEADS = 4
GLA_DK = 64
GLA_DV = 128
GLA_GATE_RANK = 16
GLA_NORMALIZER = 16.0
GLA_CHUNK = 64
DIFF_DK = 64
DIFF_DV = 128
ROPE_BASE = 10000.0
DIFF_Q_SCALE = DIFF_DK ** -0.5 * math.log2(math.e)

LANES = 128
DIFF_LANE_TILE = 512
DIFF_VT_ROWS = 128 + 16
VMEM_LIMIT = 56 * 1024 * 1024

MLP_HIDDEN_TILE = 1024
ROW_GROUP = 256
NA_ROWS_PER_STEP = 8
NA_KEY_ROWS = NA_ROWS_PER_STEP + NA_WIN_R


def _params(*sem):
    return pltpu.CompilerParams(dimension_semantics=sem, vmem_limit_bytes=VMEM_LIMIT)


def _nt_dot(a, b):
    return lax.dot_general(a, b, (((1,), (1,)), ((), ())), preferred_element_type=F32)


def _rms(y):
    return y * lax.rsqrt(jnp.mean(y * y, axis=-1, keepdims=True) + EPS)


def _mods_kernel(c_ref, w_ref, b_ref, o_ref):
    c = c_ref[...]
    s = (c * jax.nn.sigmoid(c)).astype(BF16)
    o_ref[0] = jnp.dot(s, w_ref[0].astype(BF16), preferred_element_type=F32) + b_ref[0]


def _mods(cc, w_ada, b_ada):
    depth, d, n = w_ada.shape
    tn = 1024
    return pl.pallas_call(
        _mods_kernel,
        out_shape=jax.ShapeDtypeStruct((depth, 8, n), F32),
        grid=(depth, n // tn),
        in_specs=[
            pl.BlockSpec((8, d), lambda l, j: (0, 0)),
            pl.BlockSpec((1, d, tn), lambda l, j: (l, 0, j)),
            pl.BlockSpec((1, 1, tn), lambda l, j: (l, 0, j)),
        ],
        out_specs=pl.BlockSpec((1, 8, tn), lambda l, j: (l, 0, j)),
        compiler_params=_params("arbitrary", "arbitrary"),
        name="ada_mods",
    )(cc, w_ada, b_ada.reshape(depth, 1, n))


def _modulate(x, g_ref, shift_ref, scale_ref):
    return (_rms(x) * g_ref[...] * (1.0 + scale_ref[0]) + shift_ref[0]).astype(BF16)


def _premix_kernel(x_ref, g_ref, shift_ref, scale_ref, o_ref):
    o_ref[0] = _modulate(x_ref[0], g_ref, shift_ref, scale_ref)


def _premix(x, g, mods, mod_row0, tm):
    b, t, d = x.shape
    return pl.pallas_call(
        _premix_kernel,
        out_shape=jax.ShapeDtypeStruct((b, t, d), BF16),
        grid=(b, t // tm),
        in_specs=[
            pl.BlockSpec((1, tm, d), lambda bi, i: (bi, i, 0)),
            pl.BlockSpec((1, d), lambda bi, i: (0, 0)),
            pl.BlockSpec((1, 1, d), lambda bi, i: ((bi + mod_row0) * 6 + 0, 0, 0)),
            pl.BlockSpec((1, 1, d), lambda bi, i: ((bi + mod_row0) * 6 + 1, 0, 0)),
        ],
        out_specs=pl.BlockSpec((1, tm, d), lambda bi, i: (bi, i, 0)),
        compiler_params=_params("parallel", "parallel"),
        name="premix",
    )(x, g, mods, mods)


def _inproj_kernel(h_ref, w_ref, *rest, rope, q_tiles, q_scale):
    if rope:
        cos_ref, sin_ref, o_ref = rest
    else:
        (o_ref,) = rest
    for r in range(h_ref.shape[1] // ROW_GROUP):
        rows = slice(r * ROW_GROUP, (r + 1) * ROW_GROUP)
        acc = jnp.dot(h_ref[0, rows, :], w_ref[...], preferred_element_type=F32)
        if q_tiles:
            acc = acc * jnp.where(pl.program_id(2) < q_tiles, q_scale, 1.0)
        if not rope:
            o_ref[0, rows, :] = acc.astype(BF16)
            continue
        cos = cos_ref[rows, :]
        sin = sin_ref[rows, :]
        lower = (lax.broadcasted_iota(jnp.int32, cos.shape, 1) % DIFF_DK) < DIFF_DK // 2
        for c in range(acc.shape[1] // LANES):
            a = acc[:, c * LANES:(c + 1) * LANES]
            partner = jnp.where(lower, pltpu.roll(a, LANES - DIFF_DK // 2, 1), pltpu.roll(a, DIFF_DK // 2, 1))
            o_ref[0, rows, c * LANES:(c + 1) * LANES] = (a * cos + partner * sin).astype(BF16)


def _inproj(h, w, tm, tn, rope_tabs=None, q_tiles=0, q_scale=1.0):
    b, t, d = h.shape
    n = w.shape[1]
    rope = rope_tabs is not None
    in_specs = [
        pl.BlockSpec((1, tm, d), lambda bi, i, j: (bi, i, 0)),
        pl.BlockSpec((d, tn), lambda bi, i, j: (0, j)),
    ]
    args = [h, w]
    if rope:
        in_specs += [pl.BlockSpec((tm, LANES), lambda bi, i, j: (i, 0))] * 2
        args += list(rope_tabs)
    return pl.pallas_call(
        functools.partial(_inproj_kernel, rope=rope, q_tiles=q_tiles, q_scale=q_scale),
        out_shape=jax.ShapeDtypeStruct((b, t, n), BF16),
        grid=(b, t // tm, n // tn),
        in_specs=in_specs,
        out_specs=pl.BlockSpec((1, tm, tn), lambda bi, i, j: (bi, i, j)),
        compiler_params=_params("parallel", "parallel", "parallel"),
        name="inproj_rope" if rope else "inproj",
    )(*args)


def _na_bias_table(rel_bias, rows):
    rq, kwin = NA_ROWS_PER_STEP, NA_KEY_ROWS
    nb = rows // rq
    h = rel_bias.shape[0]
    blocks = jnp.array([0, min(1, nb - 1), nb - 1], jnp.int32)
    r = blocks[:, None] * rq + jnp.arange(rq, dtype=jnp.int32)[None, :]
    kst = jnp.clip(blocks * rq - NA_WIN_R // 2, 0, rows - kwin)
    kr = kst[:, None] + jnp.arange(kwin, dtype=jnp.int32)[None, :]
    s_r = jnp.clip(r - NA_WIN_R // 2, 0, rows - NA_WIN_R)
    row_ok = (kr[:, None, :] >= s_r[:, :, None]) & (kr[:, None, :] < s_r[:, :, None] + NA_WIN_R)
    dr = jnp.clip(kr[:, None, :] - r[:, :, None] + (NA_WIN_R - 1), 0, 2 * NA_WIN_R - 2)
    col = jnp.arange(GRID_W, dtype=jnp.int32)
    c0 = jnp.clip(col - NA_WIN_C // 2, 0, GRID_W - NA_WIN_C)
    col_ok = (col[None, :] >= c0[:, None]) & (col[None, :] < c0[:, None] + NA_WIN_C)
    dc = jnp.clip(col[None, :] - col[:, None] + (NA_WIN_C - 1), 0, 2 * NA_WIN_C - 2)
    pick_dc = (dc[:, :, None] == jnp.arange(2 * NA_WIN_C - 1)[None, None, :]).astype(F32)
    pick_dr = (dr[..., None] == jnp.arange(2 * NA_WIN_R - 1)[None, None, None, :]).astype(F32)
    by_col = jnp.einsum('hrd,qkd->hrqk', rel_bias.astype(F32), pick_dc, precision=lax.Precision.HIGHEST)
    bias = jnp.einsum('vqlr,hrab->vhqalb', pick_dr, by_col, precision=lax.Precision.HIGHEST)
    ok = row_ok[:, None, :, None, :, None] & col_ok[None, None, None, :, None, :]
    bias = jnp.where(ok, bias, NEG_INF)
    return bias.reshape(3, h, rq * GRID_W, kwin * GRID_W)


def _na_kernel(q_ref, qn_ref, k_ref, v_ref, kc_ref, vc_ref, bias_ref, o_ref, s_loc, s_ctx, *, rows):
    i = pl.program_id(2)
    window = NA_KEY_ROWS * GRID_W
    kc = kc_ref[0]
    vc = vc_ref[0]
    lane = lax.broadcasted_iota(jnp.int32, q_ref.shape[1:], 1)

    def window_start(step):
        kstart = jnp.clip(step * NA_ROWS_PER_STEP - NA_WIN_R // 2, 0, rows - NA_KEY_ROWS)
        return pl.multiple_of(kstart * GRID_W, GRID_W)

    def raw_scores(q, step, hh):
        own = (lane >= hh * HEAD_DIM) & (lane < (hh + 1) * HEAD_DIM)
        qh = jnp.where(own, q * (HEAD_DIM ** -0.5), 0.0).astype(BF16)
        s_loc[hh] = _nt_dot(qh, k_ref[0, pl.ds(window_start(step), window), :])
        s_ctx[hh] = _nt_dot(qh, kc)

    @pl.when(i == 0)
    def _():
        for hh in range(LANES // HEAD_DIM):
            raw_scores(q_ref[0], 0, hh)

    following = jnp.minimum(i + 1, pl.num_programs(2) - 1)
    vw = v_ref[0, pl.ds(window_start(i), window), :]
    outs = []
    for hh in range(LANES // HEAD_DIM):
        sl = s_loc[hh] + bias_ref[0, hh]
        sc = s_ctx[hh]
        m = jnp.maximum(jnp.max(sl, axis=-1, keepdims=True), jnp.max(sc, axis=-1, keepdims=True))
        p_loc = jnp.exp(sl - m)
        p_ctx = jnp.exp(sc - m)
        l = jnp.sum(p_loc, axis=-1, keepdims=True) + jnp.sum(p_ctx, axis=-1, keepdims=True)
        raw_scores(qn_ref[0], following, hh)
        o = jnp.dot(p_loc.astype(BF16), vw, preferred_element_type=F32)
        o = o + jnp.dot(p_ctx.astype(BF16), vc, preferred_element_type=F32)
        outs.append(o / l)
    o_ref[0] = jnp.where(lane < HEAD_DIM, outs[0], outs[1]).astype(BF16)


def _na_attention(p_lat, p_ctx, bias_tbl):
    b, t, _ = p_lat.shape
    n = p_ctx.shape[1]
    rows = t // GRID_W
    nb = rows // NA_ROWS_PER_STEP
    hp = bias_tbl.shape[1] // 2
    tq = NA_ROWS_PER_STEP * GRID_W

    def variant(i):
        return jnp.where(i == 0, 0, jnp.where(i == nb - 1, 2, 1))

    return pl.pallas_call(
        functools.partial(_na_kernel, rows=rows),
        out_shape=jax.ShapeDtypeStruct((b, t, hp * LANES), BF16),
        grid=(b, hp, nb),
        in_specs=[
            pl.BlockSpec((1, tq, LANES), lambda bi, h, i: (bi, i, h)),
            pl.BlockSpec((1, tq, LANES), lambda bi, h, i: (bi, jnp.minimum(i + 1, nb - 1), h)),
            pl.BlockSpec((1, t, LANES), lambda bi, h, i: (bi, 0, hp + h)),
            pl.BlockSpec((1, t, LANES), lambda bi, h, i: (bi, 0, 2 * hp + h)),
            pl.BlockSpec((1, n, LANES), lambda bi, h, i: (bi, 0, hp + h)),
            pl.BlockSpec((1, n, LANES), lambda bi, h, i: (bi, 0, 2 * hp + h)),
            pl.BlockSpec((1, 2, tq, NA_KEY_ROWS * GRID_W), lambda bi, h, i: (variant(i), h, 0, 0)),
        ],
        out_specs=pl.BlockSpec((1, tq, LANES), lambda bi, h, i: (bi, i, h)),
        scratch_shapes=[
            pltpu.VMEM((LANES // HEAD_DIM, tq, NA_KEY_ROWS * GRID_W), F32),
            pltpu.VMEM((LANES // HEAD_DIM, tq, n), F32),
        ],
        compiler_params=_params("parallel", "parallel", "arbitrary"),
        name="na_attention",
    )(p_lat, p_lat, p_lat, p_lat, p_ctx, p_ctx, bias_tbl)


def _ctx_attn_kernel(q_ref, k_ref, v_ref, o_ref):
    q, k, v = q_ref[0], k_ref[0], v_ref[0]
    scale = HEAD_DIM ** -0.5
    lane = lax.broadcasted_iota(jnp.int32, q.shape, 1)
    outs = []
    for hh in range(LANES // HEAD_DIM):
        own = (lane >= hh * HEAD_DIM) & (lane < (hh + 1) * HEAD_DIM)
        s = _nt_dot(jnp.where(own, q, 0.0).astype(BF16), k) * scale
        p = jnp.exp(s - jnp.max(s, axis=-1, keepdims=True))
        l = jnp.sum(p, axis=-1, keepdims=True)
        outs.append(jnp.dot(p.astype(BF16), v, preferred_element_type=F32) / l)
    o_ref[0] = jnp.where(lane < HEAD_DIM, outs[0], outs[1]).astype(BF16)


def _ctx_attention(p_ctx, hp):
    b, n, _ = p_ctx.shape
    return pl.pallas_call(
        _ctx_attn_kernel,
        out_shape=jax.ShapeDtypeStruct((b, n, hp * LANES), BF16),
        grid=(b, hp),
        in_specs=[
            pl.BlockSpec((1, n, LANES), lambda bi, h: (bi, 0, h)),
            pl.BlockSpec((1, n, LANES), lambda bi, h: (bi, 0, hp + h)),
            pl.BlockSpec((1, n, LANES), lambda bi, h: (bi, 0, 2 * hp + h)),
        ],
        out_specs=pl.BlockSpec((1, n, LANES), lambda bi, h: (bi, 0, h)),
        compiler_params=_params("parallel", "parallel"),
        name="ctx_attention",
    )(p_ctx, p_ctx, p_ctx)


def _gla_kernel(*refs, backward, nc):
    if backward:
        (q_ref, k_ref, v_ref, a_ref, wg_ref, bg_ref, s0_ref, of_ref, g_ref, gn_ref,
         o_ref, sout_ref, s_scr) = refs
    else:
        q_ref, k_ref, v_ref, a_ref, wg_ref, bg_ref, s0_ref, o_ref, sout_ref, s_scr = refs
    j = pl.program_id(1)
    tb = nc * GLA_CHUNK
    kw = GLA_HEADS * GLA_DK

    @pl.when(j == 0)
    def _():
        s_scr[...] = s0_ref[0]

    pre = jnp.dot(a_ref[0], wg_ref[...], preferred_element_type=F32) + bg_ref[...]
    log_a = (jnp.minimum(pre, 0.0) - jnp.log1p(jnp.exp(-jnp.abs(pre)))) * (1.0 / GLA_NORMALIZER)

    pos = lax.broadcasted_iota(jnp.int32, (tb, kw), 0) % GLA_CHUNK
    bsum = log_a
    step = 1
    while step < GLA_CHUNK:
        if backward:
            bsum = bsum + jnp.where(pos < GLA_CHUNK - step, pltpu.roll(bsum, tb - step, 0), 0.0)
        else:
            bsum = bsum + jnp.where(pos >= step, pltpu.roll(bsum, step, 0), 0.0)
        step *= 2

    b3 = bsum.reshape(nc, GLA_CHUNK, kw)
    b_end = b3[:, 0:1, :] if backward else b3[:, GLA_CHUNK - 1:GLA_CHUNK, :]
    q3 = q_ref[0].astype(F32).reshape(nc, GLA_CHUNK, kw)
    k3 = k_ref[0].astype(F32).reshape(nc, GLA_CHUNK, kw)
    q_in = q3 * (GLA_DK ** -0.5) * jnp.exp(b3)
    k_out = (k3 * jnp.exp(-b3)).astype(BF16)
    k_dec = (k3 * jnp.exp(b_end - b3)).astype(BF16)
    decay = jnp.exp(b_end)
    v3 = v_ref[0].reshape(nc, GLA_CHUNK, GLA_HEADS * GLA_DV)

    t_idx = lax.broadcasted_iota(jnp.int32, (nc, GLA_CHUNK, GLA_CHUNK), 1)
    s_idx = lax.broadcasted_iota(jnp.int32, (nc, GLA_CHUNK, GLA_CHUNK), 2)
    tri = (s_idx > t_idx) if backward else (s_idx <= t_idx)
    eye = (lax.broadcasted_iota(jnp.int32, (nc, GLA_DV, GLA_DV), 1)
           == lax.broadcasted_iota(jnp.int32, (nc, GLA_DV, GLA_DV), 2)).astype(BF16)
    lane = lax.broadcasted_iota(jnp.int32, (nc, GLA_CHUNK, LANES), 2)
    order = range(nc - 1, -1, -1) if backward else range(nc)

    outs = []
    for h in range(GLA_HEADS):
        pair = slice((h // 2) * LANES, (h // 2 + 1) * LANES)
        own = (lane >= GLA_DK) if h % 2 else (lane < GLA_DK)
        qm = jnp.where(own, q_in[:, :, pair], 0.0).astype(BF16)
        vh = v3[:, :, h * GLA_DV:(h + 1) * GLA_DV]
        att = jnp.einsum('ctd,csd->cts', qm, k_out[:, :, pair], preferred_element_type=F32)
        att = jnp.where(tri, att, 0.0).astype(BF16)
        o_h = jnp.einsum('cts,csv->ctv', att, vh, preferred_element_type=F32)
        vt = jnp.einsum('cvw,csw->cvs', eye, vh, preferred_element_type=F32).astype(BF16)
        ut = jnp.einsum('cvs,csd->cvd', vt, k_dec[:, :, pair], preferred_element_type=F32)
        s_h = s_scr[h]
        before = [None] * nc
        for c in order:
            before[c] = s_h
            s_h = s_h * decay[c, :, pair] + ut[c]
        s_scr[h] = s_h
        s_all = jnp.stack(before, axis=0).astype(BF16)
        o_h = o_h + jnp.einsum('ctd,cvd->ctv', qm, s_all, preferred_element_type=F32)
        outs.append(o_h.reshape(tb, GLA_DV))

    @pl.when(j == pl.num_programs(1) - 1)
    def _():
        sout_ref[0] = s_scr[...]

    if not backward:
        o_ref[0] = jnp.concatenate(outs, axis=-1)
        return
    of = of_ref[0]
    g = g_ref[0].astype(F32)
    fin = []
    for h in range(GLA_HEADS):
        sl = slice(h * GLA_DV, (h + 1) * GLA_DV)
        y = _rms(of[:, sl] + outs[h]) * gn_ref[...]
        gh = g[:, sl]
        fin.append(y * (gh * jax.nn.sigmoid(gh)))
    o_ref[0] = jnp.concatenate(fin, axis=-1).astype(BF16)


def _gla_sweep(p, wg, bg, s0, nc, col0, a_col, backward, o_fwd=None, g_norm=None):
    b, n, _ = p.shape
    tb = nc * GLA_CHUNK
    nblk = n // tb
    kw = GLA_HEADS * GLA_DK
    vw = GLA_HEADS * GLA_DV
    cq, ck, cv, cg, ca = col0 // kw, col0 // kw + 1, (col0 + 2 * kw) // vw, (col0 + 2 * kw) // vw + 1, a_col // LANES

    def blk(j):
        return nblk - 1 - j if backward else j

    in_specs = [
        pl.BlockSpec((1, tb, kw), lambda bi, j: (bi, blk(j), cq)),
        pl.BlockSpec((1, tb, kw), lambda bi, j: (bi, blk(j), ck)),
        pl.BlockSpec((1, tb, vw), lambda bi, j: (bi, blk(j), cv)),
        pl.BlockSpec((1, tb, LANES), lambda bi, j: (bi, blk(j), ca)),
        pl.BlockSpec((LANES, kw), lambda bi, j: (0, 0)),
        pl.BlockSpec((1, kw), lambda bi, j: (0, 0)),
        pl.BlockSpec((1, GLA_HEADS, GLA_DV, LANES), lambda bi, j: (bi, 0, 0, 0)),
    ]
    args = [p, p, p, p, wg, bg, s0]
    if backward:
        in_specs += [
            pl.BlockSpec((1, tb, vw), lambda bi, j: (bi, blk(j), 0)),
            pl.BlockSpec((1, tb, vw), lambda bi, j: (bi, blk(j), cg)),
            pl.BlockSpec((1, GLA_DV), lambda bi, j: (0, 0)),
        ]
        args += [o_fwd, p, g_norm]
    return pl.pallas_call(
        functools.partial(_gla_kernel, backward=backward, nc=nc),
        out_shape=(jax.ShapeDtypeStruct((b, n, vw), BF16 if backward else F32),
                   jax.ShapeDtypeStruct((b, GLA_HEADS, GLA_DV, LANES), F32)),
        grid=(b, nblk),
        in_specs=in_specs,
        out_specs=(pl.BlockSpec((1, tb, vw), lambda bi, j: (bi, blk(j), 0)),
                   pl.BlockSpec((1, GLA_HEADS, GLA_DV, LANES), lambda bi, j: (bi, 0, 0, 0))),
        scratch_shapes=[pltpu.VMEM((GLA_HEADS, GLA_DV, LANES), F32)],
        compiler_params=_params("parallel", "arbitrary"),
        name="gla_bwd" if backward else "gla_fwd",
    )(*args)


def _diff_lambda(lam_ref, lam_init):
    lp = lam_ref[...]
    return (jnp.exp(jnp.sum(lp[0:1] * lp[1:2], axis=-1, keepdims=True))
            - jnp.exp(jnp.sum(lp[2:3] * lp[3:4], axis=-1, keepdims=True)) + lam_init)


def _branch_queries(q):
    lane = lax.broadcasted_iota(jnp.int32, q.shape, 1)
    return jnp.concatenate([jnp.where(lane < DIFF_DK, q, 0.0), jnp.where(lane >= DIFF_DK, q, 0.0)], axis=0)


def _diff_ctx_kernel(q_ref, k_ref, v_ref, lam_ref, gn_ref, o_ref, *, lam_init):
    tq = q_ref.shape[1]
    s = _nt_dot(_branch_queries(q_ref[0]), k_ref[0])
    p = jnp.exp2(s - jnp.max(s, axis=-1, keepdims=True))
    a = jnp.dot(p.astype(BF16), v_ref[0], preferred_element_type=F32) / jnp.sum(p, axis=-1, keepdims=True)
    o = a[0:tq] - _diff_lambda(lam_ref, lam_init) * a[tq:2 * tq]
    o_ref[0] = (_rms(o) * gn_ref[...] * (1.0 - lam_init)).astype(BF16)


def _diff_attention_ctx(qk_c, p_c, lam_params, g_norm, lam_init, v_col0):
    b, n, w = qk_c.shape
    heads = w // (2 * LANES)
    vb = v_col0 // LANES
    return pl.pallas_call(
        functools.partial(_diff_ctx_kernel, lam_init=lam_init),
        out_shape=jax.ShapeDtypeStruct((b, n, heads * DIFF_DV), BF16),
        grid=(b, heads),
        in_specs=[
            pl.BlockSpec((1, n, LANES), lambda bi, h: (bi, 0, h)),
            pl.BlockSpec((1, n, LANES), lambda bi, h: (bi, 0, heads + h)),
            pl.BlockSpec((1, n, LANES), lambda bi, h: (bi, 0, vb + h)),
            pl.BlockSpec((4, DIFF_DK), lambda bi, h: (0, 0)),
            pl.BlockSpec((1, DIFF_DV), lambda bi, h: (0, 0)),
        ],
        out_specs=pl.BlockSpec((1, n, LANES), lambda bi, h: (bi, 0, h)),
        compiler_params=_params("parallel", "parallel"),
        name="diff_attention_ctx",
    )(qk_c, qk_c, p_c, lam_params, g_norm)


def _diff_kernel(q_ref, k_ref, v_ref, kc_ref, vc_ref, lam_ref, gn_ref, o_ref,
                 qs, vt, vct, s_buf, c_buf, p_buf, a_buf, m_scr, acc, *, nch, lam_init):
    tq = q_ref.shape[1]
    tkc = k_ref.shape[1] // nch

    def transposed_with_ones(v):
        pad_rows = lax.broadcasted_iota(jnp.int32, (DIFF_VT_ROWS - DIFF_DV, v.shape[0]), 0)
        return jnp.concatenate([v.astype(F32).T.astype(BF16), jnp.where(pad_rows == 0, 1.0, 0.0).astype(BF16)],
                               axis=0)

    @pl.when(pl.program_id(2) == 0)
    def _():
        for c in range(nch):
            vt[c] = transposed_with_ones(v_ref[0, c * tkc:(c + 1) * tkc, :])
        vct[...] = transposed_with_ones(vc_ref[0])

    def stage(value_chunk, softmax_on, score_chunk):
        for n in range(2 * tq // DIFF_LANE_TILE):
            sl = slice(n * DIFF_LANE_TILE, (n + 1) * DIFF_LANE_TILE)
            if value_chunk is not None:
                pv = jnp.dot(vt[value_chunk], p_buf[:, sl], preferred_element_type=F32)
                acc[:, sl] = acc[:, sl] * a_buf[:, sl] + pv
            if softmax_on:
                m_old = m_scr[:, sl]
                m_new = jnp.maximum(m_old, c_buf[:, sl])
                a_buf[:, sl] = jnp.exp2(m_old - m_new)
                m_scr[:, sl] = m_new
                p_buf[:, sl] = jnp.exp2(s_buf[:, sl] - m_new).astype(BF16)
            if score_chunk is not None:
                k = k_ref[0, pl.ds(pl.multiple_of(score_chunk * tkc, tkc), tkc), :]
                s = _nt_dot(k, qs[sl, :])
                s_buf[:, sl] = s
                c_buf[:, sl] = jnp.max(s, axis=0, keepdims=True)

    qs[...] = _branch_queries(q_ref[0])
    sc = _nt_dot(kc_ref[0], qs[...])
    mc = jnp.max(sc, axis=0, keepdims=True)
    pc = jnp.exp2(sc - mc)
    m_scr[...] = mc
    acc[...] = jnp.dot(vct[...], pc.astype(BF16), preferred_element_type=F32)
    stage(None, False, 0)
    stage(None, True, 1)

    def steady(c, carry):
        stage(c - 1, True, c + 1)
        return carry

    lax.fori_loop(1, nch - 1, steady, 0, unroll=2 if nch % 2 == 0 else 1)
    stage(nch - 2, True, None)
    stage(nch - 1, False, None)

    a = acc[0:DIFF_DV, :] / acc[DIFF_DV:DIFF_DV + 1, :]
    o = a[:, 0:tq] - _diff_lambda(lam_ref, lam_init) * a[:, tq:2 * tq]
    y = o * lax.rsqrt(jnp.mean(o * o, axis=0, keepdims=True) + EPS) * gn_ref[...] * (1.0 - lam_init)
    o_ref[0] = y.T.astype(BF16)


def _diff_attention(qk_x, p_x, qk_c, p_c, lam_params, g_norm, lam_init, tq, tkc, v_col0):
    b, t, w = qk_x.shape
    heads = w // (2 * LANES)
    n_ctx = qk_c.shape[1]
    nch = t // tkc
    vb = v_col0 // LANES
    assert nch >= 2 and nch * tkc == t
    return pl.pallas_call(
        functools.partial(_diff_kernel, nch=nch, lam_init=lam_init),
        out_shape=jax.ShapeDtypeStruct((b, t, heads * DIFF_DV), BF16),
        grid=(b, heads, t // tq),
        in_specs=[
            pl.BlockSpec((1, tq, LANES), lambda bi, h, i: (bi, i, h)),
            pl.BlockSpec((1, t, LANES), lambda bi, h, i: (bi, 0, heads + h)),
            pl.BlockSpec((1, t, LANES), lambda bi, h, i: (bi, 0, vb + h)),
            pl.BlockSpec((1, n_ctx, LANES), lambda bi, h, i: (bi, 0, heads + h)),
            pl.BlockSpec((1, n_ctx, LANES), lambda bi, h, i: (bi, 0, vb + h)),
            pl.BlockSpec((4, DIFF_DK), lambda bi, h, i: (0, 0)),
            pl.BlockSpec((DIFF_DV, 1), lambda bi, h, i: (0, 0)),
        ],
        out_specs=pl.BlockSpec((1, tq, LANES), lambda bi, h, i: (bi, i, h)),
        scratch_shapes=[
            pltpu.VMEM((2 * tq, LANES), BF16),
            pltpu.VMEM((nch, DIFF_VT_ROWS, tkc), BF16),
            pltpu.VMEM((DIFF_VT_ROWS, n_ctx), BF16),
            pltpu.VMEM((tkc, 2 * tq), F32),
            pltpu.VMEM((1, 2 * tq), F32),
            pltpu.VMEM((tkc, 2 * tq), BF16),
            pltpu.VMEM((1, 2 * tq), F32),
            pltpu.VMEM((1, 2 * tq), F32),
            pltpu.VMEM((DIFF_VT_ROWS, 2 * tq), F32),
        ],
        compiler_params=_params("parallel", "parallel", "arbitrary"),
        name="diff_attention",
    )(qk_x, qk_x, p_x, qk_c, p_c, lam_params, g_norm.reshape(DIFF_DV, 1))


def _outproj_kernel(ya_ref, yb_ref, yd_ref, w_ref, x_ref, g_ref, gate_ref, gmlp_ref, shift_ref, scale_ref,
                    o_ref, h_ref):
    wa, wb = ya_ref.shape[2], yb_ref.shape[2]
    for r in range(x_ref.shape[1] // ROW_GROUP):
        rows = slice(r * ROW_GROUP, (r + 1) * ROW_GROUP)
        y = jnp.dot(ya_ref[0, rows, :], w_ref[0:wa, :], preferred_element_type=F32)
        y = y + jnp.dot(yb_ref[0, rows, :], w_ref[wa:wa + wb, :], preferred_element_type=F32)
        y = y + jnp.dot(yd_ref[0, rows, :], w_ref[wa + wb:, :], preferred_element_type=F32)
        x_new = x_ref[0, rows, :] + gate_ref[0] * (_rms(y) * g_ref[...])
        o_ref[0, rows, :] = x_new
        h_ref[0, rows, :] = _modulate(x_new, gmlp_ref, shift_ref, scale_ref)


def _outproj(ya, yb, yd, w, x, g_post, g_mlp, mods, mod_row0, tm):
    b, t, d = x.shape

    def mod(comp):
        return pl.BlockSpec((1, 1, d), lambda bi, i: ((bi + mod_row0) * 6 + comp, 0, 0))

    row = pl.BlockSpec((1, d), lambda bi, i: (0, 0))
    tile = pl.BlockSpec((1, tm, d), lambda bi, i: (bi, i, 0))
    return pl.pallas_call(
        _outproj_kernel,
        out_shape=(jax.ShapeDtypeStruct((b, t, d), F32), jax.ShapeDtypeStruct((b, t, d), BF16)),
        grid=(b, t // tm),
        in_specs=[
            pl.BlockSpec((1, tm, ya.shape[2]), lambda bi, i: (bi, i, 0)),
            pl.BlockSpec((1, tm, yb.shape[2]), lambda bi, i: (bi, i, 0)),
            pl.BlockSpec((1, tm, yd.shape[2]), lambda bi, i: (bi, i, 0)),
            pl.BlockSpec(w.shape, lambda bi, i: (0, 0)),
            tile, row, mod(2), row, mod(3), mod(4),
        ],
        out_specs=(tile, tile),
        compiler_params=_params("parallel", "parallel"),
        name="outproj",
    )(ya, yb, yd, w, x, g_post, mods, g_mlp, mods, mods)


def _mlp_kernel(x_ref, h_ref, w1_ref, w2_ref, gpost_ref, gate_ref, *rest, emit_next):
    if emit_next:
        gnext_ref, shift_ref, scale_ref, o_ref, hnext_ref, acc = rest
    else:
        o_ref, acc = rest
    k = pl.program_id(2)

    @pl.when(k == 0)
    def _():
        acc[...] = jnp.zeros(acc.shape, F32)

    def hidden_tile(rows):
        u = jnp.maximum(jnp.dot(h_ref[0, rows, :], w1_ref[...], preferred_element_type=F32), 0.0)
        return jnp.dot((u * u).astype(BF16), w2_ref[...], preferred_element_type=F32)

    last = pl.num_programs(2) - 1

    @pl.when(k < last)
    def _():
        acc[...] += hidden_tile(slice(None))

    @pl.when(k == last)
    def _():
        for r in range(acc.shape[0] // ROW_GROUP):
            rows = slice(r * ROW_GROUP, (r + 1) * ROW_GROUP)
            y = acc[rows, :] + hidden_tile(rows)
            x_new = x_ref[0, rows, :] + gate_ref[0] * (_rms(y) * gpost_ref[...])
            o_ref[0, rows, :] = x_new
            if emit_next:
                hnext_ref[0, rows, :] = _modulate(x_new, gnext_ref, shift_ref, scale_ref)


def _mlp(x, h, gpost, mods, mod_row0, w1, w2, tm, th, g_next=None, mods_next=None):
    b, t, d = x.shape
    hid = w1.shape[1]
    emit_next = g_next is not None

    def mod(comp):
        return pl.BlockSpec((1, 1, d), lambda bi, i, k: ((bi + mod_row0) * 6 + comp, 0, 0))

    row = pl.BlockSpec((1, d), lambda bi, i, k: (0, 0))
    tile = pl.BlockSpec((1, tm, d), lambda bi, i, k: (bi, i, 0))
    in_specs = [
        tile, tile,
        pl.BlockSpec((d, th), lambda bi, i, k: (0, k)),
        pl.BlockSpec((th, d), lambda bi, i, k: (k, 0)),
        row, mod(5),
    ]
    args = [x, h, w1, w2, gpost, mods]
    out_shape = [jax.ShapeDtypeStruct((b, t, d), F32)]
    out_specs = [tile]
    if emit_next:
        in_specs += [row, mod(0), mod(1)]
        args += [g_next, mods_next, mods_next]
        out_shape.append(jax.ShapeDtypeStruct((b, t, d), BF16))
        out_specs.append(tile)
    out = pl.pallas_call(
        functools.partial(_mlp_kernel, emit_next=emit_next),
        out_shape=tuple(out_shape),
        grid=(b, t // tm, hid // th),
        in_specs=in_specs,
        out_specs=tuple(out_specs),
        scratch_shapes=[pltpu.VMEM((tm, d), F32)],
        compiler_params=_params("parallel", "parallel", "arbitrary"),
        name="mlp",
    )(*args)
    return out if emit_next else (out[0], None)


def _rope_tables(t):
    tok = jnp.arange(t, dtype=jnp.int32)
    row = (tok // GRID_W).astype(F32)
    col = (tok % GRID_W).astype(F32)
    n_freq = DIFF_DK // 4
    inv = ROPE_BASE ** (-jnp.arange(n_freq, dtype=F32) / n_freq)
    ang = jnp.concatenate([row[:, None] * inv, col[:, None] * inv], axis=-1)
    cos, sin = jnp.cos(ang), jnp.sin(ang)
    reps = LANES // DIFF_DK
    return (jnp.tile(jnp.concatenate([cos, cos], axis=-1), (1, reps)),
            jnp.tile(jnp.concatenate([-sin, sin], axis=-1), (1, reps)))


def _pick(n, pref):
    return pref if n % pref == 0 else n


def kernel(x, c, ctx, c_ctx, w_ada, b_ada, g_pre_mix, g_post_mix, g_pre_mlp, g_post_mlp, w_in, w_out,
           na_rel_bias, gla_w_gate, gla_b_gate, gla_g_norm, diff_lambda, diff_g_norm, w_mlp_in, w_mlp_out):
    bsz, t, d = x.shape
    n_ctx = ctx.shape[1]
    depth = w_ada.shape[0]
    na_heads = na_rel_bias.shape[1]
    na_w = na_heads * HEAD_DIM
    gla_kw = GLA_HEADS * GLA_DK
    gla_vw = GLA_HEADS * GLA_DV
    diff_w = w_out.shape[1] - na_w - gla_vw
    diff_heads = diff_w // DIFF_DV
    diff_qk = diff_heads * 2 * DIFF_DK
    rows = t // GRID_W
    assert t % (NA_ROWS_PER_STEP * GRID_W) == 0 and rows >= NA_KEY_ROWS
    assert bsz + 1 <= 8

    o_gla = 3 * na_w
    o_gate = o_gla + 2 * gla_kw + 2 * gla_vw
    o_diff = o_gate + 2 * GLA_GATE_RANK
    gla_col0 = 3 * na_w
    diffv_col0 = gla_col0 + 2 * gla_kw + 2 * gla_vw
    gate_col0 = diffv_col0 + diff_w

    cc = jnp.zeros((8, d), F32).at[:bsz].set(c).at[bsz].set(c_ctx)
    mods_all = _mods(cc, w_ada, b_ada)
    rope_tabs = _rope_tables(t)
    xc = ctx.reshape(1, bsz * n_ctx, d)
    zeros_state = jnp.zeros((bsz, GLA_HEADS, GLA_DV, LANES), F32)

    tm_in = _pick(t, 1024)
    tm_out = _pick(t, 512)
    tm_ctx = _pick(bsz * n_ctx, 512)
    tq = _pick(t, 512)
    tkc = min(2048, t // 2)
    nc_lat = _pick(t // GLA_CHUNK, 16)
    nc_ctx = _pick(n_ctx // GLA_CHUNK, 4)

    mods_of = [mods_all[l].reshape(8 * 6, 1, d) for l in range(depth)]
    h_x = _premix(x, g_pre_mix[0].reshape(1, d), mods_of[0], 0, tm_out)
    h_c = _premix(xc, g_pre_mix[0].reshape(1, d), mods_of[0], bsz, tm_ctx)

    for l in range(depth):
        need_ctx = l < depth - 1
        mods = mods_of[l]
        g_next = g_pre_mix[l + 1].reshape(1, d) if need_ctx else None
        mods_next = mods_of[l + 1] if need_ctx else None
        wl = w_in[l].astype(BF16)
        w_qk = wl[:, o_diff:o_diff + 2 * diff_qk]
        w_rest = jnp.concatenate(
            [wl[:, :o_gate], wl[:, o_diff + 2 * diff_qk:], wl[:, o_gate:o_diff],
             jnp.zeros((d, LANES - 2 * GLA_GATE_RANK), BF16)], axis=1)
        w_o = w_out[l].astype(BF16)
        w1 = w_mlp_in[l].astype(BF16)
        w2 = w_mlp_out[l].astype(BF16)
        g2 = g_post_mix[l].reshape(1, d)
        g3 = g_pre_mlp[l].reshape(1, d)
        g4 = g_post_mlp[l].reshape(1, d)
        wg = jnp.zeros((2, LANES, gla_kw), F32)
        wg = wg.at[0, :GLA_GATE_RANK].set(gla_w_gate[l, 0]).at[1, GLA_GATE_RANK:2 * GLA_GATE_RANK].set(gla_w_gate[l, 1])
        wg = wg.astype(BF16)
        bg = gla_b_gate[l].reshape(2, 1, gla_kw)
        gla_gn = gla_g_norm[l].reshape(1, GLA_DV)
        diff_gn = diff_g_norm[l].reshape(1, DIFF_DV)
        lam_init = 0.8 - 0.6 * math.exp(-0.3 * l)

        qk_x = _inproj(h_x, w_qk, tm_in, diff_qk, rope_tabs, 1, DIFF_Q_SCALE)
        p_x = _inproj(h_x, w_rest, tm_in, w_rest.shape[1] // 3)
        qk_c = _inproj(h_c, w_qk, tm_ctx, diff_qk, None, 1, DIFF_Q_SCALE).reshape(bsz, n_ctx, -1)
        p_c = _inproj(h_c, w_rest, tm_ctx, w_rest.shape[1] // 3).reshape(bsz, n_ctx, -1)

        ya = _na_attention(p_x, p_c, _na_bias_table(na_rel_bias[l], rows))

        of_c, s_f = _gla_sweep(p_c, wg[0], bg[0], zeros_state, nc_ctx, gla_col0, gate_col0, False)
        yb_c, s_b = _gla_sweep(p_c, wg[1], bg[1], zeros_state, nc_ctx, gla_col0, gate_col0, True, of_c, gla_gn)
        of_x, _ = _gla_sweep(p_x, wg[0], bg[0], s_f, nc_lat, gla_col0, gate_col0, False)
        yb, _ = _gla_sweep(p_x, wg[1], bg[1], s_b, nc_lat, gla_col0, gate_col0, True, of_x, gla_gn)

        yd = _diff_attention(qk_x, p_x, qk_c, p_c, diff_lambda[l], diff_gn, lam_init, tq, tkc, diffv_col0)

        x, hm_x = _outproj(ya, yb, yd, w_o, x, g2, g3, mods, 0, tm_out)
        x, h_x = _mlp(x, hm_x, g4, mods, 0, w1, w2, tm_out, MLP_HIDDEN_TILE, g_next, mods_next)

        if need_ctx:
            ya_c = _ctx_attention(p_c, na_heads // 2)
            yd_c = _diff_attention_ctx(qk_c, p_c, diff_lambda[l], diff_gn, lam_init, diffv_col0)
            flat = lambda z: z.reshape(1, bsz * n_ctx, -1)
            xc, hm_c = _outproj(flat(ya_c), flat(yb_c), flat(yd_c), w_o, xc, g2, g3, mods, bsz, tm_ctx)
            xc, h_c = _mlp(xc, hm_c, g4, mods, bsz, w1, w2, tm_ctx, MLP_HIDDEN_TILE, g_next, mods_next)
    return x
```

```python
import functools
import math

import jax
import jax.numpy as jnp
from jax import lax
from jax.experimental import pallas as pl
from jax.experimental.pallas import tpu as pltpu

F32 = jnp.float32
BF16 = jnp.bfloat16

GRID_W = 64
EPS = 1e-6
NEG_INF = -1e30
HEAD_DIM = 64
NA_WIN_R = 8
NA_WIN_C = 16
GLA_HEADS = 4
GLA_DK = 64
GLA_DV = 128
GLA_GATE_RANK = 16
GLA_NORMALIZER = 16.0
GLA_CHUNK = 64
DIFF_DK = 64
DIFF_DV = 128
ROPE_BASE = 10000.0
DIFF_Q_SCALE = DIFF_DK ** -0.5 * math.log2(math.e)

LANES = 128
DIFF_LANE_TILE = 1024
DIFF_VT_ROWS = 128 + 16
VMEM_LIMIT = 56 * 1024 * 1024

MLP_HIDDEN_TILE = 1024
ROW_GROUP = 256
NA_ROWS_PER_STEP = 8
NA_KEY_ROWS = NA_ROWS_PER_STEP + NA_WIN_R


def _params(*sem):
    return pltpu.CompilerParams(dimension_semantics=sem, vmem_limit_bytes=VMEM_LIMIT)


def _nt_dot(a, b):
    return lax.dot_general(a, b, (((1,), (1,)), ((), ())), preferred_element_type=F32)


def _rms(y):
    return y * lax.rsqrt(jnp.mean(y * y, axis=-1, keepdims=True) + EPS)


def _mods_kernel(c_ref, w_ref, b_ref, o_ref):
    c = c_ref[...]
    s = (c * jax.nn.sigmoid(c)).astype(BF16)
    o_ref[0] = jnp.dot(s, w_ref[0].astype(BF16), preferred_element_type=F32) + b_ref[0]


def _mods(cc, w_ada, b_ada):
    depth, d, n = w_ada.shape
    tn = 1024
    return pl.pallas_call(
        _mods_kernel,
        out_shape=jax.ShapeDtypeStruct((depth, 8, n), F32),
        grid=(depth, n // tn),
        in_specs=[
            pl.BlockSpec((8, d), lambda l, j: (0, 0)),
            pl.BlockSpec((1, d, tn), lambda l, j: (l, 0, j)),
            pl.BlockSpec((1, 1, tn), lambda l, j: (l, 0, j)),
        ],
        out_specs=pl.BlockSpec((1, 8, tn), lambda l, j: (l, 0, j)),
        compiler_params=_params("arbitrary", "arbitrary"),
        name="ada_mods",
    )(cc, w_ada, b_ada.reshape(depth, 1, n))


def _modulate(x, g_ref, shift_ref, scale_ref):
    return (_rms(x) * g_ref[...] * (1.0 + scale_ref[0]) + shift_ref[0]).astype(BF16)


def _premix_kernel(x_ref, g_ref, shift_ref, scale_ref, o_ref):
    o_ref[0] = _modulate(x_ref[0], g_ref, shift_ref, scale_ref)


def _premix(x, g, mods, mod_row0, tm):
    b, t, d = x.shape
    return pl.pallas_call(
        _premix_kernel,
        out_shape=jax.ShapeDtypeStruct((b, t, d), BF16),
        grid=(b, t // tm),
        in_specs=[
            pl.BlockSpec((1, tm, d), lambda bi, i: (bi, i, 0)),
            pl.BlockSpec((1, d), lambda bi, i: (0, 0)),
            pl.BlockSpec((1, 1, d), lambda bi, i: ((bi + mod_row0) * 6 + 0, 0, 0)),
            pl.BlockSpec((1, 1, d), lambda bi, i: ((bi + mod_row0) * 6 + 1, 0, 0)),
        ],
        out_specs=pl.BlockSpec((1, tm, d), lambda bi, i: (bi, i, 0)),
        compiler_params=_params("parallel", "parallel"),
        name="premix",
    )(x, g, mods, mods)


def _inproj_kernel(h_ref, w_ref, *rest, rope, q_tiles, q_scale):
    if rope:
        cos_ref, sin_ref, o_ref = rest
    else:
        (o_ref,) = rest
    for r in range(h_ref.shape[1] // ROW_GROUP):
        rows = slice(r * ROW_GROUP, (r + 1) * ROW_GROUP)
        acc = jnp.dot(h_ref[0, rows, :], w_ref[...], preferred_element_type=F32)
        if q_tiles:
            acc = acc * jnp.where(pl.program_id(2) < q_tiles, q_scale, 1.0)
        if not rope:
            o_ref[0, rows, :] = acc.astype(BF16)
            continue
        cos = cos_ref[rows, :]
        sin = sin_ref[rows, :]
        lower = (lax.broadcasted_iota(jnp.int32, cos.shape, 1) % DIFF_DK) < DIFF_DK // 2
        for c in range(acc.shape[1] // LANES):
            a = acc[:, c * LANES:(c + 1) * LANES]
            partner = jnp.where(lower, pltpu.roll(a, LANES - DIFF_DK // 2, 1), pltpu.roll(a, DIFF_DK // 2, 1))
            o_ref[0, rows, c * LANES:(c + 1) * LANES] = (a * cos + partner * sin).astype(BF16)


def _inproj(h, w, tm, tn, rope_tabs=None, q_tiles=0, q_scale=1.0):
    b, t, d = h.shape
    n = w.shape[1]
    rope = rope_tabs is not None
    in_specs = [
        pl.BlockSpec((1, tm, d), lambda bi, i, j: (bi, i, 0)),
        pl.BlockSpec((d, tn), lambda bi, i, j: (0, j)),
    ]
    args = [h, w]
    if rope:
        in_specs += [pl.BlockSpec((tm, LANES), lambda bi, i, j: (i, 0))] * 2
        args += list(rope_tabs)
    return pl.pallas_call(
        functools.partial(_inproj_kernel, rope=rope, q_tiles=q_tiles, q_scale=q_scale),
        out_shape=jax.ShapeDtypeStruct((b, t, n), BF16),
        grid=(b, t // tm, n // tn),
        in_specs=in_specs,
        out_specs=pl.BlockSpec((1, tm, tn), lambda bi, i, j: (bi, i, j)),
        compiler_params=_params("parallel", "parallel", "parallel"),
        name="inproj_rope" if rope else "inproj",
    )(*args)


def _na_bias_table(rel_bias, rows):
    rq, kwin = NA_ROWS_PER_STEP, NA_KEY_ROWS
    nb = rows // rq
    h = rel_bias.shape[0]
    blocks = jnp.array([0, min(1, nb - 1), nb - 1], jnp.int32)
    r = blocks[:, None] * rq + jnp.arange(rq, dtype=jnp.int32)[None, :]
    kst = jnp.clip(blocks * rq - NA_WIN_R // 2, 0, rows - kwin)
    kr = kst[:, None] + jnp.arange(kwin, dtype=jnp.int32)[None, :]
    s_r = jnp.clip(r - NA_WIN_R // 2, 0, rows - NA_WIN_R)
    row_ok = (kr[:, None, :] >= s_r[:, :, None]) & (kr[:, None, :] < s_r[:, :, None] + NA_WIN_R)
    dr = jnp.clip(kr[:, None, :] - r[:, :, None] + (NA_WIN_R - 1), 0, 2 * NA_WIN_R - 2)
    col = jnp.arange(GRID_W, dtype=jnp.int32)
    c0 = jnp.clip(col - NA_WIN_C // 2, 0, GRID_W - NA_WIN_C)
    col_ok = (col[None, :] >= c0[:, None]) & (col[None, :] < c0[:, None] + NA_WIN_C)
    dc = jnp.clip(col[None, :] - col[:, None] + (NA_WIN_C - 1), 0, 2 * NA_WIN_C - 2)
    pick_dc = (dc[:, :, None] == jnp.arange(2 * NA_WIN_C - 1)[None, None, :]).astype(F32)
    pick_dr = (dr[..., None] == jnp.arange(2 * NA_WIN_R - 1)[None, None, None, :]).astype(F32)
    by_col = jnp.einsum('hrd,qkd->hrqk', rel_bias.astype(F32), pick_dc, precision=lax.Precision.HIGHEST)
    bias = jnp.einsum('vqlr,hrab->vhqalb', pick_dr, by_col, precision=lax.Precision.HIGHEST)
    ok = row_ok[:, None, :, None, :, None] & col_ok[None, None, None, :, None, :]
    bias = jnp.where(ok, bias, NEG_INF)
    return bias.reshape(3, h, rq * GRID_W, kwin * GRID_W)


def _na_kernel(q_ref, qn_ref, k_ref, v_ref, kc_ref, vc_ref, bias_ref, o_ref, s_loc, s_ctx, *, rows):
    i = pl.program_id(2)
    window = NA_KEY_ROWS * GRID_W
    kc = kc_ref[0]
    vc = vc_ref[0]
    lane = lax.broadcasted_iota(jnp.int32, q_ref.shape[1:], 1)

    def window_start(step):
        kstart = jnp.clip(step * NA_ROWS_PER_STEP - NA_WIN_R // 2, 0, rows - NA_KEY_ROWS)
        return pl.multiple_of(kstart * GRID_W, GRID_W)

    def raw_scores(q, step, hh):
        own = (lane >= hh * HEAD_DIM) & (lane < (hh + 1) * HEAD_DIM)
        qh = jnp.where(own, q * (HEAD_DIM ** -0.5), 0.0).astype(BF16)
        s_loc[hh] = _nt_dot(qh, k_ref[0, pl.ds(window_start(step), window), :])
        s_ctx[hh] = _nt_dot(qh, kc)

    @pl.when(i == 0)
    def _():
        for hh in range(LANES // HEAD_DIM):
            raw_scores(q_ref[0], 0, hh)

    following = jnp.minimum(i + 1, pl.num_programs(2) - 1)
    vw = v_ref[0, pl.ds(window_start(i), window), :]
    outs = []
    for hh in range(LANES // HEAD_DIM):
        sl = s_loc[hh] + bias_ref[0, hh]
        sc = s_ctx[hh]
        m = jnp.maximum(jnp.max(sl, axis=-1, keepdims=True), jnp.max(sc, axis=-1, keepdims=True))
        p_loc = jnp.exp(sl - m)
        p_ctx = jnp.exp(sc - m)
        l = jnp.sum(p_loc, axis=-1, keepdims=True) + jnp.sum(p_ctx, axis=-1, keepdims=True)
        raw_scores(qn_ref[0], following, hh)
        o = jnp.dot(p_loc.astype(BF16), vw, preferred_element_type=F32)
        o = o + jnp.dot(p_ctx.astype(BF16), vc, preferred_element_type=F32)
        outs.append(o / l)
    o_ref[0] = jnp.where(lane < HEAD_DIM, outs[0], outs[1]).astype(BF16)


def _na_attention(p_lat, p_ctx, bias_tbl):
    b, t, _ = p_lat.shape
    n = p_ctx.shape[1]
    rows = t // GRID_W
    nb = rows // NA_ROWS_PER_STEP
    hp = bias_tbl.shape[1] // 2
    tq = NA_ROWS_PER_STEP * GRID_W

    def variant(i):
        return jnp.where(i == 0, 0, jnp.where(i == nb - 1, 2, 1))

    return pl.pallas_call(
        functools.partial(_na_kernel, rows=rows),
        out_shape=jax.ShapeDtypeStruct((b, t, hp * LANES), BF16),
        grid=(b, hp, nb),
        in_specs=[
            pl.BlockSpec((1, tq, LANES), lambda bi, h, i: (bi, i, h)),
            pl.BlockSpec((1, tq, LANES), lambda bi, h, i: (bi, jnp.minimum(i + 1, nb - 1), h)),
            pl.BlockSpec((1, t, LANES), lambda bi, h, i: (bi, 0, hp + h)),
            pl.BlockSpec((1, t, LANES), lambda bi, h, i: (bi, 0, 2 * hp + h)),
            pl.BlockSpec((1, n, LANES), lambda bi, h, i: (bi, 0, hp + h)),
            pl.BlockSpec((1, n, LANES), lambda bi, h, i: (bi, 0, 2 * hp + h)),
            pl.BlockSpec((1, 2, tq, NA_KEY_ROWS * GRID_W), lambda bi, h, i: (variant(i), h, 0, 0)),
        ],
        out_specs=pl.BlockSpec((1, tq, LANES), lambda bi, h, i: (bi, i, h)),
        scratch_shapes=[
            pltpu.VMEM((LANES // HEAD_DIM, tq, NA_KEY_ROWS * GRID_W), F32),
            pltpu.VMEM((LANES // HEAD_DIM, tq, n), F32),
        ],
        compiler_params=_params("parallel", "parallel", "arbitrary"),
        name="na_attention",
    )(p_lat, p_lat, p_lat, p_lat, p_ctx, p_ctx, bias_tbl)


def _ctx_attn_kernel(q_ref, k_ref, v_ref, o_ref):
    q, k, v = q_ref[0], k_ref[0], v_ref[0]
    scale = HEAD_DIM ** -0.5
    lane = lax.broadcasted_iota(jnp.int32, q.shape, 1)
    outs = []
    for hh in range(LANES // HEAD_DIM):
        own = (lane >= hh * HEAD_DIM) & (lane < (hh + 1) * HEAD_DIM)
        s = _nt_dot(jnp.where(own, q, 0.0).astype(BF16), k) * scale
        p = jnp.exp(s - jnp.max(s, axis=-1, keepdims=True))
        l = jnp.sum(p, axis=-1, keepdims=True)
        outs.append(jnp.dot(p.astype(BF16), v, preferred_element_type=F32) / l)
    o_ref[0] = jnp.where(lane < HEAD_DIM, outs[0], outs[1]).astype(BF16)


def _ctx_attention(p_ctx, hp):
    b, n, _ = p_ctx.shape
    return pl.pallas_call(
        _ctx_attn_kernel,
        out_shape=jax.ShapeDtypeStruct((b, n, hp * LANES), BF16),
        grid=(b, hp),
        in_specs=[
            pl.BlockSpec((1, n, LANES), lambda bi, h: (bi, 0, h)),
            pl.BlockSpec((1, n, LANES), lambda bi, h: (bi, 0, hp + h)),
            pl.BlockSpec((1, n, LANES), lambda bi, h: (bi, 0, 2 * hp + h)),
        ],
        out_specs=pl.BlockSpec((1, n, LANES), lambda bi, h: (bi, 0, h)),
        compiler_params=_params("parallel", "parallel"),
        name="ctx_attention",
    )(p_ctx, p_ctx, p_ctx)


def _gla_kernel(*refs, backward, nc):
    if backward:
        (q_ref, k_ref, v_ref, a_ref, wg_ref, bg_ref, s0_ref, of_ref, g_ref, gn_ref,
         o_ref, sout_ref, s_scr) = refs
    else:
        q_ref, k_ref, v_ref, a_ref, wg_ref, bg_ref, s0_ref, o_ref, sout_ref, s_scr = refs
    j = pl.program_id(1)
    tb = nc * GLA_CHUNK
    kw = GLA_HEADS * GLA_DK

    @pl.when(j == 0)
    def _():
        s_scr[...] = s0_ref[0]

    pre = jnp.dot(a_ref[0], wg_ref[...], preferred_element_type=F32) + bg_ref[...]
    log_a = (jnp.minimum(pre, 0.0) - jnp.log1p(jnp.exp(-jnp.abs(pre)))) * (1.0 / GLA_NORMALIZER)

    pos = lax.broadcasted_iota(jnp.int32, (tb, kw), 0) % GLA_CHUNK
    bsum = log_a
    step = 1
    while step < GLA_CHUNK:
        if backward:
            bsum = bsum + jnp.where(pos < GLA_CHUNK - step, pltpu.roll(bsum, tb - step, 0), 0.0)
        else:
            bsum = bsum + jnp.where(pos >= step, pltpu.roll(bsum, step, 0), 0.0)
        step *= 2

    b3 = bsum.reshape(nc, GLA_CHUNK, kw)
    b_end = b3[:, 0:1, :] if backward else b3[:, GLA_CHUNK - 1:GLA_CHUNK, :]
    q3 = q_ref[0].astype(F32).reshape(nc, GLA_CHUNK, kw)
    k3 = k_ref[0].astype(F32).reshape(nc, GLA_CHUNK, kw)
    q_in = q3 * (GLA_DK ** -0.5) * jnp.exp(b3)
    k_out = (k3 * jnp.exp(-b3)).astype(BF16)
    k_dec = (k3 * jnp.exp(b_end - b3)).astype(BF16)
    decay = jnp.exp(b_end)
    v3 = v_ref[0].reshape(nc, GLA_CHUNK, GLA_HEADS * GLA_DV)

    t_idx = lax.broadcasted_iota(jnp.int32, (nc, GLA_CHUNK, GLA_CHUNK), 1)
    s_idx = lax.broadcasted_iota(jnp.int32, (nc, GLA_CHUNK, GLA_CHUNK), 2)
    tri = (s_idx > t_idx) if backward else (s_idx <= t_idx)
    eye = (lax.broadcasted_iota(jnp.int32, (nc, GLA_DV, GLA_DV), 1)
           == lax.broadcasted_iota(jnp.int32, (nc, GLA_DV, GLA_DV), 2)).astype(BF16)
    lane = lax.broadcasted_iota(jnp.int32, (nc, GLA_CHUNK, LANES), 2)
    order = range(nc - 1, -1, -1) if backward else range(nc)

    outs = []
    for h in range(GLA_HEADS):
        pair = slice((h // 2) * LANES, (h // 2 + 1) * LANES)
        own = (lane >= GLA_DK) if h % 2 else (lane < GLA_DK)
        qm = jnp.where(own, q_in[:, :, pair], 0.0).astype(BF16)
        vh = v3[:, :, h * GLA_DV:(h + 1) * GLA_DV]
        att = jnp.einsum('ctd,csd->cts', qm, k_out[:, :, pair], preferred_element_type=F32)
        att = jnp.where(tri, att, 0.0).astype(BF16)
        o_h = jnp.einsum('cts,csv->ctv', att, vh, preferred_element_type=F32)
        vt = jnp.einsum('cvw,csw->cvs', eye, vh, preferred_element_type=F32).astype(BF16)
        ut = jnp.einsum('cvs,csd->cvd', vt, k_dec[:, :, pair], preferred_element_type=F32)
        s_h = s_scr[h]
        before = [None] * nc
        for c in order:
            before[c] = s_h
            s_h = s_h * decay[c, :, pair] + ut[c]
        s_scr[h] = s_h
        s_all = jnp.stack(before, axis=0).astype(BF16)
        o_h = o_h + jnp.einsum('ctd,cvd->ctv', qm, s_all, preferred_element_type=F32)
        outs.append(o_h.reshape(tb, GLA_DV))

    @pl.when(j == pl.num_programs(1) - 1)
    def _():
        sout_ref[0] = s_scr[...]

    if not backward:
        o_ref[0] = jnp.concatenate(outs, axis=-1)
        return
    of = of_ref[0]
    g = g_ref[0].astype(F32)
    fin = []
    for h in range(GLA_HEADS):
        sl = slice(h * GLA_DV, (h + 1) * GLA_DV)
        y = _rms(of[:, sl] + outs[h]) * gn_ref[...]
        gh = g[:, sl]
        fin.append(y * (gh * jax.nn.sigmoid(gh)))
    o_ref[0] = jnp.concatenate(fin, axis=-1).astype(BF16)


def _gla_sweep(p, wg, bg, s0, nc, col0, a_col, backward, o_fwd=None, g_norm=None):
    b, n, _ = p.shape
    tb = nc * GLA_CHUNK
    nblk = n // tb
    kw = GLA_HEADS * GLA_DK
    vw = GLA_HEADS * GLA_DV
    cq, ck, cv, cg, ca = col0 // kw, col0 // kw + 1, (col0 + 2 * kw) // vw, (col0 + 2 * kw) // vw + 1, a_col // LANES

    def blk(j):
        return nblk - 1 - j if backward else j

    in_specs = [
        pl.BlockSpec((1, tb, kw), lambda bi, j: (bi, blk(j), cq)),
        pl.BlockSpec((1, tb, kw), lambda bi, j: (bi, blk(j), ck)),
        pl.BlockSpec((1, tb, vw), lambda bi, j: (bi, blk(j), cv)),
        pl.BlockSpec((1, tb, LANES), lambda bi, j: (bi, blk(j), ca)),
        pl.BlockSpec((LANES, kw), lambda bi, j: (0, 0)),
        pl.BlockSpec((1, kw), lambda bi, j: (0, 0)),
        pl.BlockSpec((1, GLA_HEADS, GLA_DV, LANES), lambda bi, j: (bi, 0, 0, 0)),
    ]
    args = [p, p, p, p, wg, bg, s0]
    if backward:
        in_specs += [
            pl.BlockSpec((1, tb, vw), lambda bi, j: (bi, blk(j), 0)),
            pl.BlockSpec((1, tb, vw), lambda bi, j: (bi, blk(j), cg)),
            pl.BlockSpec((1, GLA_DV), lambda bi, j: (0, 0)),
        ]
        args += [o_fwd, p, g_norm]
    return pl.pallas_call(
        functools.partial(_gla_kernel, backward=backward, nc=nc),
        out_shape=(jax.ShapeDtypeStruct((b, n, vw), BF16 if backward else F32),
                   jax.ShapeDtypeStruct((b, GLA_HEADS, GLA_DV, LANES), F32)),
        grid=(b, nblk),
        in_specs=in_specs,
        out_specs=(pl.BlockSpec((1, tb, vw), lambda bi, j: (bi, blk(j), 0)),
                   pl.BlockSpec((1, GLA_HEADS, GLA_DV, LANES), lambda bi, j: (bi, 0, 0, 0))),
        scratch_shapes=[pltpu.VMEM((GLA_HEADS, GLA_DV, LANES), F32)],
        compiler_params=_params("parallel", "arbitrary"),
        name="gla_bwd" if backward else "gla_fwd",
    )(*args)


def _diff_lambda(lam_ref, lam_init):
    lp = lam_ref[...]
    return (jnp.exp(jnp.sum(lp[0:1] * lp[1:2], axis=-1, keepdims=True))
            - jnp.exp(jnp.sum(lp[2:3] * lp[3:4], axis=-1, keepdims=True)) + lam_init)


def _branch_queries(q):
    lane = lax.broadcasted_iota(jnp.int32, q.shape, 1)
    return jnp.concatenate([jnp.where(lane < DIFF_DK, q, 0.0), jnp.where(lane >= DIFF_DK, q, 0.0)], axis=0)


def _diff_ctx_kernel(q_ref, k_ref, v_ref, lam_ref, gn_ref, o_ref, *, lam_init):
    tq = q_ref.shape[1]
    s = _nt_dot(_branch_queries(q_ref[0]), k_ref[0])
    p = jnp.exp2(s - jnp.max(s, axis=-1, keepdims=True))
    a = jnp.dot(p.astype(BF16), v_ref[0], preferred_element_type=F32) / jnp.sum(p, axis=-1, keepdims=True)
    o = a[0:tq] - _diff_lambda(lam_ref, lam_init) * a[tq:2 * tq]
    o_ref[0] = (_rms(o) * gn_ref[...] * (1.0 - lam_init)).astype(BF16)


def _diff_attention_ctx(qk_c, p_c, lam_params, g_norm, lam_init, v_col0):
    b, n, w = qk_c.shape
    heads = w // (2 * LANES)
    vb = v_col0 // LANES
    return pl.pallas_call(
        functools.partial(_diff_ctx_kernel, lam_init=lam_init),
        out_shape=jax.ShapeDtypeStruct((b, n, heads * DIFF_DV), BF16),
        grid=(b, heads),
        in_specs=[
            pl.BlockSpec((1, n, LANES), lambda bi, h: (bi, 0, h)),
            pl.BlockSpec((1, n, LANES), lambda bi, h: (bi, 0, heads + h)),
            pl.BlockSpec((1, n, LANES), lambda bi, h: (bi, 0, vb + h)),
            pl.BlockSpec((4, DIFF_DK), lambda bi, h: (0, 0)),
            pl.BlockSpec((1, DIFF_DV), lambda bi, h: (0, 0)),
        ],
        out_specs=pl.BlockSpec((1, n, LANES), lambda bi, h: (bi, 0, h)),
        compiler_params=_params("parallel", "parallel"),
        name="diff_attention_ctx",
    )(qk_c, qk_c, p_c, lam_params, g_norm)


def _diff_kernel(q_ref, k_ref, v_ref, kc_ref, vc_ref, lam_ref, gn_ref, o_ref,
                 qs, vt, vct, s_buf, c_buf, p_buf, a_buf, m_scr, acc, *, nch, lam_init):
    tq = q_ref.shape[1]
    tkc = k_ref.shape[1] // nch

    def transposed_with_ones(v):
        pad_rows = lax.broadcasted_iota(jnp.int32, (DIFF_VT_ROWS - DIFF_DV, v.shape[0]), 0)
        return jnp.concatenate([v.astype(F32).T.astype(BF16), jnp.where(pad_rows == 0, 1.0, 0.0).astype(BF16)],
                               axis=0)

    @pl.when(pl.program_id(2) == 0)
    def _():
        for c in range(nch):
            vt[c] = transposed_with_ones(v_ref[0, c * tkc:(c + 1) * tkc, :])
        vct[...] = transposed_with_ones(vc_ref[0])

    def stage(value_chunk, softmax_on, score_chunk):
        for n in range(2 * tq // DIFF_LANE_TILE):
            sl = slice(n * DIFF_LANE_TILE, (n + 1) * DIFF_LANE_TILE)
            if value_chunk is not None:
                pv = jnp.dot(vt[value_chunk], p_buf[:, sl], preferred_element_type=F32)
                acc[:, sl] = acc[:, sl] * a_buf[:, sl] + pv
            if softmax_on:
                m_old = m_scr[:, sl]
                m_new = jnp.maximum(m_old, c_buf[:, sl])
                a_buf[:, sl] = jnp.exp2(m_old - m_new)
                m_scr[:, sl] = m_new
                p_buf[:, sl] = jnp.exp2(s_buf[:, sl] - m_new).astype(BF16)
            if score_chunk is not None:
                k = k_ref[0, pl.ds(pl.multiple_of(score_chunk * tkc, tkc), tkc), :]
                s = _nt_dot(k, qs[sl, :])
                s_buf[:, sl] = s
                c_buf[:, sl] = jnp.max(s, axis=0, keepdims=True)

    qs[...] = _branch_queries(q_ref[0])
    sc = _nt_dot(kc_ref[0], qs[...])
    mc = jnp.max(sc, axis=0, keepdims=True)
    pc = jnp.exp2(sc - mc)
    m_scr[...] = mc
    acc[...] = jnp.dot(vct[...], pc.astype(BF16), preferred_element_type=F32)
    stage(None, False, 0)
    stage(None, True, 1)

    def steady(c, carry):
        stage(c - 1, True, c + 1)
        return carry

    lax.fori_loop(1, nch - 1, steady, 0, unroll=2 if nch % 2 == 0 else 1)
    stage(nch - 2, True, None)
    stage(nch - 1, False, None)

    a = acc[0:DIFF_DV, :] / acc[DIFF_DV:DIFF_DV + 1, :]
    o = a[:, 0:tq] - _diff_lambda(lam_ref, lam_init) * a[:, tq:2 * tq]
    y = o * lax.rsqrt(jnp.mean(o * o, axis=0, keepdims=True) + EPS) * gn_ref[...] * (1.0 - lam_init)
    o_ref[0] = y.T.astype(BF16)


def _diff_attention(qk_x, p_x, qk_c, p_c, lam_params, g_norm, lam_init, tq, tkc, v_col0):
    b, t, w = qk_x.shape
    heads = w // (2 * LANES)
    n_ctx = qk_c.shape[1]
    nch = t // tkc
    vb = v_col0 // LANES
    assert nch >= 2 and nch * tkc == t
    return pl.pallas_call(
        functools.partial(_diff_kernel, nch=nch, lam_init=lam_init),
        out_shape=jax.ShapeDtypeStruct((b, t, heads * DIFF_DV), BF16),
        grid=(b, heads, t // tq),
        in_specs=[
            pl.BlockSpec((1, tq, LANES), lambda bi, h, i: (bi, i, h)),
            pl.BlockSpec((1, t, LANES), lambda bi, h, i: (bi, 0, heads + h)),
            pl.BlockSpec((1, t, LANES), lambda bi, h, i: (bi, 0, vb + h)),
            pl.BlockSpec((1, n_ctx, LANES), lambda bi, h, i: (bi, 0, heads + h)),
            pl.BlockSpec((1, n_ctx, LANES), lambda bi, h, i: (bi, 0, vb + h)),
            pl.BlockSpec((4, DIFF_DK), lambda bi, h, i: (0, 0)),
            pl.BlockSpec((DIFF_DV, 1), lambda bi, h, i: (0, 0)),
        ],
        out_specs=pl.BlockSpec((1, tq, LANES), lambda bi, h, i: (bi, i, h)),
        scratch_shapes=[
            pltpu.VMEM((2 * tq, LANES), BF16),
            pltpu.VMEM((nch, DIFF_VT_ROWS, tkc), BF16),
            pltpu.VMEM((DIFF_VT_ROWS, n_ctx), BF16),
            pltpu.VMEM((tkc, 2 * tq), F32),
            pltpu.VMEM((1, 2 * tq), F32),
            pltpu.VMEM((tkc, 2 * tq), BF16),
            pltpu.VMEM((1, 2 * tq), F32),
            pltpu.VMEM((1, 2 * tq), F32),
            pltpu.VMEM((DIFF_VT_ROWS, 2 * tq), F32),
        ],
        compiler_params=_params("parallel", "parallel", "arbitrary"),
        name="diff_attention",
    )(qk_x, qk_x, p_x, qk_c, p_c, lam_params, g_norm.reshape(DIFF_DV, 1))


def _outproj_kernel(ya_ref, yb_ref, yd_ref, w_ref, x_ref, g_ref, gate_ref, gmlp_ref, shift_ref, scale_ref,
                    o_ref, h_ref):
    wa, wb = ya_ref.shape[2], yb_ref.shape[2]
    for r in range(x_ref.shape[1] // ROW_GROUP):
        rows = slice(r * ROW_GROUP, (r + 1) * ROW_GROUP)
        y = jnp.dot(ya_ref[0, rows, :], w_ref[0:wa, :], preferred_element_type=F32)
        y = y + jnp.dot(yb_ref[0, rows, :], w_ref[wa:wa + wb, :], preferred_element_type=F32)
        y = y + jnp.dot(yd_ref[0, rows, :], w_ref[wa + wb:, :], preferred_element_type=F32)
        x_new = x_ref[0, rows, :] + gate_ref[0] * (_rms(y) * g_ref[...])
        o_ref[0, rows, :] = x_new
        h_ref[0, rows, :] = _modulate(x_new, gmlp_ref, shift_ref, scale_ref)


def _outproj(ya, yb, yd, w, x, g_post, g_mlp, mods, mod_row0, tm):
    b, t, d = x.shape

    def mod(comp):
        return pl.BlockSpec((1, 1, d), lambda bi, i: ((bi + mod_row0) * 6 + comp, 0, 0))

    row = pl.BlockSpec((1, d), lambda bi, i: (0, 0))
    tile = pl.BlockSpec((1, tm, d), lambda bi, i: (bi, i, 0))
    return pl.pallas_call(
        _outproj_kernel,
        out_shape=(jax.ShapeDtypeStruct((b, t, d), F32), jax.ShapeDtypeStruct((b, t, d), BF16)),
        grid=(b, t // tm),
        in_specs=[
            pl.BlockSpec((1, tm, ya.shape[2]), lambda bi, i: (bi, i, 0)),
            pl.BlockSpec((1, tm, yb.shape[2]), lambda bi, i: (bi, i, 0)),
            pl.BlockSpec((1, tm, yd.shape[2]), lambda bi, i: (bi, i, 0)),
            pl.BlockSpec(w.shape, lambda bi, i: (0, 0)),
            tile, row, mod(2), row, mod(3), mod(4),
        ],
        out_specs=(tile, tile),
        compiler_params=_params("parallel", "parallel"),
        name="outproj",
    )(ya, yb, yd, w, x, g_post, mods, g_mlp, mods, mods)


def _mlp_kernel(x_ref, h_ref, w1_ref, w2_ref, gpost_ref, gate_ref, *rest, emit_next):
    if emit_next:
        gnext_ref, shift_ref, scale_ref, o_ref, hnext_ref, acc = rest
    else:
        o_ref, acc = rest
    k = pl.program_id(2)

    @pl.when(k == 0)
    def _():
        acc[...] = jnp.zeros(acc.shape, F32)

    def hidden_tile(rows):
        u = jnp.maximum(jnp.dot(h_ref[0, rows, :], w1_ref[...], preferred_element_type=F32), 0.0)
        return jnp.dot((u * u).astype(BF16), w2_ref[...], preferred_element_type=F32)

    last = pl.num_programs(2) - 1

    @pl.when(k < last)
    def _():
        acc[...] += hidden_tile(slice(None))

    @pl.when(k == last)
    def _():
        for r in range(acc.shape[0] // ROW_GROUP):
            rows = slice(r * ROW_GROUP, (r + 1) * ROW_GROUP)
            y = acc[rows, :] + hidden_tile(rows)
            x_new = x_ref[0, rows, :] + gate_ref[0] * (_rms(y) * gpost_ref[...])
            o_ref[0, rows, :] = x_new
            if emit_next:
                hnext_ref[0, rows, :] = _modulate(x_new, gnext_ref, shift_ref, scale_ref)


def _mlp(x, h, gpost, mods, mod_row0, w1, w2, tm, th, g_next=None, mods_next=None):
    b, t, d = x.shape
    hid = w1.shape[1]
    emit_next = g_next is not None

    def mod(comp):
        return pl.BlockSpec((1, 1, d), lambda bi, i, k: ((bi + mod_row0) * 6 + comp, 0, 0))

    row = pl.BlockSpec((1, d), lambda bi, i, k: (0, 0))
    tile = pl.BlockSpec((1, tm, d), lambda bi, i, k: (bi, i, 0))
    in_specs = [
        tile, tile,
        pl.BlockSpec((d, th), lambda bi, i, k: (0, k)),
        pl.BlockSpec((th, d), lambda bi, i, k: (k, 0)),
        row, mod(5),
    ]
    args = [x, h, w1, w2, gpost, mods]
    out_shape = [jax.ShapeDtypeStruct((b, t, d), F32)]
    out_specs = [tile]
    if emit_next:
        in_specs += [row, mod(0), mod(1)]
        args += [g_next, mods_next, mods_next]
        out_shape.append(jax.ShapeDtypeStruct((b, t, d), BF16))
        out_specs.append(tile)
    out = pl.pallas_call(
        functools.partial(_mlp_kernel, emit_next=emit_next),
        out_shape=tuple(out_shape),
        grid=(b, t // tm, hid // th),
        in_specs=in_specs,
        out_specs=tuple(out_specs),
        scratch_shapes=[pltpu.VMEM((tm, d), F32)],
        compiler_params=_params("parallel", "parallel", "arbitrary"),
        name="mlp",
    )(*args)
    return out if emit_next else (out[0], None)


def _rope_tables(t):
    tok = jnp.arange(t, dtype=jnp.int32)
    row = (tok // GRID_W).astype(F32)
    col = (tok % GRID_W).astype(F32)
    n_freq = DIFF_DK // 4
    inv = ROPE_BASE ** (-jnp.arange(n_freq, dtype=F32) / n_freq)
    ang = jnp.concatenate([row[:, None] * inv, col[:, None] * inv], axis=-1)
    cos, sin = jnp.cos(ang), jnp.sin(ang)
    reps = LANES // DIFF_DK
    return (jnp.tile(jnp.concatenate([cos, cos], axis=-1), (1, reps)),
            jnp.tile(jnp.concatenate([-sin, sin], axis=-1), (1, reps)))


def _pick(n, pref):
    return pref if n % pref == 0 else n


def kernel(x, c, ctx, c_ctx, w_ada, b_ada, g_pre_mix, g_post_mix, g_pre_mlp, g_post_mlp, w_in, w_out,
           na_rel_bias, gla_w_gate, gla_b_gate, gla_g_norm, diff_lambda, diff_g_norm, w_mlp_in, w_mlp_out):
    bsz, t, d = x.shape
    n_ctx = ctx.shape[1]
    depth = w_ada.shape[0]
    na_heads = na_rel_bias.shape[1]
    na_w = na_heads * HEAD_DIM
    gla_kw = GLA_HEADS * GLA_DK
    gla_vw = GLA_HEADS * GLA_DV
    diff_w = w_out.shape[1] - na_w - gla_vw
    diff_heads = diff_w // DIFF_DV
    diff_qk = diff_heads * 2 * DIFF_DK
    rows = t // GRID_W
    assert t % (NA_ROWS_PER_STEP * GRID_W) == 0 and rows >= NA_KEY_ROWS
    assert bsz + 1 <= 8

    o_gla = 3 * na_w
    o_gate = o_gla + 2 * gla_kw + 2 * gla_vw
    o_diff = o_gate + 2 * GLA_GATE_RANK
    gla_col0 = 3 * na_w
    diffv_col0 = gla_col0 + 2 * gla_kw + 2 * gla_vw
    gate_col0 = diffv_col0 + diff_w

    cc = jnp.zeros((8, d), F32).at[:bsz].set(c).at[bsz].set(c_ctx)
    mods_all = _mods(cc, w_ada, b_ada)
    rope_tabs = _rope_tables(t)
    xc = ctx.reshape(1, bsz * n_ctx, d)
    zeros_state = jnp.zeros((bsz, GLA_HEADS, GLA_DV, LANES), F32)

    tm_in = _pick(t, 1024)
    tm_out = _pick(t, 512)
    tm_ctx = _pick(bsz * n_ctx, 512)
    tq = _pick(t, 512)
    tkc = min(2048, t // 2)
    nc_lat = _pick(t // GLA_CHUNK, 16)
    nc_ctx = _pick(n_ctx // GLA_CHUNK, 4)

    mods_of = [mods_all[l].reshape(8 * 6, 1, d) for l in range(depth)]
    h_x = _premix(x, g_pre_mix[0].reshape(1, d), mods_of[0], 0, tm_out)
    h_c = _premix(xc, g_pre_mix[0].reshape(1, d), mods_of[0], bsz, tm_ctx)

    for l in range(depth):
        need_ctx = l < depth - 1
        mods = mods_of[l]
        g_next = g_pre_mix[l + 1].reshape(1, d) if need_ctx else None
        mods_next = mods_of[l + 1] if need_ctx else None
        wl = w_in[l].astype(BF16)
        w_qk = wl[:, o_diff:o_diff + 2 * diff_qk]
        w_rest = jnp.concatenate(
            [wl[:, :o_gate], wl[:, o_diff + 2 * diff_qk:], wl[:, o_gate:o_diff],
             jnp.zeros((d, LANES - 2 * GLA_GATE_RANK), BF16)], axis=1)
        w_o = w_out[l].astype(BF16)
        w1 = w_mlp_in[l].astype(BF16)
        w2 = w_mlp_out[l].astype(BF16)
        g2 = g_post_mix[l].reshape(1, d)
        g3 = g_pre_mlp[l].reshape(1, d)
        g4 = g_post_mlp[l].reshape(1, d)
        wg = jnp.zeros((2, LANES, gla_kw), F32)
        wg = wg.at[0, :GLA_GATE_RANK].set(gla_w_gate[l, 0]).at[1, GLA_GATE_RANK:2 * GLA_GATE_RANK].set(gla_w_gate[l, 1])
        wg = wg.astype(BF16)
        bg = gla_b_gate[l].reshape(2, 1, gla_kw)
        gla_gn = gla_g_norm[l].reshape(1, GLA_DV)
        diff_gn = diff_g_norm[l].reshape(1, DIFF_DV)
        lam_init = 0.8 - 0.6 * math.exp(-0.3 * l)

        qk_x = _inproj(h_x, w_qk, tm_in, diff_qk, rope_tabs, 1, DIFF_Q_SCALE)
        p_x = _inproj(h_x, w_rest, tm_in, w_rest.shape[1] // 3)
        qk_c = _inproj(h_c, w_qk, tm_ctx, diff_qk, None, 1, DIFF_Q_SCALE).reshape(bsz, n_ctx, -1)
        p_c = _inproj(h_c, w_rest, tm_ctx, w_rest.shape[1] // 3).reshape(bsz, n_ctx, -1)

        ya = _na_attention(p_x, p_c, _na_bias_table(na_rel_bias[l], rows))

        of_c, s_f = _gla_sweep(p_c, wg[0], bg[0], zeros_state, nc_ctx, gla_col0, gate_col0, False)
        yb_c, s_b = _gla_sweep(p_c, wg[1], bg[1], zeros_state, nc_ctx, gla_col0, gate_col0, True, of_c, gla_gn)
        of_x, _ = _gla_sweep(p_x, wg[0], bg[0], s_f, nc_lat, gla_col0, gate_col0, False)
        yb, _ = _gla_sweep(p_x, wg[1], bg[1], s_b, nc_lat, gla_col0, gate_col0, True, of_x, gla_gn)

        yd = _diff_attention(qk_x, p_x, qk_c, p_c, diff_lambda[l], diff_gn, lam_init, tq, tkc, diffv_col0)

        x, hm_x = _outproj(ya, yb, yd, w_o, x, g2, g3, mods, 0, tm_out)
        x, h_x = _mlp(x, hm_x, g4, mods, 0, w1, w2, tm_out, MLP_HIDDEN_TILE, g_next, mods_next)

        if need_ctx:
            ya_c = _ctx_attention(p_c, na_heads // 2)
            yd_c = _diff_attention_ctx(qk_c, p_c, diff_lambda[l], diff_gn, lam_init, diffv_col0)
            flat = lambda z: z.reshape(1, bsz * n_ctx, -1)
            xc, hm_c = _outproj(flat(ya_c), flat(yb_c), flat(yd_c), w_o, xc, g2, g3, mods, bsz, tm_ctx)
            xc, h_c = _mlp(xc, hm_c, g4, mods, bsz, w1, w2, tm_ctx, MLP_HIDDEN_TILE, g_next, mods_next)
    return x
```

```python
import functools
import math

import jax
import jax.numpy as jnp
from jax import lax
from jax.experimental import pallas as pl
from jax.experimental.pallas import tpu as pltpu

F32 = jnp.float32
BF16 = jnp.bfloat16

GRID_W = 64
EPS = 1e-6
NEG_INF = -1e30
HEAD_DIM = 64
NA_WIN_R = 8
NA_WIN_C = 16
GLA_HEADS = 4
GLA_DK = 64
GLA_DV = 128
GLA_GATE_RANK = 16
GLA_NORMALIZER = 16.0
GLA_CHUNK = 64
DIFF_DK = 64
DIFF_DV = 128
ROPE_BASE = 10000.0
DIFF_Q_SCALE = DIFF_DK ** -0.5 * math.log2(math.e)

LANES = 128
DIFF_LANE_TILE = 1024
DIFF_VT_ROWS = 128 + 16
VMEM_LIMIT = 56 * 1024 * 1024

MLP_HIDDEN_TILE = 1024
ROW_GROUP = 256
NA_ROWS_PER_STEP = 8
NA_KEY_ROWS = NA_ROWS_PER_STEP + NA_WIN_R


def _params(*sem):
    return pltpu.CompilerParams(dimension_semantics=sem, vmem_limit_bytes=VMEM_LIMIT)


def _nt_dot(a, b):
    return lax.dot_general(a, b, (((1,), (1,)), ((), ())), preferred_element_type=F32)


def _rms(y):
    return y * lax.rsqrt(jnp.mean(y * y, axis=-1, keepdims=True) + EPS)


def _mods_kernel(c_ref, w_ref, b_ref, o_ref):
    c = c_ref[...]
    s = (c * jax.nn.sigmoid(c)).astype(BF16)
    o_ref[0] = jnp.dot(s, w_ref[0].astype(BF16), preferred_element_type=F32) + b_ref[0]


def _mods(cc, w_ada, b_ada):
    depth, d, n = w_ada.shape
    tn = 1024
    return pl.pallas_call(
        _mods_kernel,
        out_shape=jax.ShapeDtypeStruct((depth, 8, n), F32),
        grid=(depth, n // tn),
        in_specs=[
            pl.BlockSpec((8, d), lambda l, j: (0, 0)),
            pl.BlockSpec((1, d, tn), lambda l, j: (l, 0, j)),
            pl.BlockSpec((1, 1, tn), lambda l, j: (l, 0, j)),
        ],
        out_specs=pl.BlockSpec((1, 8, tn), lambda l, j: (l, 0, j)),
        compiler_params=_params("arbitrary", "arbitrary"),
        name="ada_mods",
    )(cc, w_ada, b_ada.reshape(depth, 1, n))


def _modulate(x, g_ref, shift_ref, scale_ref):
    return (_rms(x) * g_ref[...] * (1.0 + scale_ref[0]) + shift_ref[0]).astype(BF16)


def _premix_kernel(x_ref, g_ref, shift_ref, scale_ref, o_ref):
    o_ref[0] = _modulate(x_ref[0], g_ref, shift_ref, scale_ref)


def _premix(x, g, mods, mod_row0, tm):
    b, t, d = x.shape
    return pl.pallas_call(
        _premix_kernel,
        out_shape=jax.ShapeDtypeStruct((b, t, d), BF16),
        grid=(b, t // tm),
        in_specs=[
            pl.BlockSpec((1, tm, d), lambda bi, i: (bi, i, 0)),
            pl.BlockSpec((1, d), lambda bi, i: (0, 0)),
            pl.BlockSpec((1, 1, d), lambda bi, i: ((bi + mod_row0) * 6 + 0, 0, 0)),
            pl.BlockSpec((1, 1, d), lambda bi, i: ((bi + mod_row0) * 6 + 1, 0, 0)),
        ],
        out_specs=pl.BlockSpec((1, tm, d), lambda bi, i: (bi, i, 0)),
        compiler_params=_params("parallel", "parallel"),
        name="premix",
    )(x, g, mods, mods)


def _inproj_kernel(h_ref, w_ref, *rest, rope, q_tiles, q_scale):
    if rope:
        cos_ref, sin_ref, o_ref = rest
    else:
        (o_ref,) = rest
    for r in range(h_ref.shape[1] // ROW_GROUP):
        rows = slice(r * ROW_GROUP, (r + 1) * ROW_GROUP)
        acc = jnp.dot(h_ref[0, rows, :], w_ref[...], preferred_element_type=F32)
        if q_tiles:
            acc = acc * jnp.where(pl.program_id(2) < q_tiles, q_scale, 1.0)
        if not rope:
            o_ref[0, rows, :] = acc.astype(BF16)
            continue
        cos = cos_ref[rows, :]
        sin = sin_ref[rows, :]
        lower = (lax.broadcasted_iota(jnp.int32, cos.shape, 1) % DIFF_DK) < DIFF_DK // 2
        for c in range(acc.shape[1] // LANES):
            a = acc[:, c * LANES:(c + 1) * LANES]
            partner = jnp.where(lower, pltpu.roll(a, LANES - DIFF_DK // 2, 1), pltpu.roll(a, DIFF_DK // 2, 1))
            o_ref[0, rows, c * LANES:(c + 1) * LANES] = (a * cos + partner * sin).astype(BF16)


def _inproj(h, w, tm, tn, rope_tabs=None, q_tiles=0, q_scale=1.0):
    b, t, d = h.shape
    n = w.shape[1]
    rope = rope_tabs is not None
    in_specs = [
        pl.BlockSpec((1, tm, d), lambda bi, i, j: (bi, i, 0)),
        pl.BlockSpec((d, tn), lambda bi, i, j: (0, j)),
    ]
    args = [h, w]
    if rope:
        in_specs += [pl.BlockSpec((tm, LANES), lambda bi, i, j: (i, 0))] * 2
        args += list(rope_tabs)
    return pl.pallas_call(
        functools.partial(_inproj_kernel, rope=rope, q_tiles=q_tiles, q_scale=q_scale),
        out_shape=jax.ShapeDtypeStruct((b, t, n), BF16),
        grid=(b, t // tm, n // tn),
        in_specs=in_specs,
        out_specs=pl.BlockSpec((1, tm, tn), lambda bi, i, j: (bi, i, j)),
        compiler_params=_params("parallel", "parallel", "parallel"),
        name="inproj_rope" if rope else "inproj",
    )(*args)


def _na_bias_table(rel_bias, rows):
    rq, kwin = NA_ROWS_PER_STEP, NA_KEY_ROWS
    nb = rows // rq
    h = rel_bias.shape[0]
    blocks = jnp.array([0, min(1, nb - 1), nb - 1], jnp.int32)
    r = blocks[:, None] * rq + jnp.arange(rq, dtype=jnp.int32)[None, :]
    kst = jnp.clip(blocks * rq - NA_WIN_R // 2, 0, rows - kwin)
    kr = kst[:, None] + jnp.arange(kwin, dtype=jnp.int32)[None, :]
    s_r = jnp.clip(r - NA_WIN_R // 2, 0, rows - NA_WIN_R)
    row_ok = (kr[:, None, :] >= s_r[:, :, None]) & (kr[:, None, :] < s_r[:, :, None] + NA_WIN_R)
    dr = jnp.clip(kr[:, None, :] - r[:, :, None] + (NA_WIN_R - 1), 0, 2 * NA_WIN_R - 2)
    col = jnp.arange(GRID_W, dtype=jnp.int32)
    c0 = jnp.clip(col - NA_WIN_C // 2, 0, GRID_W - NA_WIN_C)
    col_ok = (col[None, :] >= c0[:, None]) & (col[None, :] < c0[:, None] + NA_WIN_C)
    dc = jnp.clip(col[None, :] - col[:, None] + (NA_WIN_C - 1), 0, 2 * NA_WIN_C - 2)
    pick_dc = (dc[:, :, None] == jnp.arange(2 * NA_WIN_C - 1)[None, None, :]).astype(F32)
    pick_dr = (dr[..., None] == jnp.arange(2 * NA_WIN_R - 1)[None, None, None, :]).astype(F32)
    by_col = jnp.einsum('hrd,qkd->hrqk', rel_bias.astype(F32), pick_dc, precision=lax.Precision.HIGHEST)
    bias = jnp.einsum('vqlr,hrab->vhqalb', pick_dr, by_col, precision=lax.Precision.HIGHEST)
    ok = row_ok[:, None, :, None, :, None] & col_ok[None, None, None, :, None, :]
    bias = jnp.where(ok, bias, NEG_INF)
    return bias.reshape(3, h, rq * GRID_W, kwin * GRID_W)


def _na_kernel(q_ref, qn_ref, k_ref, v_ref, kc_ref, vc_ref, bias_ref, o_ref, s_loc, s_ctx, *, rows):
    i = pl.program_id(2)
    window = NA_KEY_ROWS * GRID_W
    kc = kc_ref[0]
    vc = vc_ref[0]
    lane = lax.broadcasted_iota(jnp.int32, q_ref.shape[1:], 1)

    def window_start(step):
        kstart = jnp.clip(step * NA_ROWS_PER_STEP - NA_WIN_R // 2, 0, rows - NA_KEY_ROWS)
        return pl.multiple_of(kstart * GRID_W, GRID_W)

    def raw_scores(q, step, hh):
        own = (lane >= hh * HEAD_DIM) & (lane < (hh + 1) * HEAD_DIM)
        qh = jnp.where(own, q * (HEAD_DIM ** -0.5), 0.0).astype(BF16)
        s_loc[hh] = _nt_dot(qh, k_ref[0, pl.ds(window_start(step), window), :])
        s_ctx[hh] = _nt_dot(qh, kc)

    @pl.when(i == 0)
    def _():
        for hh in range(LANES // HEAD_DIM):
            raw_scores(q_ref[0], 0, hh)

    following = jnp.minimum(i + 1, pl.num_programs(2) - 1)
    vw = v_ref[0, pl.ds(window_start(i), window), :]
    outs = []
    for hh in range(LANES // HEAD_DIM):
        sl = s_loc[hh] + bias_ref[0, hh]
        sc = s_ctx[hh]
        m = jnp.maximum(jnp.max(sl, axis=-1, keepdims=True), jnp.max(sc, axis=-1, keepdims=True))
        p_loc = jnp.exp(sl - m)
        p_ctx = jnp.exp(sc - m)
        l = jnp.sum(p_loc, axis=-1, keepdims=True) + jnp.sum(p_ctx, axis=-1, keepdims=True)
        raw_scores(qn_ref[0], following, hh)
        o = jnp.dot(p_loc.astype(BF16), vw, preferred_element_type=F32)
        o = o + jnp.dot(p_ctx.astype(BF16), vc, preferred_element_type=F32)
        outs.append(o / l)
    o_ref[0] = jnp.where(lane < HEAD_DIM, outs[0], outs[1]).astype(BF16)


def _na_attention(p_lat, p_ctx, bias_tbl):
    b, t, _ = p_lat.shape
    n = p_ctx.shape[1]
    rows = t // GRID_W
    nb = rows // NA_ROWS_PER_STEP
    hp = bias_tbl.shape[1] // 2
    tq = NA_ROWS_PER_STEP * GRID_W

    def variant(i):
        return jnp.where(i == 0, 0, jnp.where(i == nb - 1, 2, 1))

    return pl.pallas_call(
        functools.partial(_na_kernel, rows=rows),
        out_shape=jax.ShapeDtypeStruct((b, t, hp * LANES), BF16),
        grid=(b, hp, nb),
        in_specs=[
            pl.BlockSpec((1, tq, LANES), lambda bi, h, i: (bi, i, h)),
            pl.BlockSpec((1, tq, LANES), lambda bi, h, i: (bi, jnp.minimum(i + 1, nb - 1), h)),
            pl.BlockSpec((1, t, LANES), lambda bi, h, i: (bi, 0, hp + h)),
            pl.BlockSpec((1, t, LANES), lambda bi, h, i: (bi, 0, 2 * hp + h)),
            pl.BlockSpec((1, n, LANES), lambda bi, h, i: (bi, 0, hp + h)),
            pl.BlockSpec((1, n, LANES), lambda bi, h, i: (bi, 0, 2 * hp + h)),
            pl.BlockSpec((1, 2, tq, NA_KEY_ROWS * GRID_W), lambda bi, h, i: (variant(i), h, 0, 0)),
        ],
        out_specs=pl.BlockSpec((1, tq, LANES), lambda bi, h, i: (bi, i, h)),
        scratch_shapes=[
            pltpu.VMEM((LANES // HEAD_DIM, tq, NA_KEY_ROWS * GRID_W), F32),
            pltpu.VMEM((LANES // HEAD_DIM, tq, n), F32),
        ],
        compiler_params=_params("parallel", "parallel", "arbitrary"),
        name="na_attention",
    )(p_lat, p_lat, p_lat, p_lat, p_ctx, p_ctx, bias_tbl)


def _ctx_attn_kernel(q_ref, k_ref, v_ref, o_ref):
    q, k, v = q_ref[0], k_ref[0], v_ref[0]
    scale = HEAD_DIM ** -0.5
    lane = lax.broadcasted_iota(jnp.int32, q.shape, 1)
    outs = []
    for hh in range(LANES // HEAD_DIM):
        own = (lane >= hh * HEAD_DIM) & (lane < (hh + 1) * HEAD_DIM)
        s = _nt_dot(jnp.where(own, q, 0.0).astype(BF16), k) * scale
        p = jnp.exp(s - jnp.max(s, axis=-1, keepdims=True))
        l = jnp.sum(p, axis=-1, keepdims=True)
        outs.append(jnp.dot(p.astype(BF16), v, preferred_element_type=F32) / l)
    o_ref[0] = jnp.where(lane < HEAD_DIM, outs[0], outs[1]).astype(BF16)


def _ctx_attention(p_ctx, hp):
    b, n, _ = p_ctx.shape
    return pl.pallas_call(
        _ctx_attn_kernel,
        out_shape=jax.ShapeDtypeStruct((b, n, hp * LANES), BF16),
        grid=(b, hp),
        in_specs=[
            pl.BlockSpec((1, n, LANES), lambda bi, h: (bi, 0, h)),
            pl.BlockSpec((1, n, LANES), lambda bi, h: (bi, 0, hp + h)),
            pl.BlockSpec((1, n, LANES), lambda bi, h: (bi, 0, 2 * hp + h)),
        ],
        out_specs=pl.BlockSpec((1, n, LANES), lambda bi, h: (bi, 0, h)),
        compiler_params=_params("parallel", "parallel"),
        name="ctx_attention",
    )(p_ctx, p_ctx, p_ctx)


def _gla_kernel(*refs, backward, nc):
    if backward:
        (q_ref, k_ref, v_ref, a_ref, wg_ref, bg_ref, s0_ref, of_ref, g_ref, gn_ref,
         o_ref, sout_ref, s_scr) = refs
    else:
        q_ref, k_ref, v_ref, a_ref, wg_ref, bg_ref, s0_ref, o_ref, sout_ref, s_scr = refs
    j = pl.program_id(1)
    tb = nc * GLA_CHUNK
    kw = GLA_HEADS * GLA_DK

    @pl.when(j == 0)
    def _():
        s_scr[...] = s0_ref[0]

    pre = jnp.dot(a_ref[0], wg_ref[...], preferred_element_type=F32) + bg_ref[...]
    log_a = (jnp.minimum(pre, 0.0) - jnp.log1p(jnp.exp(-jnp.abs(pre)))) * (1.0 / GLA_NORMALIZER)

    pos = lax.broadcasted_iota(jnp.int32, (tb, kw), 0) % GLA_CHUNK
    bsum = log_a
    step = 1
    while step < GLA_CHUNK:
        if backward:
            bsum = bsum + jnp.where(pos < GLA_CHUNK - step, pltpu.roll(bsum, tb - step, 0), 0.0)
        else:
            bsum = bsum + jnp.where(pos >= step, pltpu.roll(bsum, step, 0), 0.0)
        step *= 2

    b3 = bsum.reshape(nc, GLA_CHUNK, kw)
    b_end = b3[:, 0:1, :] if backward else b3[:, GLA_CHUNK - 1:GLA_CHUNK, :]
    q3 = q_ref[0].astype(F32).reshape(nc, GLA_CHUNK, kw)
    k3 = k_ref[0].astype(F32).reshape(nc, GLA_CHUNK, kw)
    q_in = q3 * (GLA_DK ** -0.5) * jnp.exp(b3)
    k_out = (k3 * jnp.exp(-b3)).astype(BF16)
    k_dec = (k3 * jnp.exp(b_end - b3)).astype(BF16)
    decay = jnp.exp(b_end)
    v3 = v_ref[0].reshape(nc, GLA_CHUNK, GLA_HEADS * GLA_DV)

    t_idx = lax.broadcasted_iota(jnp.int32, (nc, GLA_CHUNK, GLA_CHUNK), 1)
    s_idx = lax.broadcasted_iota(jnp.int32, (nc, GLA_CHUNK, GLA_CHUNK), 2)
    tri = (s_idx > t_idx) if backward else (s_idx <= t_idx)
    lane = lax.broadcasted_iota(jnp.int32, (nc, GLA_CHUNK, LANES), 2)
    order = range(nc - 1, -1, -1) if backward else range(nc)

    outs = []
    for h in range(GLA_HEADS):
        pair = slice((h // 2) * LANES, (h // 2 + 1) * LANES)
        own = (lane >= GLA_DK) if h % 2 else (lane < GLA_DK)
        qm = jnp.where(own, q_in[:, :, pair], 0.0).astype(BF16)
        vh = v3[:, :, h * GLA_DV:(h + 1) * GLA_DV]
        att = jnp.einsum('ctd,csd->cts', qm, k_out[:, :, pair], preferred_element_type=F32)
        att = jnp.where(tri, att, 0.0).astype(BF16)
        o_h = jnp.einsum('cts,csv->ctv', att, vh, preferred_element_type=F32)
        ut = jnp.einsum('csv,csd->cvd', vh, k_dec[:, :, pair], preferred_element_type=F32)
        s_h = s_scr[h]
        before = [None] * nc
        for c in order:
            before[c] = s_h
            s_h = s_h * decay[c, :, pair] + ut[c]
        s_scr[h] = s_h
        s_all = jnp.stack(before, axis=0).astype(BF16)
        o_h = o_h + jnp.einsum('ctd,cvd->ctv', qm, s_all, preferred_element_type=F32)
        outs.append(o_h.reshape(tb, GLA_DV))

    @pl.when(j == pl.num_programs(1) - 1)
    def _():
        sout_ref[0] = s_scr[...]

    if not backward:
        o_ref[0] = jnp.concatenate(outs, axis=-1)
        return
    of = of_ref[0]
    g = g_ref[0].astype(F32)
    fin = []
    for h in range(GLA_HEADS):
        sl = slice(h * GLA_DV, (h + 1) * GLA_DV)
        y = _rms(of[:, sl] + outs[h]) * gn_ref[...]
        gh = g[:, sl]
        fin.append(y * (gh * jax.nn.sigmoid(gh)))
    o_ref[0] = jnp.concatenate(fin, axis=-1).astype(BF16)


def _gla_sweep(p, wg, bg, s0, nc, col0, a_col, backward, o_fwd=None, g_norm=None):
    b, n, _ = p.shape
    tb = nc * GLA_CHUNK
    nblk = n // tb
    kw = GLA_HEADS * GLA_DK
    vw = GLA_HEADS * GLA_DV
    cq, ck, cv, cg, ca = col0 // kw, col0 // kw + 1, (col0 + 2 * kw) // vw, (col0 + 2 * kw) // vw + 1, a_col // LANES

    def blk(j):
        return nblk - 1 - j if backward else j

    in_specs = [
        pl.BlockSpec((1, tb, kw), lambda bi, j: (bi, blk(j), cq)),
        pl.BlockSpec((1, tb, kw), lambda bi, j: (bi, blk(j), ck)),
        pl.BlockSpec((1, tb, vw), lambda bi, j: (bi, blk(j), cv)),
        pl.BlockSpec((1, tb, LANES), lambda bi, j: (bi, blk(j), ca)),
        pl.BlockSpec((LANES, kw), lambda bi, j: (0, 0)),
        pl.BlockSpec((1, kw), lambda bi, j: (0, 0)),
        pl.BlockSpec((1, GLA_HEADS, GLA_DV, LANES), lambda bi, j: (bi, 0, 0, 0)),
    ]
    args = [p, p, p, p, wg, bg, s0]
    if backward:
        in_specs += [
            pl.BlockSpec((1, tb, vw), lambda bi, j: (bi, blk(j), 0)),
            pl.BlockSpec((1, tb, vw), lambda bi, j: (bi, blk(j), cg)),
            pl.BlockSpec((1, GLA_DV), lambda bi, j: (0, 0)),
        ]
        args += [o_fwd, p, g_norm]
    return pl.pallas_call(
        functools.partial(_gla_kernel, backward=backward, nc=nc),
        out_shape=(jax.ShapeDtypeStruct((b, n, vw), BF16 if backward else F32),
                   jax.ShapeDtypeStruct((b, GLA_HEADS, GLA_DV, LANES), F32)),
        grid=(b, nblk),
        in_specs=in_specs,
        out_specs=(pl.BlockSpec((1, tb, vw), lambda bi, j: (bi, blk(j), 0)),
                   pl.BlockSpec((1, GLA_HEADS, GLA_DV, LANES), lambda bi, j: (bi, 0, 0, 0))),
        scratch_shapes=[pltpu.VMEM((GLA_HEADS, GLA_DV, LANES), F32)],
        compiler_params=_params("parallel", "arbitrary"),
        name="gla_bwd" if backward else "gla_fwd",
    )(*args)


def _diff_lambda(lam_ref, lam_init):
    lp = lam_ref[...]
    return (jnp.exp(jnp.sum(lp[0:1] * lp[1:2], axis=-1, keepdims=True))
            - jnp.exp(jnp.sum(lp[2:3] * lp[3:4], axis=-1, keepdims=True)) + lam_init)


def _branch_queries(q):
    lane = lax.broadcasted_iota(jnp.int32, q.shape, 1)
    return jnp.concatenate([jnp.where(lane < DIFF_DK, q, 0.0), jnp.where(lane >= DIFF_DK, q, 0.0)], axis=0)


def _diff_ctx_kernel(q_ref, k_ref, v_ref, lam_ref, gn_ref, o_ref, *, lam_init):
    tq = q_ref.shape[1]
    s = _nt_dot(_branch_queries(q_ref[0]), k_ref[0])
    p = jnp.exp2(s - jnp.max(s, axis=-1, keepdims=True))
    a = jnp.dot(p.astype(BF16), v_ref[0], preferred_element_type=F32) / jnp.sum(p, axis=-1, keepdims=True)
    o = a[0:tq] - _diff_lambda(lam_ref, lam_init) * a[tq:2 * tq]
    o_ref[0] = (_rms(o) * gn_ref[...] * (1.0 - lam_init)).astype(BF16)


def _diff_attention_ctx(qk_c, p_c, lam_params, g_norm, lam_init, v_col0):
    b, n, w = qk_c.shape
    heads = w // (2 * LANES)
    vb = v_col0 // LANES
    return pl.pallas_call(
        functools.partial(_diff_ctx_kernel, lam_init=lam_init),
        out_shape=jax.ShapeDtypeStruct((b, n, heads * DIFF_DV), BF16),
        grid=(b, heads),
        in_specs=[
            pl.BlockSpec((1, n, LANES), lambda bi, h: (bi, 0, h)),
            pl.BlockSpec((1, n, LANES), lambda bi, h: (bi, 0, heads + h)),
            pl.BlockSpec((1, n, LANES), lambda bi, h: (bi, 0, vb + h)),
            pl.BlockSpec((4, DIFF_DK), lambda bi, h: (0, 0)),
            pl.BlockSpec((1, DIFF_DV), lambda bi, h: (0, 0)),
        ],
        out_specs=pl.BlockSpec((1, n, LANES), lambda bi, h: (bi, 0, h)),
        compiler_params=_params("parallel", "parallel"),
        name="diff_attention_ctx",
    )(qk_c, qk_c, p_c, lam_params, g_norm)


def _diff_kernel(q_ref, k_ref, v_ref, kc_ref, vc_ref, lam_ref, gn_ref, o_ref,
                 qs, vt, vct, s_buf, c_buf, p_buf, a_buf, m_scr, acc, *, nch, lam_init):
    tq = q_ref.shape[1]
    tkc = k_ref.shape[1] // nch

    def transposed_with_ones(v):
        pad_rows = lax.broadcasted_iota(jnp.int32, (DIFF_VT_ROWS - DIFF_DV, v.shape[0]), 0)
        return jnp.concatenate([v.astype(F32).T.astype(BF16), jnp.where(pad_rows == 0, 1.0, 0.0).astype(BF16)],
                               axis=0)

    @pl.when(pl.program_id(2) == 0)
    def _():
        for c in range(nch):
            vt[c] = transposed_with_ones(v_ref[0, c * tkc:(c + 1) * tkc, :])
        vct[...] = transposed_with_ones(vc_ref[0])

    def stage(value_chunk, softmax_on, score_chunk):
        for n in range(2 * tq // DIFF_LANE_TILE):
            sl = slice(n * DIFF_LANE_TILE, (n + 1) * DIFF_LANE_TILE)
            if value_chunk is not None:
                pv = jnp.dot(vt[value_chunk], p_buf[:, sl], preferred_element_type=F32)
                acc[:, sl] = acc[:, sl] * a_buf[:, sl] + pv
            if softmax_on:
                m_old = m_scr[:, sl]
                m_new = jnp.maximum(m_old, c_buf[:, sl])
                a_buf[:, sl] = jnp.exp2(m_old - m_new)
                m_scr[:, sl] = m_new
                p_buf[:, sl] = jnp.exp2(s_buf[:, sl] - m_new).astype(BF16)
            if score_chunk is not None:
                k = k_ref[0, pl.ds(pl.multiple_of(score_chunk * tkc, tkc), tkc), :]
                s = _nt_dot(k, qs[sl, :])
                s_buf[:, sl] = s
                c_buf[:, sl] = jnp.max(s, axis=0, keepdims=True)

    qs[...] = _branch_queries(q_ref[0])
    sc = _nt_dot(kc_ref[0], qs[...])
    mc = jnp.max(sc, axis=0, keepdims=True)
    pc = jnp.exp2(sc - mc)
    m_scr[...] = mc
    acc[...] = jnp.dot(vct[...], pc.astype(BF16), preferred_element_type=F32)
    stage(None, False, 0)
    stage(None, True, 1)

    def steady(c, carry):
        stage(c - 1, True, c + 1)
        return carry

    lax.fori_loop(1, nch - 1, steady, 0, unroll=2 if nch % 2 == 0 else 1)
    stage(nch - 2, True, None)
    stage(nch - 1, False, None)

    a = acc[0:DIFF_DV, :] / acc[DIFF_DV:DIFF_DV + 1, :]
    o = a[:, 0:tq] - _diff_lambda(lam_ref, lam_init) * a[:, tq:2 * tq]
    y = o * lax.rsqrt(jnp.mean(o * o, axis=0, keepdims=True) + EPS) * gn_ref[...] * (1.0 - lam_init)
    o_ref[0] = y.T.astype(BF16)


def _diff_attention(qk_x, p_x, qk_c, p_c, lam_params, g_norm, lam_init, tq, tkc, v_col0):
    b, t, w = qk_x.shape
    heads = w // (2 * LANES)
    n_ctx = qk_c.shape[1]
    nch = t // tkc
    vb = v_col0 // LANES
    assert nch >= 2 and nch * tkc == t
    return pl.pallas_call(
        functools.partial(_diff_kernel, nch=nch, lam_init=lam_init),
        out_shape=jax.ShapeDtypeStruct((b, t, heads * DIFF_DV), BF16),
        grid=(b, heads, t // tq),
        in_specs=[
            pl.BlockSpec((1, tq, LANES), lambda bi, h, i: (bi, i, h)),
            pl.BlockSpec((1, t, LANES), lambda bi, h, i: (bi, 0, heads + h)),
            pl.BlockSpec((1, t, LANES), lambda bi, h, i: (bi, 0, vb + h)),
            pl.BlockSpec((1, n_ctx, LANES), lambda bi, h, i: (bi, 0, heads + h)),
            pl.BlockSpec((1, n_ctx, LANES), lambda bi, h, i: (bi, 0, vb + h)),
            pl.BlockSpec((4, DIFF_DK), lambda bi, h, i: (0, 0)),
            pl.BlockSpec((DIFF_DV, 1), lambda bi, h, i: (0, 0)),
        ],
        out_specs=pl.BlockSpec((1, tq, LANES), lambda bi, h, i: (bi, i, h)),
        scratch_shapes=[
            pltpu.VMEM((2 * tq, LANES), BF16),
            pltpu.VMEM((nch, DIFF_VT_ROWS, tkc), BF16),
            pltpu.VMEM((DIFF_VT_ROWS, n_ctx), BF16),
            pltpu.VMEM((tkc, 2 * tq), F32),
            pltpu.VMEM((1, 2 * tq), F32),
            pltpu.VMEM((tkc, 2 * tq), BF16),
            pltpu.VMEM((1, 2 * tq), F32),
            pltpu.VMEM((1, 2 * tq), F32),
            pltpu.VMEM((DIFF_VT_ROWS, 2 * tq), F32),
        ],
        compiler_params=_params("parallel", "parallel", "arbitrary"),
        name="diff_attention",
    )(qk_x, qk_x, p_x, qk_c, p_c, lam_params, g_norm.reshape(DIFF_DV, 1))


def _outproj_kernel(ya_ref, yb_ref, yd_ref, w_ref, x_ref, g_ref, gate_ref, gmlp_ref, shift_ref, scale_ref,
                    o_ref, h_ref):
    wa, wb = ya_ref.shape[2], yb_ref.shape[2]
    for r in range(x_ref.shape[1] // ROW_GROUP):
        rows = slice(r * ROW_GROUP, (r + 1) * ROW_GROUP)
        y = jnp.dot(ya_ref[0, rows, :], w_ref[0:wa, :], preferred_element_type=F32)
        y = y + jnp.dot(yb_ref[0, rows, :], w_ref[wa:wa + wb, :], preferred_element_type=F32)
        y = y + jnp.dot(yd_ref[0, rows, :], w_ref[wa + wb:, :], preferred_element_type=F32)
        x_new = x_ref[0, rows, :] + gate_ref[0] * (_rms(y) * g_ref[...])
        o_ref[0, rows, :] = x_new
        h_ref[0, rows, :] = _modulate(x_new, gmlp_ref, shift_ref, scale_ref)


def _outproj(ya, yb, yd, w, x, g_post, g_mlp, mods, mod_row0, tm):
    b, t, d = x.shape

    def mod(comp):
        return pl.BlockSpec((1, 1, d), lambda bi, i: ((bi + mod_row0) * 6 + comp, 0, 0))

    row = pl.BlockSpec((1, d), lambda bi, i: (0, 0))
    tile = pl.BlockSpec((1, tm, d), lambda bi, i: (bi, i, 0))
    return pl.pallas_call(
        _outproj_kernel,
        out_shape=(jax.ShapeDtypeStruct((b, t, d), F32), jax.ShapeDtypeStruct((b, t, d), BF16)),
        grid=(b, t // tm),
        in_specs=[
            pl.BlockSpec((1, tm, ya.shape[2]), lambda bi, i: (bi, i, 0)),
            pl.BlockSpec((1, tm, yb.shape[2]), lambda bi, i: (bi, i, 0)),
            pl.BlockSpec((1, tm, yd.shape[2]), lambda bi, i: (bi, i, 0)),
            pl.BlockSpec(w.shape, lambda bi, i: (0, 0)),
            tile, row, mod(2), row, mod(3), mod(4),
        ],
        out_specs=(tile, tile),
        compiler_params=_params("parallel", "parallel"),
        name="outproj",
    )(ya, yb, yd, w, x, g_post, mods, g_mlp, mods, mods)


def _mlp_kernel(x_ref, h_ref, w1_ref, w2_ref, gpost_ref, gate_ref, *rest, emit_next):
    if emit_next:
        gnext_ref, shift_ref, scale_ref, o_ref, hnext_ref, acc = rest
    else:
        o_ref, acc = rest
    k = pl.program_id(2)

    @pl.when(k == 0)
    def _():
        acc[...] = jnp.zeros(acc.shape, F32)

    def hidden_tile(rows):
        u = jnp.maximum(jnp.dot(h_ref[0, rows, :], w1_ref[...], preferred_element_type=F32), 0.0)
        return jnp.dot((u * u).astype(BF16), w2_ref[...], preferred_element_type=F32)

    last = pl.num_programs(2) - 1

    @pl.when(k < last)
    def _():
        acc[...] += hidden_tile(slice(None))

    @pl.when(k == last)
    def _():
        for r in range(acc.shape[0] // ROW_GROUP):
            rows = slice(r * ROW_GROUP, (r + 1) * ROW_GROUP)
            y = acc[rows, :] + hidden_tile(rows)
            x_new = x_ref[0, rows, :] + gate_ref[0] * (_rms(y) * gpost_ref[...])
            o_ref[0, rows, :] = x_new
            if emit_next:
                hnext_ref[0, rows, :] = _modulate(x_new, gnext_ref, shift_ref, scale_ref)


def _mlp(x, h, gpost, mods, mod_row0, w1, w2, tm, th, g_next=None, mods_next=None):
    b, t, d = x.shape
    hid = w1.shape[1]
    emit_next = g_next is not None

    def mod(comp):
        return pl.BlockSpec((1, 1, d), lambda bi, i, k: ((bi + mod_row0) * 6 + comp, 0, 0))

    row = pl.BlockSpec((1, d), lambda bi, i, k: (0, 0))
    tile = pl.BlockSpec((1, tm, d), lambda bi, i, k: (bi, i, 0))
    in_specs = [
        tile, tile,
        pl.BlockSpec((d, th), lambda bi, i, k: (0, k)),
        pl.BlockSpec((th, d), lambda bi, i, k: (k, 0)),
        row, mod(5),
    ]
    args = [x, h, w1, w2, gpost, mods]
    out_shape = [jax.ShapeDtypeStruct((b, t, d), F32)]
    out_specs = [tile]
    if emit_next:
        in_specs += [row, mod(0), mod(1)]
        args += [g_next, mods_next, mods_next]
        out_shape.append(jax.ShapeDtypeStruct((b, t, d), BF16))
        out_specs.append(tile)
    out = pl.pallas_call(
        functools.partial(_mlp_kernel, emit_next=emit_next),
        out_shape=tuple(out_shape),
        grid=(b, t // tm, hid // th),
        in_specs=in_specs,
        out_specs=tuple(out_specs),
        scratch_shapes=[pltpu.VMEM((tm, d), F32)],
        compiler_params=_params("parallel", "parallel", "arbitrary"),
        name="mlp",
    )(*args)
    return out if emit_next else (out[0], None)


def _rope_tables(t):
    tok = jnp.arange(t, dtype=jnp.int32)
    row = (tok // GRID_W).astype(F32)
    col = (tok % GRID_W).astype(F32)
    n_freq = DIFF_DK // 4
    inv = ROPE_BASE ** (-jnp.arange(n_freq, dtype=F32) / n_freq)
    ang = jnp.concatenate([row[:, None] * inv, col[:, None] * inv], axis=-1)
    cos, sin = jnp.cos(ang), jnp.sin(ang)
    reps = LANES // DIFF_DK
    return (jnp.tile(jnp.concatenate([cos, cos], axis=-1), (1, reps)),
            jnp.tile(jnp.concatenate([-sin, sin], axis=-1), (1, reps)))


def _pick(n, pref):
    return pref if n % pref == 0 else n


def kernel(x, c, ctx, c_ctx, w_ada, b_ada, g_pre_mix, g_post_mix, g_pre_mlp, g_post_mlp, w_in, w_out,
           na_rel_bias, gla_w_gate, gla_b_gate, gla_g_norm, diff_lambda, diff_g_norm, w_mlp_in, w_mlp_out):
    bsz, t, d = x.shape
    n_ctx = ctx.shape[1]
    depth = w_ada.shape[0]
    na_heads = na_rel_bias.shape[1]
    na_w = na_heads * HEAD_DIM
    gla_kw = GLA_HEADS * GLA_DK
    gla_vw = GLA_HEADS * GLA_DV
    diff_w = w_out.shape[1] - na_w - gla_vw
    diff_heads = diff_w // DIFF_DV
    diff_qk = diff_heads * 2 * DIFF_DK
    rows = t // GRID_W
    assert t % (NA_ROWS_PER_STEP * GRID_W) == 0 and rows >= NA_KEY_ROWS
    assert bsz + 1 <= 8

    o_gla = 3 * na_w
    o_gate = o_gla + 2 * gla_kw + 2 * gla_vw
    o_diff = o_gate + 2 * GLA_GATE_RANK
    gla_col0 = 3 * na_w
    diffv_col0 = gla_col0 + 2 * gla_kw + 2 * gla_vw
    gate_col0 = diffv_col0 + diff_w

    cc = jnp.zeros((8, d), F32).at[:bsz].set(c).at[bsz].set(c_ctx)
    mods_all = _mods(cc, w_ada, b_ada)
    rope_tabs = _rope_tables(t)
    xc = ctx.reshape(1, bsz * n_ctx, d)
    zeros_state = jnp.zeros((bsz, GLA_HEADS, GLA_DV, LANES), F32)

    tm_in = _pick(t, 1024)
    tm_out = _pick(t, 512)
    tm_ctx = _pick(bsz * n_ctx, 512)
    tq = _pick(t, 512)
    tkc = min(2048, t // 2)
    nc_lat = _pick(t // GLA_CHUNK, 16)
    nc_ctx = _pick(n_ctx // GLA_CHUNK, 4)

    mods_of = [mods_all[l].reshape(8 * 6, 1, d) for l in range(depth)]
    h_x = _premix(x, g_pre_mix[0].reshape(1, d), mods_of[0], 0, tm_out)
    h_c = _premix(xc, g_pre_mix[0].reshape(1, d), mods_of[0], bsz, tm_ctx)

    for l in range(depth):
        need_ctx = l < depth - 1
        mods = mods_of[l]
        g_next = g_pre_mix[l + 1].reshape(1, d) if need_ctx else None
        mods_next = mods_of[l + 1] if need_ctx else None
        wl = w_in[l].astype(BF16)
        w_qk = wl[:, o_diff:o_diff + 2 * diff_qk]
        w_rest = jnp.concatenate(
            [wl[:, :o_gate], wl[:, o_diff + 2 * diff_qk:], wl[:, o_gate:o_diff],
             jnp.zeros((d, LANES - 2 * GLA_GATE_RANK), BF16)], axis=1)
        w_o = w_out[l].astype(BF16)
        w1 = w_mlp_in[l].astype(BF16)
        w2 = w_mlp_out[l].astype(BF16)
        g2 = g_post_mix[l].reshape(1, d)
        g3 = g_pre_mlp[l].reshape(1, d)
        g4 = g_post_mlp[l].reshape(1, d)
        wg = jnp.zeros((2, LANES, gla_kw), F32)
        wg = wg.at[0, :GLA_GATE_RANK].set(gla_w_gate[l, 0]).at[1, GLA_GATE_RANK:2 * GLA_GATE_RANK].set(gla_w_gate[l, 1])
        wg = wg.astype(BF16)
        bg = gla_b_gate[l].reshape(2, 1, gla_kw)
        gla_gn = gla_g_norm[l].reshape(1, GLA_DV)
        diff_gn = diff_g_norm[l].reshape(1, DIFF_DV)
        lam_init = 0.8 - 0.6 * math.exp(-0.3 * l)

        qk_x = _inproj(h_x, w_qk, tm_in, diff_qk, rope_tabs, 1, DIFF_Q_SCALE)
        p_x = _inproj(h_x, w_rest, tm_in, w_rest.shape[1] // 3)
        qk_c = _inproj(h_c, w_qk, tm_ctx, diff_qk, None, 1, DIFF_Q_SCALE).reshape(bsz, n_ctx, -1)
        p_c = _inproj(h_c, w_rest, tm_ctx, w_rest.shape[1] // 3).reshape(bsz, n_ctx, -1)

        ya = _na_attention(p_x, p_c, _na_bias_table(na_rel_bias[l], rows))

        of_c, s_f = _gla_sweep(p_c, wg[0], bg[0], zeros_state, nc_ctx, gla_col0, gate_col0, False)
        yb_c, s_b = _gla_sweep(p_c, wg[1], bg[1], zeros_state, nc_ctx, gla_col0, gate_col0, True, of_c, gla_gn)
        of_x, _ = _gla_sweep(p_x, wg[0], bg[0], s_f, nc_lat, gla_col0, gate_col0, False)
        yb, _ = _gla_sweep(p_x, wg[1], bg[1], s_b, nc_lat, gla_col0, gate_col0, True, of_x, gla_gn)

        yd = _diff_attention(qk_x, p_x, qk_c, p_c, diff_lambda[l], diff_gn, lam_init, tq, tkc, diffv_col0)

        x, hm_x = _outproj(ya, yb, yd, w_o, x, g2, g3, mods, 0, tm_out)
        x, h_x = _mlp(x, hm_x, g4, mods, 0, w1, w2, tm_out, MLP_HIDDEN_TILE, g_next, mods_next)

        if need_ctx:
            ya_c = _ctx_attention(p_c, na_heads // 2)
            yd_c = _diff_attention_ctx(qk_c, p_c, diff_lambda[l], diff_gn, lam_init, diffv_col0)
            flat = lambda z: z.reshape(1, bsz * n_ctx, -1)
            xc, hm_c = _outproj(flat(ya_c), flat(yb_c), flat(yd_c), w_o, xc, g2, g3, mods, bsz, tm_ctx)
            xc, h_c = _mlp(xc, hm_c, g4, mods, bsz, w1, w2, tm_ctx, MLP_HIDDEN_TILE, g_next, mods_next)
    return x
```
